```python
import jax, jax.numpy as jnp
from jax import lax
import numpy as np

D_MODEL = 4096
BATCH = 2
SEQ = 8192
DEPTH = 1

CHUNK = 64
D_FF = ((8 * D_MODEL // 3 + 127) // 128) * 128
MACARON_WEIGHT = 0.5
NORM_EPS = 1e-6

D_HGRN = D_MODEL // 2
HGRN_EXPAND = 128
HGRN_HEADS = D_HGRN // HGRN_EXPAND
HGRN_DK = HGRN_EXPAND
HGRN_DV = D_HGRN // HGRN_HEADS

D_ATTN = D_MODEL // 2
ATTN_HEADS = 16
ATTN_HEAD_DIM = D_ATTN // ATTN_HEADS
LEFT_CHUNKS = 8
BAND = (LEFT_CHUNKS + 1) * CHUNK
MAX_REL = 256

IN_SIZES = (D_HGRN, D_HGRN, D_HGRN, D_HGRN, D_ATTN, D_ATTN, D_ATTN, D_MODEL, D_MODEL)
IN_COLS = int(sum(IN_SIZES))
IN_SPLITS = tuple(int(s) for s in np.cumsum(IN_SIZES)[:-1])

kernel_name = "hybrid_hgrn2_chunkattn_macaron_sandwich"


def rmsnorm(x, g):
    xf = x.astype(jnp.float32)
    y = xf * lax.rsqrt(jnp.mean(xf * xf, axis=-1, keepdims=True) + NORM_EPS)
    return (y * g.astype(jnp.float32)).astype(x.dtype)


def swiglu_half_step(x, pre_g, post_g, w1, w3, w2):
    h = rmsnorm(x, pre_g)
    y = (jax.nn.silu(h @ w1) * (h @ w3)) @ w2
    return x + MACARON_WEIGHT * rmsnorm(y, post_g)


def hgrn2_mixer(q_raw, f_raw, i_raw, g_raw, lb, norm_g):
    B, S, _ = q_raw.shape
    nc = S // CHUNK
    dtype = q_raw.dtype

    def to_chunks(t, d):
        return t.reshape(B, nc, CHUNK, HGRN_HEADS, d).transpose(1, 0, 3, 2, 4).astype(jnp.float32)

    f = lb + (1.0 - lb) * jax.nn.sigmoid(f_raw.astype(jnp.float32))
    q = to_chunks(jax.nn.silu(q_raw.astype(jnp.float32)), HGRN_DK)
    k = to_chunks(1.0 - f, HGRN_DK)
    logf = to_chunks(jnp.log(f), HGRN_DK)
    v = to_chunks(i_raw, HGRN_DV)
    tri = jnp.tril(jnp.ones((CHUNK, CHUNK), dtype=bool))[:, :, None]

    def step(state, xs):
        qc, kc, vc, lfc = xs
        a = jnp.cumsum(lfc, axis=2)
        diff = a[:, :, :, None, :] - a[:, :, None, :, :]
        decay = jnp.exp(jnp.where(tri, diff, -jnp.inf))
        scores = jnp.einsum('bhtsk,bhsk->bhts', decay * qc[:, :, :, None, :], kc)
        o = jnp.einsum('bhts,bhsv->bhtv', scores, vc)
        o = o + jnp.einsum('bhtk,bhkv->bhtv', qc * jnp.exp(a), state)
        a_last = a[:, :, -1:, :]
        new_state = jnp.exp(a_last[:, :, 0, :])[..., None] * state + jnp.einsum(
            'bhsk,bhsv->bhkv', kc * jnp.exp(a_last - a), vc)
        return new_state, o

    s0 = jnp.zeros((B, HGRN_HEADS, HGRN_DK, HGRN_DV), jnp.float32)
    _, o = lax.scan(step, s0, (q, k, v, logf))
    o = o.transpose(1, 0, 3, 2, 4).reshape(B, S, HGRN_HEADS, HGRN_DV)
    o = o * lax.rsqrt(jnp.mean(o * o, axis=-1, keepdims=True) + NORM_EPS)
    o = o * norm_g.astype(jnp.float32).reshape(HGRN_HEADS, HGRN_DV)
    o = o.reshape(B, S, D_HGRN) * jax.nn.silu(g_raw.astype(jnp.float32))
    return o.astype(dtype)


def chunk_band_attention(q_raw, k_raw, v_raw, rel_bias):
    B, S, _ = q_raw.shape
    nc = S // CHUNK
    pad = LEFT_CHUNKS * CHUNK

    def heads(t):
        return t.reshape(B, S, ATTN_HEADS, ATTN_HEAD_DIM).transpose(0, 2, 1, 3)

    q = heads(q_raw) * (ATTN_HEAD_DIM ** -0.5)
    k = jnp.pad(heads(k_raw), ((0, 0), (0, 0), (pad, 0), (0, 0)))
    v = jnp.pad(heads(v_raw), ((0, 0), (0, 0), (pad, 0), (0, 0)))
    q_chunks = q.reshape(B, ATTN_HEADS, nc, CHUNK, ATTN_HEAD_DIM).transpose(2, 0, 1, 3, 4)

    qi = jnp.arange(CHUNK)[:, None]
    kj = jnp.arange(BAND)[None, :]
    rel = qi - kj + pad
    bias = rel_bias.astype(jnp.float32)[:, jnp.clip(rel, -MAX_REL, MAX_REL) + MAX_REL]

    def one_chunk(args):
        c, q_blk = args
        start = c * CHUNK
        k_band = lax.dynamic_slice_in_dim(k, start, BAND, axis=2)
        v_band = lax.dynamic_slice_in_dim(v, start, BAND, axis=2)
        s = jnp.einsum('bhqd,bhkd->bhqk', q_blk, k_band).astype(jnp.float32) + bias
        s = jnp.where(kj >= pad - start, s, -jnp.inf)
        p = jax.nn.softmax(s, axis=-1).astype(v_band.dtype)
        return jnp.einsum('bhqk,bhkd->bhqd', p, v_band)

    out = lax.map(one_chunk, (jnp.arange(nc, dtype=jnp.int32), q_chunks))
    return out.transpose(1, 0, 3, 2, 4).reshape(B, S, D_ATTN)


def setup_inputs(seed: int = 0) -> dict:
    key = jax.random.key(seed)
    ks = jax.random.split(key, 20)
    f32 = jnp.float32

    def w(k, shape, fan_in):
        return jax.random.normal(k, shape, f32) * (fan_in ** -0.5)

    def gain(k, shape):
        return 1.0 + 0.1 * jax.random.normal(k, shape, f32)

    return {
        "x": jax.random.normal(ks[0], (BATCH, SEQ, D_MODEL), f32),
        "ffn1_pre_g": gain(ks[1], (DEPTH, D_MODEL)),
        "ffn1_post_g": gain(ks[2], (DEPTH, D_MODEL)),
        "ffn1_w1": w(ks[3], (DEPTH, D_MODEL, D_FF), D_MODEL),
        "ffn1_w3": w(ks[4], (DEPTH, D_MODEL, D_FF), D_MODEL),
        "ffn1_w2": w(ks[5], (DEPTH, D_FF, D_MODEL), D_FF),
        "mix_pre_g": gain(ks[6], (DEPTH, D_MODEL)),
        "mix_post_g": gain(ks[7], (DEPTH, D_MODEL)),
        "w_in": w(ks[8], (DEPTH, D_MODEL, IN_COLS), D_MODEL),
        "b_gate": 0.1 * jax.random.normal(ks[9], (DEPTH, 2, D_MODEL), f32),
        "hgrn_lb_logits": 0.5 * jax.random.normal(ks[10], (DEPTH + 1, D_HGRN), f32),
        "hgrn_norm_g": gain(ks[11], (DEPTH, D_HGRN)),
        "rel_bias": 0.2 * jax.random.normal(ks[12], (DEPTH, ATTN_HEADS, 2 * MAX_REL + 1), f32),
        "w_up_a": w(ks[13], (DEPTH, D_HGRN, D_MODEL), D_HGRN),
        "w_up_b": w(ks[14], (DEPTH, D_ATTN, D_MODEL), D_ATTN),
        "w_out": w(ks[15], (DEPTH, D_MODEL, D_MODEL), D_MODEL),
        "ffn2_pre_g": gain(ks[16], (DEPTH, D_MODEL)),
        "ffn2_post_g": gain(ks[17], (DEPTH, D_MODEL)),
        "ffn2_w1": w(ks[18], (DEPTH, D_MODEL, D_FF), D_MODEL),
        "ffn2_w3": w(ks[19], (DEPTH, D_MODEL, D_FF), D_MODEL),
        "ffn2_w2": w(jax.random.fold_in(key, 99), (DEPTH, D_FF, D_MODEL), D_FF),
    }


def reference(x, ffn1_pre_g, ffn1_post_g, ffn1_w1, ffn1_w3, ffn1_w2,
              mix_pre_g, mix_post_g, w_in, b_gate, hgrn_lb_logits, hgrn_norm_g,
              rel_bias, w_up_a, w_up_b, w_out,
              ffn2_pre_g, ffn2_post_g, ffn2_w1, ffn2_w3, ffn2_w2):
    lower_bounds = jnp.cumsum(jax.nn.softmax(hgrn_lb_logits.astype(jnp.float32), axis=0), axis=0)
    for layer in range(DEPTH):
        x = swiglu_half_step(x, ffn1_pre_g[layer], ffn1_post_g[layer],
                             ffn1_w1[layer], ffn1_w3[layer], ffn1_w2[layer])
        h = rmsnorm(x, mix_pre_g[layer])
        proj = h @ w_in[layer]
        qa, fa, ia, ga, qb, kb, vb, gate_a, gate_b = jnp.split(proj, IN_SPLITS, axis=-1)
        y_a = hgrn2_mixer(qa, fa, ia, ga, lower_bounds[layer], hgrn_norm_g[layer]) @ w_up_a[layer]
        y_b = chunk_band_attention(qb, kb, vb, rel_bias[layer]) @ w_up_b[layer]
        g_a = jax.nn.sigmoid(gate_a + b_gate[layer, 0])
        g_b = jax.nn.sigmoid(gate_b + b_gate[layer, 1])
        y = (g_a * y_a + g_b * y_b) @ w_out[layer]
        x = x + rmsnorm(y, mix_post_g[layer])
        x = swiglu_half_step(x, ffn2_pre_g[layer], ffn2_post_g[layer],
                             ffn2_w1[layer], ffn2_w3[layer], ffn2_w2[layer])
    return x
```

```python
import functools

import numpy as np
import jax
import jax.numpy as jnp
from jax import lax
from jax.experimental import pallas as pl
from jax.experimental.pallas import tpu as pltpu

F32 = jnp.float32
BF16 = jnp.bfloat16

NORM_EPS = 1e-6
MACARON_WEIGHT = 0.5
CHUNK = 64
HEAD_DIM = 128
LEFT_CHUNKS = 8
MAX_REL = 256
MASK_VALUE = -1e30

V7X_VMEM_BYTES = 64 * 1024 * 1024
VMEM_LIMIT_CAP = V7X_VMEM_BYTES - 6 * 1024 * 1024

FF_PAD_MULTIPLE = 512


def _vmem_limit(pipelined_bytes, resident_bytes=0):
    est = 2 * pipelined_bytes + resident_bytes + 8 * 1024 * 1024
    return int(min(max(est, 32 * 1024 * 1024), VMEM_LIMIT_CAP))


def _nbytes(shape, dtype):
    return int(np.prod(shape)) * jnp.dtype(dtype).itemsize


def _dot(a, b):
    return jnp.dot(a, b, preferred_element_type=F32)


def _dot_nt(a, b):
    return lax.dot_general(a, b, (((1,), (1,)), ((), ())), preferred_element_type=F32)


def _dot_tn(a, b):
    return lax.dot_general(a, b, (((0,), (0,)), ((), ())), preferred_element_type=F32)


def _silu(x):
    return x * jax.nn.sigmoid(x)


def _norm_cast_kernel(x_ref, g_ref, o_ref):
    x = x_ref[...]
    inv = lax.rsqrt(jnp.mean(x * x, axis=-1, keepdims=True) + NORM_EPS)
    o_ref[...] = (x * inv * g_ref[...]).astype(o_ref.dtype)


def norm_cast(x, g, *, tm=256):
    t, d = x.shape
    return pl.pallas_call(
        _norm_cast_kernel,
        grid=(t // tm,),
        in_specs=[pl.BlockSpec((tm, d), lambda i: (i, 0)),
                  pl.BlockSpec((1, d), lambda i: (0, 0))],
        out_specs=pl.BlockSpec((tm, d), lambda i: (i, 0)),
        out_shape=jax.ShapeDtypeStruct((t, d), BF16),
        compiler_params=pltpu.CompilerParams(
            dimension_semantics=("parallel",),
            vmem_limit_bytes=_vmem_limit(_nbytes((tm, d), F32) + _nbytes((tm, d), BF16),
                                         3 * _nbytes((tm, d), F32))),
        name="norm_cast",
    )(x, g.reshape(1, d))


def _glu_up_kernel(h_ref, w1_ref, w3_ref, o_ref):
    h = h_ref[...]
    a = _dot(h, w1_ref[...])
    b = _dot(h, w3_ref[...])
    o_ref[...] = (_silu(a) * b).astype(o_ref.dtype)


def glu_up(h, w1, w3, *, tm=1024, tn=256):
    t, d = h.shape
    f = w1.shape[1]
    return pl.pallas_call(
        _glu_up_kernel,
        grid=(t // tm, f // tn),
        in_specs=[pl.BlockSpec((tm, d), lambda i, j: (i, 0)),
                  pl.BlockSpec((d, tn), lambda i, j: (0, j)),
                  pl.BlockSpec((d, tn), lambda i, j: (0, j))],
        out_specs=pl.BlockSpec((tm, tn), lambda i, j: (i, j)),
        out_shape=jax.ShapeDtypeStruct((t, f), BF16),
        compiler_params=pltpu.CompilerParams(
            dimension_semantics=("parallel", "arbitrary"),
            vmem_limit_bytes=_vmem_limit(
                _nbytes((tm, d), BF16) + 2 * _nbytes((d, tn), BF16) + _nbytes((tm, tn), BF16),
                4 * _nbytes((tm, tn), F32))),
        name="glu_up",
    )(h, w1, w3)


def _down_resnorm_kernel(a_ref, w_ref, x_ref, gpost_ref, gnext_ref, xo_ref, *maybe_h_ref,
                         scale, n_chunk, row_chunk):
    k = pl.program_id(1)
    d = xo_ref.shape[1]
    a = a_ref[...]

    @pl.when(k == 0)
    def _():
        for n in range(0, d, n_chunk):
            xo_ref[:, n:n + n_chunk] = _dot(a, w_ref[:, n:n + n_chunk])

    @pl.when(k > 0)
    def _():
        for n in range(0, d, n_chunk):
            xo_ref[:, n:n + n_chunk] += _dot(a, w_ref[:, n:n + n_chunk])

    @pl.when(k == pl.num_programs(1) - 1)
    def _():
        gpost = gpost_ref[...]
        gnext = gnext_ref[...]

        def body(r, carry):
            rows = pl.ds(pl.multiple_of(r * row_chunk, row_chunk), row_chunk)
            y = xo_ref[rows, :]
            inv = lax.rsqrt(jnp.mean(y * y, axis=-1, keepdims=True) + NORM_EPS)
            xn = x_ref[rows, :] + scale * (y * inv * gpost)
            xo_ref[rows, :] = xn
            if maybe_h_ref:
                inv2 = lax.rsqrt(jnp.mean(xn * xn, axis=-1, keepdims=True) + NORM_EPS)
                maybe_h_ref[0][rows, :] = (xn * inv2 * gnext).astype(BF16)
            return carry

        lax.fori_loop(0, xo_ref.shape[0] // row_chunk, body, 0)


def down_resnorm(a, w, x, g_post, g_next, *, scale, emit_next, tm=512, tk=512):
    t, kdim = a.shape
    d = w.shape[1]
    out_shape = [jax.ShapeDtypeStruct((t, d), F32)]
    out_specs = [pl.BlockSpec((tm, d), lambda i, k: (i, 0))]
    if emit_next:
        out_shape.append(jax.ShapeDtypeStruct((t, d), BF16))
        out_specs.append(pl.BlockSpec((tm, d), lambda i, k: (i, 0)))
    pipelined = (_nbytes((tm, tk), BF16) + _nbytes((tk, d), BF16) + 2 * _nbytes((tm, d), F32)
                 + (_nbytes((tm, d), BF16) if emit_next else 0))
    outs = pl.pallas_call(
        functools.partial(_down_resnorm_kernel, scale=scale, n_chunk=512, row_chunk=64),
        grid=(t // tm, kdim // tk),
        in_specs=[pl.BlockSpec((tm, tk), lambda i, k: (i, k)),
                  pl.BlockSpec((tk, d), lambda i, k: (k, 0)),
                  pl.BlockSpec((tm, d), lambda i, k: (i, 0)),
                  pl.BlockSpec((1, d), lambda i, k: (0, 0)),
                  pl.BlockSpec((1, d), lambda i, k: (0, 0))],
        out_specs=out_specs,
        out_shape=out_shape,
        compiler_params=pltpu.CompilerParams(
            dimension_semantics=("parallel", "arbitrary"),
            vmem_limit_bytes=_vmem_limit(pipelined, 0)),
        name="down_resnorm",
    )(a, w, x, g_post.reshape(1, d), g_next.reshape(1, d))
    return outs if emit_next else (outs[0], None)


def _proj_kernel(*refs, n_seg, n_aux, epilogue):
    h_ref = refs[0]
    w_refs = refs[1:1 + n_seg]
    aux_refs = refs[1 + n_seg:1 + n_seg + n_aux]
    out_refs = refs[1 + n_seg + n_aux:]
    h = h_ref[...]
    ys = [_dot(h, w_ref[...]) for w_ref in w_refs]
    outs = epilogue(ys, [r[...] for r in aux_refs])
    for o_ref, o in zip(out_refs, outs):
        o_ref[...] = o.astype(o_ref.dtype)


def proj(h, w_in, col_starts, width, aux, epilogue, out_dtypes, *, tm=1024, tn=256):
    t, d = h.shape
    n_seg = len(col_starts)

    def w_spec(start):
        off = start // tn
        return pl.BlockSpec((d, tn), lambda i, j: (0, off + j))

    pipelined = (_nbytes((tm, d), BF16) + n_seg * _nbytes((d, tn), BF16)
                 + sum(_nbytes((tm, tn), dt) for dt in out_dtypes))
    return pl.pallas_call(
        functools.partial(_proj_kernel, n_seg=n_seg, n_aux=len(aux), epilogue=epilogue),
        grid=(t // tm, width // tn),
        in_specs=([pl.BlockSpec((tm, d), lambda i, j: (i, 0))]
                  + [w_spec(s) for s in col_starts]
                  + [pl.BlockSpec((1, tn), lambda i, j: (0, j)) for _ in aux]),
        out_specs=[pl.BlockSpec((tm, tn), lambda i, j: (i, j)) for _ in out_dtypes],
        out_shape=[jax.ShapeDtypeStruct((t, width), dt) for dt in out_dtypes],
        compiler_params=pltpu.CompilerParams(
            dimension_semantics=("parallel", "arbitrary"),
            vmem_limit_bytes=_vmem_limit(pipelined, (n_seg + 4) * _nbytes((tm, tn), F32))),
        name="proj",
    )(h, *([w_in] * n_seg), *[a.reshape(1, width) for a in aux])


def _hgrn_epilogue(ys, aux):
    q, fr, i, g = ys
    lb, = aux
    f = lb + (1.0 - lb) * jax.nn.sigmoid(fr)
    return _silu(q), jnp.log(f), 1.0 - f, i, _silu(g)


def _attn_epilogue(ys, aux):
    q, k, v = ys
    return q * (HEAD_DIM ** -0.5), k, v


def _gate_epilogue(ys, aux):
    ga, gb = ys
    ba, bb = aux
    return jax.nn.sigmoid(ga + ba), jax.nn.sigmoid(gb + bb)


def _hgrn_tables():
    c = CHUNK
    t = np.arange(c)[:, None]
    u = np.arange(c)[None, :]
    blocks = []
    for p in range(7):
        hb = 1 << p
        blocks.append((u >= (t // hb) * hb) & (u <= t))
    for p in range(7):
        hb = 1 << p
        blocks.append((u > t) & (u <= (t // hb) * hb + hb - 1))
    return np.concatenate(blocks, axis=0).astype(np.float32)


def _hgrn_kernel(sel_ref, qs_ref, lf_ref, kk_ref, v_ref, gs_ref, ng_ref, o_ref, st_ref, *, heads_per_block):
    c = CHUNK
    dk = HEAD_DIM

    @pl.when(pl.program_id(2) == 0)
    def _():
        st_ref[...] = jnp.zeros_like(st_ref)

    sel = sel_ref[...]
    ti = lax.broadcasted_iota(jnp.int32, (c, c), 0)
    si = lax.broadcasted_iota(jnp.int32, (c, c), 1)
    diag_mask = ti == si
    level_masks = []
    for p in range(6):
        hb = 1 << p
        level_masks.append(((ti // (2 * hb)) == (si // (2 * hb)))
                           & (((ti // hb) % 2) == 1) & (((si // hb) % 2) == 0))

    def chunk_body(ci, carry):
        rows = pl.ds(pl.multiple_of(ci * c, c), c)
        lf = lf_ref[rows, :]
        hi = lf.astype(BF16)
        r1 = lf - hi.astype(F32)
        mid = r1.astype(BF16)
        lo = (r1 - mid.astype(F32)).astype(BF16)
        args = _dot(sel, hi) + _dot(sel, mid) + _dot(sel, lo)
        e = jnp.exp(jnp.minimum(args, 0.0))
        for hh in range(heads_per_block):
            cols = slice(hh * dk, (hh + 1) * dk)
            q = qs_ref[rows, cols].astype(F32)
            k = kk_ref[rows, cols].astype(F32)
            v = v_ref[rows, cols]
            qb = qs_ref[rows, cols]
            kb = kk_ref[rows, cols]
            scores = jnp.where(diag_mask, _dot_nt(qb, kb), 0.0)
            for p in range(6):
                qt = (q * e[p * c:(p + 1) * c, cols]).astype(BF16)
                kt = (k * e[(7 + p) * c:(8 + p) * c, cols]).astype(BF16)
                scores = scores + jnp.where(level_masks[p], _dot_nt(qt, kt), 0.0)
            e_in = e[6 * c:7 * c, cols]
            e_out = e[13 * c:14 * c, cols]
            st = st_ref[hh]
            o = _dot(scores.astype(BF16), v)
            o = o + _dot_nt((q * e_in).astype(BF16), st.astype(BF16))
            st_ref[hh] = st * e_in[c - 1:c, :] + _dot_tn(v, (k * e_out).astype(BF16))
            inv = lax.rsqrt(jnp.mean(o * o, axis=-1, keepdims=True) + NORM_EPS)
            o = o * inv * ng_ref[:, cols] * gs_ref[rows, cols].astype(F32)
            o_ref[rows, cols] = o.astype(o_ref.dtype)
        return carry

    lax.fori_loop(0, qs_ref.shape[0] // c, chunk_body, 0)


def hgrn(qs, lf, kk, v, gs, norm_g, *, batch, seq, block_len=512, heads_per_block=2):
    t, width = qs.shape
    bw = heads_per_block * HEAD_DIM
    n_l = seq // block_len
    sel = jnp.asarray(_hgrn_tables(), BF16)
    tok = pl.BlockSpec((block_len, bw), lambda b, h, l: (b * n_l + l, h))
    pipelined = 4 * _nbytes((block_len, bw), BF16) + _nbytes((block_len, bw), F32)
    return pl.pallas_call(
        functools.partial(_hgrn_kernel, heads_per_block=heads_per_block),
        grid=(batch, width // bw, n_l),
        in_specs=[pl.BlockSpec(sel.shape, lambda b, h, l: (0, 0)),
                  tok, tok, tok, tok, tok,
                  pl.BlockSpec((1, bw), lambda b, h, l: (0, h))],
        out_specs=tok,
        out_shape=jax.ShapeDtypeStruct((t, width), BF16),
        scratch_shapes=[pltpu.VMEM((heads_per_block, HEAD_DIM, HEAD_DIM), F32)],
        compiler_params=pltpu.CompilerParams(
            dimension_semantics=("parallel", "parallel", "arbitrary"),
            vmem_limit_bytes=_vmem_limit(pipelined, 0)),
        name="hgrn",
    )(sel, qs, lf, kk, v, gs, norm_g.reshape(1, width))


ATTN_Q_BLOCK = 2 * CHUNK
ATTN_PAD = LEFT_CHUNKS * CHUNK
ATTN_WINDOW = ATTN_PAD + ATTN_Q_BLOCK


def _attn_bias_index():
    i = np.arange(ATTN_Q_BLOCK)[:, None]
    j = np.arange(ATTN_WINDOW)[None, :]
    rel = i - j + ATTN_PAD
    idx = np.clip(rel, -MAX_REL, MAX_REL) + MAX_REL
    allowed = (j // CHUNK >= i // CHUNK) & (j // CHUNK <= i // CHUNK + LEFT_CHUNKS)
    return idx, allowed


def _attn_kernel(q_ref, k_ref, v_ref, bias_ref, o_ref, kpad_ref, vpad_ref):
    seq = q_ref.shape[0]
    kpad_ref[0:ATTN_PAD, :] = jnp.zeros((ATTN_PAD, HEAD_DIM), kpad_ref.dtype)
    vpad_ref[0:ATTN_PAD, :] = jnp.zeros((ATTN_PAD, HEAD_DIM), vpad_ref.dtype)
    kpad_ref[ATTN_PAD:, :] = k_ref[...]
    vpad_ref[ATTN_PAD:, :] = v_ref[...]
    bias = bias_ref[0]
    col = lax.broadcasted_iota(jnp.int32, (ATTN_Q_BLOCK, ATTN_WINDOW), 1)

    def body(n, carry):
        start = pl.multiple_of(n * ATTN_Q_BLOCK, ATTN_Q_BLOCK)
        q = q_ref[pl.ds(start, ATTN_Q_BLOCK), :]
        kw = kpad_ref[pl.ds(start, ATTN_WINDOW), :]
        vw = vpad_ref[pl.ds(start, ATTN_WINDOW), :]
        s = _dot_nt(q, kw) + bias
        s = jnp.where(col >= ATTN_PAD - start, s, MASK_VALUE)
        m = jnp.max(s, axis=-1, keepdims=True)
        p = jnp.exp(s - m)
        denom = jnp.sum(p, axis=-1, keepdims=True)
        o = _dot(p.astype(BF16), vw) / denom
        o_ref[pl.ds(start, ATTN_Q_BLOCK), :] = o.astype(o_ref.dtype)
        return carry

    lax.fori_loop(0, seq // ATTN_Q_BLOCK, body, 0)


def band_attn(q, k, v, bias, *, batch, seq):
    t, width = q.shape
    tok = pl.BlockSpec((seq, HEAD_DIM), lambda b, h: (b, h))
    return pl.pallas_call(
        _attn_kernel,
        grid=(batch, width // HEAD_DIM),
        in_specs=[tok, tok, tok,
                  pl.BlockSpec((1, ATTN_Q_BLOCK, ATTN_WINDOW), lambda b, h: (h, 0, 0))],
        out_specs=tok,
        out_shape=jax.ShapeDtypeStruct((t, width), BF16),
        scratch_shapes=[pltpu.VMEM((seq + ATTN_PAD, HEAD_DIM), BF16),
                        pltpu.VMEM((seq + ATTN_PAD, HEAD_DIM), BF16)],
        compiler_params=pltpu.CompilerParams(
            dimension_semantics=("parallel", "parallel"),
            vmem_limit_bytes=_vmem_limit(
                4 * _nbytes((seq, HEAD_DIM), BF16) + _nbytes((ATTN_Q_BLOCK, ATTN_WINDOW), F32),
                2 * _nbytes((seq + ATTN_PAD, HEAD_DIM), BF16))),
        name="band_attn",
    )(q, k, v, bias)


def _mix_up_kernel(oa_ref, ob_ref, wa_ref, wb_ref, ga_ref, gb_ref, o_ref):
    ya = _dot(oa_ref[...], wa_ref[...])
    yb = _dot(ob_ref[...], wb_ref[...])
    o_ref[...] = (ga_ref[...].astype(F32) * ya + gb_ref[...].astype(F32) * yb).astype(o_ref.dtype)


def mix_up(oa, ob, wa, wb, ga, gb, *, tm=1024, tn=512):
    t, kdim = oa.shape
    d = wa.shape[1]
    pipelined = (2 * _nbytes((tm, kdim), BF16) + 2 * _nbytes((kdim, tn), BF16) + 3 * _nbytes((tm, tn), BF16))
    return pl.pallas_call(
        _mix_up_kernel,
        grid=(t // tm, d // tn),
        in_specs=[pl.BlockSpec((tm, kdim), lambda i, j: (i, 0)),
                  pl.BlockSpec((tm, kdim), lambda i, j: (i, 0)),
                  pl.BlockSpec((kdim, tn), lambda i, j: (0, j)),
                  pl.BlockSpec((kdim, tn), lambda i, j: (0, j)),
                  pl.BlockSpec((tm, tn), lambda i, j: (i, j)),
                  pl.BlockSpec((tm, tn), lambda i, j: (i, j))],
        out_specs=pl.BlockSpec((tm, tn), lambda i, j: (i, j)),
        out_shape=jax.ShapeDtypeStruct((t, d), BF16),
        compiler_params=pltpu.CompilerParams(
            dimension_semantics=("parallel", "arbitrary"),
            vmem_limit_bytes=_vmem_limit(pipelined, 4 * _nbytes((tm, tn), F32))),
        name="mix_up",
    )(oa, ob, wa, wb, ga, gb)


def _pad_ff(w, axis):
    f = w.shape[axis]
    pad = (-f) % FF_PAD_MULTIPLE
    if pad == 0:
        return w
    widths = [(0, 0)] * w.ndim
    widths[axis] = (0, pad)
    return jnp.pad(w, widths)


def _ffn(x, h, w1, w3, w2, post_g, next_g, emit_next):
    w1b = _pad_ff(w1.astype(BF16), 1)
    w3b = _pad_ff(w3.astype(BF16), 1)
    w2b = _pad_ff(w2.astype(BF16), 0)
    g = glu_up(h, w1b, w3b)
    return down_resnorm(g, w2b, x, post_g, next_g, scale=MACARON_WEIGHT, emit_next=emit_next)


def kernel(x, ffn1_pre_g, ffn1_post_g, ffn1_w1, ffn1_w3, ffn1_w2, mix_pre_g, mix_post_g, w_in, b_gate,
           hgrn_lb_logits, hgrn_norm_g, rel_bias, w_up_a, w_up_b, w_out,
           ffn2_pre_g, ffn2_post_g, ffn2_w1, ffn2_w3, ffn2_w2):
    batch, seq, d = x.shape
    depth = ffn1_w1.shape[0]
    d_half = d // 2
    lower_bounds = jnp.cumsum(jax.nn.softmax(hgrn_lb_logits.astype(F32), axis=0), axis=0)
    bias_idx, bias_allowed = _attn_bias_index()

    xt = x.reshape(batch * seq, d)
    h = norm_cast(xt, ffn1_pre_g[0])
    for layer in range(depth):
        xt, h = _ffn(xt, h, ffn1_w1[layer], ffn1_w3[layer], ffn1_w2[layer],
                     ffn1_post_g[layer], mix_pre_g[layer], True)

        w_in_b = w_in[layer].astype(BF16)
        qs, lf, kk, vv, gs = proj(h, w_in_b, [0, d_half, 2 * d_half, 3 * d_half], d_half,
                                  [lower_bounds[layer]], _hgrn_epilogue, [BF16, F32, BF16, BF16, BF16])
        qb, kb, vb = proj(h, w_in_b, [4 * d_half, 5 * d_half, 6 * d_half], d_half,
                          [], _attn_epilogue, [BF16, BF16, BF16])
        ga, gb = proj(h, w_in_b, [7 * d_half, 7 * d_half + d], d,
                      [b_gate[layer, 0], b_gate[layer, 1]], _gate_epilogue, [BF16, BF16])

        oa = hgrn(qs, lf, kk, vv, gs, hgrn_norm_g[layer], batch=batch, seq=seq)
        bias = jnp.where(bias_allowed[None], rel_bias[layer].astype(F32)[:, bias_idx], MASK_VALUE)
        ob = band_attn(qb, kb, vb, bias, batch=batch, seq=seq)

        m = mix_up(oa, ob, w_up_a[layer].astype(BF16), w_up_b[layer].astype(BF16), ga, gb)
        last = layer == depth - 1
        next_g = ffn2_pre_g[layer]
        xt, h = down_resnorm(m, w_out[layer].astype(BF16), xt, mix_post_g[layer], next_g,
                             scale=1.0, emit_next=True)
        next_pre = ffn1_pre_g[layer + 1] if not last else ffn2_pre_g[layer]
        xt, h = _ffn(xt, h, ffn2_w1[layer], ffn2_w3[layer], ffn2_w2[layer],
                     ffn2_post_g[layer], next_pre, not last)
    return xt.reshape(batch, seq, d)
```

```python
import functools

import numpy as np
import jax
import jax.numpy as jnp
from jax import lax
from jax.experimental import pallas as pl
from jax.experimental.pallas import tpu as pltpu

F32 = jnp.float32
BF16 = jnp.bfloat16

NORM_EPS = 1e-6
LOG2_E = 1.4426950408889634
MACARON_WEIGHT = 0.5
CHUNK = 64
HEAD_DIM = 128
LEFT_CHUNKS = 8
MAX_REL = 256
MASK_VALUE = -1e30

V7X_VMEM_BYTES = 64 * 1024 * 1024
VMEM_LIMIT_CAP = V7X_VMEM_BYTES - 6 * 1024 * 1024

FF_PAD_MULTIPLE = 512


def _vmem_limit(pipelined_bytes, resident_bytes=0):
    est = 2 * pipelined_bytes + resident_bytes + 8 * 1024 * 1024
    return int(min(max(est, 32 * 1024 * 1024), VMEM_LIMIT_CAP))


def _nbytes(shape, dtype):
    return int(np.prod(shape)) * jnp.dtype(dtype).itemsize


def _dot(a, b):
    return jnp.dot(a, b, preferred_element_type=F32)


def _dot_nt(a, b):
    return lax.dot_general(a, b, (((1,), (1,)), ((), ())), preferred_element_type=F32)


def _dot_tn(a, b):
    return lax.dot_general(a, b, (((0,), (0,)), ((), ())), preferred_element_type=F32)


def _silu(x):
    return x * jax.nn.sigmoid(x)


def _norm_cast_kernel(x_ref, g_ref, o_ref):
    x = x_ref[...]
    inv = lax.rsqrt(jnp.mean(x * x, axis=-1, keepdims=True) + NORM_EPS)
    o_ref[...] = (x * inv * g_ref[...]).astype(o_ref.dtype)


def norm_cast(x, g, *, tm=256):
    t, d = x.shape
    return pl.pallas_call(
        _norm_cast_kernel,
        grid=(t // tm,),
        in_specs=[pl.BlockSpec((tm, d), lambda i: (i, 0)),
                  pl.BlockSpec((1, d), lambda i: (0, 0))],
        out_specs=pl.BlockSpec((tm, d), lambda i: (i, 0)),
        out_shape=jax.ShapeDtypeStruct((t, d), BF16),
        compiler_params=pltpu.CompilerParams(
            dimension_semantics=("parallel",),
            vmem_limit_bytes=_vmem_limit(_nbytes((tm, d), F32) + _nbytes((tm, d), BF16),
                                         3 * _nbytes((tm, d), F32))),
        name="norm_cast",
    )(x, g.reshape(1, d))


def _glu_up_kernel(h_ref, w1_ref, w3_ref, o_ref):
    h = h_ref[...]
    a = _dot(h, w1_ref[...])
    b = _dot(h, w3_ref[...])
    o_ref[...] = (_silu(a) * b).astype(o_ref.dtype)


def glu_up(h, w1, w3, *, tm=1024, tn=256):
    t, d = h.shape
    f = w1.shape[1]
    return pl.pallas_call(
        _glu_up_kernel,
        grid=(t // tm, f // tn),
        in_specs=[pl.BlockSpec((tm, d), lambda i, j: (i, 0)),
                  pl.BlockSpec((d, tn), lambda i, j: (0, j)),
                  pl.BlockSpec((d, tn), lambda i, j: (0, j))],
        out_specs=pl.BlockSpec((tm, tn), lambda i, j: (i, j)),
        out_shape=jax.ShapeDtypeStruct((t, f), BF16),
        compiler_params=pltpu.CompilerParams(
            dimension_semantics=("parallel", "arbitrary"),
            vmem_limit_bytes=_vmem_limit(
                _nbytes((tm, d), BF16) + 2 * _nbytes((d, tn), BF16) + _nbytes((tm, tn), BF16),
                4 * _nbytes((tm, tn), F32))),
        name="glu_up",
    )(h, w1, w3)


def _down_resnorm_kernel(a_ref, w_ref, x_ref, gpost_ref, gnext_ref, xo_ref, *maybe_h_ref,
                         scale, n_chunk, row_chunk):
    k = pl.program_id(1)
    d = xo_ref.shape[1]
    a = a_ref[...]

    @pl.when(k == 0)
    def _():
        for n in range(0, d, n_chunk):
            xo_ref[:, n:n + n_chunk] = _dot(a, w_ref[:, n:n + n_chunk])

    @pl.when(k > 0)
    def _():
        for n in range(0, d, n_chunk):
            xo_ref[:, n:n + n_chunk] += _dot(a, w_ref[:, n:n + n_chunk])

    @pl.when(k == pl.num_programs(1) - 1)
    def _():
        gpost = gpost_ref[...]
        gnext = gnext_ref[...]

        def body(r, carry):
            rows = pl.ds(pl.multiple_of(r * row_chunk, row_chunk), row_chunk)
            y = xo_ref[rows, :]
            inv = lax.rsqrt(jnp.mean(y * y, axis=-1, keepdims=True) + NORM_EPS)
            xn = x_ref[rows, :] + scale * (y * inv * gpost)
            xo_ref[rows, :] = xn
            if maybe_h_ref:
                inv2 = lax.rsqrt(jnp.mean(xn * xn, axis=-1, keepdims=True) + NORM_EPS)
                maybe_h_ref[0][rows, :] = (xn * inv2 * gnext).astype(BF16)
            return carry

        lax.fori_loop(0, xo_ref.shape[0] // row_chunk, body, 0)


def down_resnorm(a, w, x, g_post, g_next, *, scale, emit_next, tm=512, tk=512):
    t, kdim = a.shape
    d = w.shape[1]
    out_shape = [jax.ShapeDtypeStruct((t, d), F32)]
    out_specs = [pl.BlockSpec((tm, d), lambda i, k: (i, 0))]
    if emit_next:
        out_shape.append(jax.ShapeDtypeStruct((t, d), BF16))
        out_specs.append(pl.BlockSpec((tm, d), lambda i, k: (i, 0)))
    pipelined = (_nbytes((tm, tk), BF16) + _nbytes((tk, d), BF16) + 2 * _nbytes((tm, d), F32)
                 + (_nbytes((tm, d), BF16) if emit_next else 0))
    outs = pl.pallas_call(
        functools.partial(_down_resnorm_kernel, scale=scale, n_chunk=512, row_chunk=64),
        grid=(t // tm, kdim // tk),
        in_specs=[pl.BlockSpec((tm, tk), lambda i, k: (i, k)),
                  pl.BlockSpec((tk, d), lambda i, k: (k, 0)),
                  pl.BlockSpec((tm, d), lambda i, k: (i, 0)),
                  pl.BlockSpec((1, d), lambda i, k: (0, 0)),
                  pl.BlockSpec((1, d), lambda i, k: (0, 0))],
        out_specs=out_specs,
        out_shape=out_shape,
        compiler_params=pltpu.CompilerParams(
            dimension_semantics=("parallel", "arbitrary"),
            vmem_limit_bytes=_vmem_limit(pipelined, 0)),
        name="down_resnorm",
    )(a, w, x, g_post.reshape(1, d), g_next.reshape(1, d))
    return outs if emit_next else (outs[0], None)


def _proj_kernel(*refs, n_seg, n_aux, epilogue):
    h_ref = refs[0]
    w_refs = refs[1:1 + n_seg]
    aux_refs = refs[1 + n_seg:1 + n_seg + n_aux]
    out_refs = refs[1 + n_seg + n_aux:]
    h = h_ref[...]
    ys = [_dot(h, w_ref[...]) for w_ref in w_refs]
    outs = epilogue(ys, [r[...] for r in aux_refs])
    for o_ref, o in zip(out_refs, outs):
        o_ref[...] = o.astype(o_ref.dtype)


def proj(h, w_in, col_starts, width, aux, epilogue, out_dtypes, *, tm=1024, tn=256):
    t, d = h.shape
    n_seg = len(col_starts)

    def w_spec(start):
        off = start // tn
        return pl.BlockSpec((d, tn), lambda i, j: (0, off + j))

    pipelined = (_nbytes((tm, d), BF16) + n_seg * _nbytes((d, tn), BF16)
                 + sum(_nbytes((tm, tn), dt) for dt in out_dtypes))
    return pl.pallas_call(
        functools.partial(_proj_kernel, n_seg=n_seg, n_aux=len(aux), epilogue=epilogue),
        grid=(t // tm, width // tn),
        in_specs=([pl.BlockSpec((tm, d), lambda i, j: (i, 0))]
                  + [w_spec(s) for s in col_starts]
                  + [pl.BlockSpec((1, tn), lambda i, j: (0, j)) for _ in aux]),
        out_specs=[pl.BlockSpec((tm, tn), lambda i, j: (i, j)) for _ in out_dtypes],
        out_shape=[jax.ShapeDtypeStruct((t, width), dt) for dt in out_dtypes],
        compiler_params=pltpu.CompilerParams(
            dimension_semantics=("parallel", "arbitrary"),
            vmem_limit_bytes=_vmem_limit(pipelined, (n_seg + 4) * _nbytes((tm, tn), F32))),
        name="proj",
    )(h, *([w_in] * n_seg), *[a.reshape(1, width) for a in aux])


def _hgrn_epilogue(ys, aux):
    q, fr, i, g = ys
    lb, = aux
    f = lb + (1.0 - lb) * jax.nn.sigmoid(fr)
    return _silu(q), jnp.log(f) * LOG2_E, 1.0 - f, i, _silu(g)


def _attn_epilogue(ys, aux):
    q, k, v = ys
    return q * (HEAD_DIM ** -0.5), k, v


def _gate_epilogue(ys, aux):
    ga, gb = ys
    ba, bb = aux
    return jax.nn.sigmoid(ga + ba), jax.nn.sigmoid(gb + bb)


def _lower_half_total(p_hb, level, row8):
    c, w = p_hb.shape
    hb = 1 << level
    if level >= 2:
        blk = min(2 * hb, c)
        return jnp.concatenate(
            [jnp.broadcast_to(p_hb[b * blk + hb - 1:b * blk + hb, :], (blk, w)) for b in range(c // blk)], axis=0)
    x = p_hb.reshape(c // 8, 8, w)
    odd = (row8 & 1) == 1
    if level == 0:
        g = jnp.where(odd, pltpu.roll(x, 1, 1), x)
    else:
        z = jnp.where(odd, x, pltpu.roll(x, 7, 1))
        g = jnp.where((row8 & 2) == 0, z, pltpu.roll(z, 2, 1))
    return g.reshape(c, w)


def _hgrn_kernel(qs_ref, lf_ref, kk_ref, v_ref, gs_ref, ng_ref, o_ref, st_ref, *, heads_per_block,
                 chunks_per_iter):
    c = CHUNK
    dk = HEAD_DIM
    w = qs_ref.shape[1]
    n_levels = 6

    @pl.when(pl.program_id(2) == 0)
    def _():
        st_ref[...] = jnp.zeros_like(st_ref)

    row = lax.broadcasted_iota(jnp.int32, (c, w), 0)
    row8 = lax.broadcasted_iota(jnp.int32, (c // 8, 8, w), 1)
    upper = [((row >> p) & 1) == 1 for p in range(n_levels)]
    ti = lax.broadcasted_iota(jnp.int32, (c, c), 0)
    si = lax.broadcasted_iota(jnp.int32, (c, c), 1)
    diag_mask = ti == si
    level_masks = [((ti >> (p + 1)) == (si >> (p + 1))) & (((ti >> p) & 1) == 1) & (((si >> p) & 1) == 0)
                   for p in range(n_levels)]

    def iter_body(it, carry):
        pre = []
        for cc in range(chunks_per_iter):
            rows = pl.ds(pl.multiple_of((it * chunks_per_iter + cc) * c, c), c)
            p_hb = lf_ref[rows, :]
            e_q, e_k = [], []
            for p in range(n_levels + 1):
                g = _lower_half_total(p_hb, p, row8)
                e_q.append(jnp.exp2(p_hb))
                e_k.append(None if p == 0 else jnp.exp2(g - p_hb))
                if p < n_levels:
                    p_hb = p_hb + jnp.where(upper[p], g, 0.0)
            pre.append((rows, e_q, e_k))
        indep = {}
        for cc, (rows, e_q, e_k) in enumerate(pre):
            for hh in range(heads_per_block):
                cols = slice(hh * dk, (hh + 1) * dk)
                qb = qs_ref[rows, cols]
                kb = kk_ref[rows, cols]
                v = v_ref[rows, cols]
                q = qb.astype(F32)
                k = kb.astype(F32)
                parts = [_dot_nt(qb, kb)]
                for p in range(n_levels):
                    qt = (q * e_q[p][:, cols]).astype(BF16)
                    kt = kb if p == 0 else (k * e_k[p][:, cols]).astype(BF16)
                    parts.append(_dot_nt(qt, kt))
                e_in = e_q[n_levels][:, cols]
                e_out = e_k[n_levels][:, cols]
                kv = _dot_tn(v, (k * e_out).astype(BF16))
                indep[cc, hh] = (parts, kv, (q * e_in).astype(BF16), v, e_in[c - 1:c, :])
        for hh in range(heads_per_block):
            cols = slice(hh * dk, (hh + 1) * dk)
            st = st_ref[hh]
            for cc, (rows, _, _) in enumerate(pre):
                parts, kv, q_in, v, decay_all = indep[cc, hh]
                scores = jnp.where(diag_mask, parts[0], 0.0)
                for p in range(n_levels):
                    scores = jnp.where(level_masks[p], parts[p + 1], scores)
                o = _dot(scores.astype(BF16), v) + _dot_nt(q_in, st.astype(BF16))
                st = st * decay_all + kv
                inv = lax.rsqrt(jnp.mean(o * o, axis=-1, keepdims=True) + NORM_EPS)
                o = o * inv * ng_ref[:, cols] * gs_ref[rows, cols].astype(F32)
                o_ref[rows, cols] = o.astype(o_ref.dtype)
            st_ref[hh] = st
        return carry

    lax.fori_loop(0, qs_ref.shape[0] // (c * chunks_per_iter), iter_body, 0)


def hgrn(qs, lf, kk, v, gs, norm_g, *, batch, seq, block_len=512, heads_per_block=4, chunks_per_iter=2):
    t, width = qs.shape
    bw = heads_per_block * HEAD_DIM
    n_l = seq // block_len
    tok = pl.BlockSpec((block_len, bw), lambda b, h, l: (b * n_l + l, h))
    pipelined = 4 * _nbytes((block_len, bw), BF16) + _nbytes((block_len, bw), F32)
    return pl.pallas_call(
        functools.partial(_hgrn_kernel, heads_per_block=heads_per_block, chunks_per_iter=chunks_per_iter),
        grid=(batch, width // bw, n_l),
        in_specs=[tok, tok, tok, tok, tok,
                  pl.BlockSpec((1, bw), lambda b, h, l: (0, h))],
        out_specs=tok,
        out_shape=jax.ShapeDtypeStruct((t, width), BF16),
        scratch_shapes=[pltpu.VMEM((heads_per_block, HEAD_DIM, HEAD_DIM), F32)],
        compiler_params=pltpu.CompilerParams(
            dimension_semantics=("parallel", "parallel", "arbitrary"),
            vmem_limit_bytes=_vmem_limit(pipelined, 0)),
        name="hgrn",
    )(qs, lf, kk, v, gs, norm_g.reshape(1, width))


ATTN_Q_BLOCK = CHUNK
ATTN_WINDOW = 640
ATTN_LEAD = ATTN_WINDOW - ATTN_Q_BLOCK
ATTN_BIAS_LANES = 768


def _attn_bias_row(rel_bias):
    m = np.arange(ATTN_BIAS_LANES)
    m = np.where(m >= ATTN_WINDOW, m - ATTN_BIAS_LANES, m)
    idx = np.clip(ATTN_LEAD - m, -MAX_REL, MAX_REL) + MAX_REL
    runs, start = [], 0
    for pos in range(1, len(idx) + 1):
        if pos == len(idx) or idx[pos] != idx[pos - 1] - 1:
            runs.append((start, pos))
            start = pos
    pieces = [jnp.flip(rel_bias[:, int(idx[hi - 1]):int(idx[lo]) + 1], axis=1) for lo, hi in runs]
    return jnp.concatenate(pieces, axis=1)


def _attn_kernel(q_ref, k_ref, v_ref, brow_ref, o_ref, kpad_ref, vpad_ref, *, group):
    seq = q_ref.shape[0]
    kpad_ref[0:ATTN_LEAD, :] = jnp.zeros((ATTN_LEAD, HEAD_DIM), kpad_ref.dtype)
    vpad_ref[0:ATTN_LEAD, :] = jnp.zeros((ATTN_LEAD, HEAD_DIM), vpad_ref.dtype)
    kpad_ref[ATTN_LEAD:, :] = k_ref[...]
    vpad_ref[ATTN_LEAD:, :] = v_ref[...]
    brow = jnp.broadcast_to(brow_ref[0], (ATTN_Q_BLOCK, ATTN_BIAS_LANES))
    toep = pltpu.roll(brow, 0, 1, stride=1, stride_axis=0)[:, :ATTN_WINDOW]
    qi = lax.broadcasted_iota(jnp.int32, (ATTN_Q_BLOCK, ATTN_WINDOW), 0)
    col = lax.broadcasted_iota(jnp.int32, (ATTN_Q_BLOCK, ATTN_WINDOW), 1)
    qc = qi // CHUNK
    kc = col // CHUNK
    lead = ATTN_LEAD // CHUNK
    bias = jnp.where((kc >= qc + lead - LEFT_CHUNKS) & (kc <= qc + lead), toep, MASK_VALUE)

    def do_group(base, masked):
        starts = [base + u * ATTN_Q_BLOCK for u in range(group)]
        scores = []
        for start in starts:
            q = q_ref[pl.ds(start, ATTN_Q_BLOCK), :]
            kw = kpad_ref[pl.ds(start, ATTN_WINDOW), :]
            s = _dot_nt(q, kw) + bias
            if masked:
                s = jnp.where(col >= ATTN_LEAD - start, s, MASK_VALUE)
            scores.append(s)
        probs = []
        for s in scores:
            m = jnp.max(s, axis=-1, keepdims=True)
            p = jnp.exp(s - m)
            probs.append((p.astype(BF16), jnp.sum(p, axis=-1, keepdims=True)))
        for start, (p, denom) in zip(starts, probs):
            vw = vpad_ref[pl.ds(start, ATTN_WINDOW), :]
            o = _dot(p, vw) / denom
            o_ref[pl.ds(start, ATTN_Q_BLOCK), :] = o.astype(o_ref.dtype)

    span = group * ATTN_Q_BLOCK
    n_masked = pl.cdiv(ATTN_LEAD, span)
    for g in range(n_masked):
        do_group(g * span, True)

    def body(g, carry):
        do_group(pl.multiple_of(g * span, span), False)
        return carry

    lax.fori_loop(n_masked, seq // span, body, 0)


def band_attn(q, k, v, bias_row, *, batch, seq, group=8):
    t, width = q.shape
    tok = pl.BlockSpec((seq, HEAD_DIM), lambda b, h: (b, h))
    return pl.pallas_call(
        functools.partial(_attn_kernel, group=group),
        grid=(batch, width // HEAD_DIM),
        in_specs=[tok, tok, tok,
                  pl.BlockSpec((1, 1, ATTN_BIAS_LANES), lambda b, h: (h, 0, 0))],
        out_specs=tok,
        out_shape=jax.ShapeDtypeStruct((t, width), BF16),
        scratch_shapes=[pltpu.VMEM((seq + ATTN_LEAD, HEAD_DIM), BF16),
                        pltpu.VMEM((seq + ATTN_LEAD, HEAD_DIM), BF16)],
        compiler_params=pltpu.CompilerParams(
            dimension_semantics=("parallel", "parallel"),
            vmem_limit_bytes=_vmem_limit(
                4 * _nbytes((seq, HEAD_DIM), BF16),
                2 * _nbytes((seq + ATTN_LEAD, HEAD_DIM), BF16))),
        name="band_attn",
    )(q, k, v, bias_row.reshape(bias_row.shape[0], 1, ATTN_BIAS_LANES))


def _mix_up_kernel(oa_ref, ob_ref, wa_ref, wb_ref, ga_ref, gb_ref, o_ref):
    ya = _dot(oa_ref[...], wa_ref[...])
    yb = _dot(ob_ref[...], wb_ref[...])
    o_ref[...] = (ga_ref[...].astype(F32) * ya + gb_ref[...].astype(F32) * yb).astype(o_ref.dtype)


def mix_up(oa, ob, wa, wb, ga, gb, *, tm=1024, tn=512):
    t, kdim = oa.shape
    d = wa.shape[1]
    pipelined = (2 * _nbytes((tm, kdim), BF16) + 2 * _nbytes((kdim, tn), BF16) + 3 * _nbytes((tm, tn), BF16))
    return pl.pallas_call(
        _mix_up_kernel,
        grid=(t // tm, d // tn),
        in_specs=[pl.BlockSpec((tm, kdim), lambda i, j: (i, 0)),
                  pl.BlockSpec((tm, kdim), lambda i, j: (i, 0)),
                  pl.BlockSpec((kdim, tn), lambda i, j: (0, j)),
                  pl.BlockSpec((kdim, tn), lambda i, j: (0, j)),
                  pl.BlockSpec((tm, tn), lambda i, j: (i, j)),
                  pl.BlockSpec((tm, tn), lambda i, j: (i, j))],
        out_specs=pl.BlockSpec((tm, tn), lambda i, j: (i, j)),
        out_shape=jax.ShapeDtypeStruct((t, d), BF16),
        compiler_params=pltpu.CompilerParams(
            dimension_semantics=("parallel", "arbitrary"),
            vmem_limit_bytes=_vmem_limit(pipelined, 4 * _nbytes((tm, tn), F32))),
        name="mix_up",
    )(oa, ob, wa, wb, ga, gb)


def _pad_ff(w, axis):
    f = w.shape[axis]
    pad = (-f) % FF_PAD_MULTIPLE
    if pad == 0:
        return w
    widths = [(0, 0)] * w.ndim
    widths[axis] = (0, pad)
    return jnp.pad(w, widths)


def _ffn(x, h, w1, w3, w2, post_g, next_g, emit_next):
    w1b = _pad_ff(w1.astype(BF16), 1)
    w3b = _pad_ff(w3.astype(BF16), 1)
    w2b = _pad_ff(w2.astype(BF16), 0)
    g = glu_up(h, w1b, w3b)
    return down_resnorm(g, w2b, x, post_g, next_g, scale=MACARON_WEIGHT, emit_next=emit_next)


def kernel(x, ffn1_pre_g, ffn1_post_g, ffn1_w1, ffn1_w3, ffn1_w2, mix_pre_g, mix_post_g, w_in, b_gate,
           hgrn_lb_logits, hgrn_norm_g, rel_bias, w_up_a, w_up_b, w_out,
           ffn2_pre_g, ffn2_post_g, ffn2_w1, ffn2_w3, ffn2_w2):
    batch, seq, d = x.shape
    depth = ffn1_w1.shape[0]
    d_half = d // 2
    lower_bounds = jnp.cumsum(jax.nn.softmax(hgrn_lb_logits.astype(F32), axis=0), axis=0)
    xt = x.reshape(batch * seq, d)
    h = norm_cast(xt, ffn1_pre_g[0])
    for layer in range(depth):
        xt, h = _ffn(xt, h, ffn1_w1[layer], ffn1_w3[layer], ffn1_w2[layer],
                     ffn1_post_g[layer], mix_pre_g[layer], True)

        w_in_b = w_in[layer].astype(BF16)
        qs, lf, kk, vv, gs = proj(h, w_in_b, [0, d_half, 2 * d_half, 3 * d_half], d_half,
                                  [lower_bounds[layer]], _hgrn_epilogue, [BF16, F32, BF16, BF16, BF16])
        qb, kb, vb = proj(h, w_in_b, [4 * d_half, 5 * d_half, 6 * d_half], d_half,
                          [], _attn_epilogue, [BF16, BF16, BF16])
        ga, gb = proj(h, w_in_b, [7 * d_half, 7 * d_half + d], d,
                      [b_gate[layer, 0], b_gate[layer, 1]], _gate_epilogue, [BF16, BF16])

        oa = hgrn(qs, lf, kk, vv, gs, hgrn_norm_g[layer], batch=batch, seq=seq)
        ob = band_attn(qb, kb, vb, _attn_bias_row(rel_bias[layer].astype(F32)), batch=batch, seq=seq)

        m = mix_up(oa, ob, w_up_a[layer].astype(BF16), w_up_b[layer].astype(BF16), ga, gb)
        last = layer == depth - 1
        xt, h = down_resnorm(m, w_out[layer].astype(BF16), xt, mix_post_g[layer], ffn2_pre_g[layer],
                             scale=1.0, emit_next=True)
        next_pre = ffn1_pre_g[layer + 1] if not last else ffn2_pre_g[layer]
        xt, h = _ffn(xt, h, ffn2_w1[layer], ffn2_w3[layer], ffn2_w2[layer],
                     ffn2_post_g[layer], next_pre, not last)
    return xt.reshape(batch, seq, d)
```

```python
import functools

import numpy as np
import jax
import jax.numpy as jnp
from jax import lax
from jax.experimental import pallas as pl
from jax.experimental.pallas import tpu as pltpu

F32 = jnp.float32
BF16 = jnp.bfloat16

NORM_EPS = 1e-6
LOG2_E = 1.4426950408889634
MACARON_WEIGHT = 0.5
CHUNK = 64
HEAD_DIM = 128
LEFT_CHUNKS = 8
MAX_REL = 256
MASK_VALUE = -1e30

V7X_VMEM_BYTES = 64 * 1024 * 1024
VMEM_LIMIT_CAP = V7X_VMEM_BYTES - 6 * 1024 * 1024


def _vmem_limit(pipelined_bytes, resident_bytes=0):
    est = 2 * pipelined_bytes + resident_bytes + 8 * 1024 * 1024
    return int(min(max(est, 32 * 1024 * 1024), VMEM_LIMIT_CAP))


def _nbytes(shape, dtype):
    return int(np.prod(shape)) * jnp.dtype(dtype).itemsize


def _dot(a, b):
    return jnp.dot(a, b, preferred_element_type=F32)


def _dot_nt(a, b):
    return lax.dot_general(a, b, (((1,), (1,)), ((), ())), preferred_element_type=F32)


def _dot_tn(a, b):
    return lax.dot_general(a, b, (((0,), (0,)), ((), ())), preferred_element_type=F32)


def _silu(x):
    return x * jax.nn.sigmoid(x)


def _norm_cast_kernel(x_ref, g_ref, o_ref):
    x = x_ref[...]
    inv = lax.rsqrt(jnp.mean(x * x, axis=-1, keepdims=True) + NORM_EPS)
    o_ref[...] = (x * inv * g_ref[...]).astype(o_ref.dtype)


def norm_cast(x, g, *, tm=256):
    t, d = x.shape
    return pl.pallas_call(
        _norm_cast_kernel,
        grid=(t // tm,),
        in_specs=[pl.BlockSpec((tm, d), lambda i: (i, 0)),
                  pl.BlockSpec((1, d), lambda i: (0, 0))],
        out_specs=pl.BlockSpec((tm, d), lambda i: (i, 0)),
        out_shape=jax.ShapeDtypeStruct((t, d), BF16),
        compiler_params=pltpu.CompilerParams(
            dimension_semantics=("parallel",),
            vmem_limit_bytes=_vmem_limit(_nbytes((tm, d), F32) + _nbytes((tm, d), BF16),
                                         3 * _nbytes((tm, d), F32))),
        name="norm_cast",
    )(x, g.reshape(1, d))


def _glu_up_kernel(h_ref, w1_ref, w3_ref, o_ref):
    h = h_ref[...]
    a = _dot(h, w1_ref[...].astype(BF16))
    b = _dot(h, w3_ref[...].astype(BF16))
    o_ref[...] = (_silu(a) * b).astype(o_ref.dtype)


def glu_up(h, w1, w3, *, tm=1024, tn=256):
    t, d = h.shape
    f = w1.shape[1]
    return pl.pallas_call(
        _glu_up_kernel,
        grid=(t // tm, f // tn),
        in_specs=[pl.BlockSpec((tm, d), lambda i, j: (i, 0)),
                  pl.BlockSpec((d, tn), lambda i, j: (0, j)),
                  pl.BlockSpec((d, tn), lambda i, j: (0, j))],
        out_specs=pl.BlockSpec((tm, tn), lambda i, j: (i, j)),
        out_shape=jax.ShapeDtypeStruct((t, f), BF16),
        compiler_params=pltpu.CompilerParams(
            dimension_semantics=("parallel", "arbitrary"),
            vmem_limit_bytes=_vmem_limit(
                _nbytes((tm, d), BF16) + 2 * _nbytes((d, tn), w1.dtype) + _nbytes((tm, tn), BF16),
                4 * _nbytes((tm, tn), F32) + 2 * _nbytes((d, tn), BF16))),
        name="glu_up",
    )(h, w1, w3)


def _matmul_kernel(a_ref, w_ref, o_ref):
    o_ref[...] = _dot(a_ref[...], w_ref[...]).astype(o_ref.dtype)


def matmul_kres(a, w, *, tm=512, tn=256):
    t, kdim = a.shape
    n = w.shape[1]
    return pl.pallas_call(
        _matmul_kernel,
        grid=(t // tm, n // tn),
        in_specs=[pl.BlockSpec((tm, kdim), lambda i, j: (i, 0)),
                  pl.BlockSpec((kdim, tn), lambda i, j: (0, j))],
        out_specs=pl.BlockSpec((tm, tn), lambda i, j: (i, j)),
        out_shape=jax.ShapeDtypeStruct((t, n), BF16),
        compiler_params=pltpu.CompilerParams(
            dimension_semantics=("parallel", "arbitrary"),
            vmem_limit_bytes=_vmem_limit(
                _nbytes((tm, kdim), BF16) + _nbytes((kdim, tn), BF16) + _nbytes((tm, tn), BF16),
                2 * _nbytes((tm, tn), F32))),
        name="matmul_kres",
    )(a, w)


def _resnorm_kernel(y_ref, x_ref, gpost_ref, gnext_ref, xo_ref, *maybe_h_ref, scale):
    y = y_ref[...].astype(F32)
    inv = lax.rsqrt(jnp.mean(y * y, axis=-1, keepdims=True) + NORM_EPS)
    xn = x_ref[...] + scale * (y * inv * gpost_ref[...])
    xo_ref[...] = xn
    if maybe_h_ref:
        inv2 = lax.rsqrt(jnp.mean(xn * xn, axis=-1, keepdims=True) + NORM_EPS)
        maybe_h_ref[0][...] = (xn * inv2 * gnext_ref[...]).astype(BF16)


def resnorm(y, x, g_post, g_next, *, scale, emit_next, tm=256):
    t, d = x.shape
    row = pl.BlockSpec((tm, d), lambda i: (i, 0))
    vec = pl.BlockSpec((1, d), lambda i: (0, 0))
    out_shape = [jax.ShapeDtypeStruct((t, d), F32)]
    if emit_next:
        out_shape.append(jax.ShapeDtypeStruct((t, d), BF16))
    pipelined = (_nbytes((tm, d), y.dtype) + 2 * _nbytes((tm, d), F32)
                 + (_nbytes((tm, d), BF16) if emit_next else 0))
    outs = pl.pallas_call(
        functools.partial(_resnorm_kernel, scale=scale),
        grid=(t // tm,),
        in_specs=[row, row, vec, vec],
        out_specs=[row] * len(out_shape),
        out_shape=out_shape,
        compiler_params=pltpu.CompilerParams(
            dimension_semantics=("parallel",),
            vmem_limit_bytes=_vmem_limit(pipelined, 3 * _nbytes((tm, d), F32))),
        name="resnorm",
    )(y, x, g_post.reshape(1, d), g_next.reshape(1, d))
    return outs if emit_next else (outs[0], None)


def _down_resnorm_kernel(a_ref, w_ref, x_ref, gpost_ref, gnext_ref, xo_ref, *maybe_h_ref,
                         scale, n_chunk, row_chunk):
    k = pl.program_id(1)
    d = xo_ref.shape[1]
    a = a_ref[...]

    @pl.when(k == 0)
    def _():
        for n in range(0, d, n_chunk):
            xo_ref[:, n:n + n_chunk] = _dot(a, w_ref[:, n:n + n_chunk])

    @pl.when(k > 0)
    def _():
        for n in range(0, d, n_chunk):
            xo_ref[:, n:n + n_chunk] += _dot(a, w_ref[:, n:n + n_chunk])

    @pl.when(k == pl.num_programs(1) - 1)
    def _():
        gpost = gpost_ref[...]
        gnext = gnext_ref[...]

        def body(r, carry):
            rows = pl.ds(pl.multiple_of(r * row_chunk, row_chunk), row_chunk)
            y = xo_ref[rows, :]
            inv = lax.rsqrt(jnp.mean(y * y, axis=-1, keepdims=True) + NORM_EPS)
            xn = x_ref[rows, :] + scale * (y * inv * gpost)
            xo_ref[rows, :] = xn
            if maybe_h_ref:
                inv2 = lax.rsqrt(jnp.mean(xn * xn, axis=-1, keepdims=True) + NORM_EPS)
                maybe_h_ref[0][rows, :] = (xn * inv2 * gnext).astype(BF16)
            return carry

        lax.fori_loop(0, xo_ref.shape[0] // row_chunk, body, 0)


def down_resnorm(a, w, x, g_post, g_next, *, scale, emit_next, tm=512, tk=512):
    t, kdim = a.shape
    d = w.shape[1]
    out_shape = [jax.ShapeDtypeStruct((t, d), F32)]
    out_specs = [pl.BlockSpec((tm, d), lambda i, k: (i, 0))]
    if emit_next:
        out_shape.append(jax.ShapeDtypeStruct((t, d), BF16))
        out_specs.append(pl.BlockSpec((tm, d), lambda i, k: (i, 0)))
    pipelined = (_nbytes((tm, tk), BF16) + _nbytes((tk, d), BF16) + 2 * _nbytes((tm, d), F32)
                 + (_nbytes((tm, d), BF16) if emit_next else 0))
    outs = pl.pallas_call(
        functools.partial(_down_resnorm_kernel, scale=scale, n_chunk=512, row_chunk=64),
        grid=(t // tm, kdim // tk),
        in_specs=[pl.BlockSpec((tm, tk), lambda i, k: (i, k)),
                  pl.BlockSpec((tk, d), lambda i, k: (k, 0)),
                  pl.BlockSpec((tm, d), lambda i, k: (i, 0)),
                  pl.BlockSpec((1, d), lambda i, k: (0, 0)),
                  pl.BlockSpec((1, d), lambda i, k: (0, 0))],
        out_specs=out_specs,
        out_shape=out_shape,
        compiler_params=pltpu.CompilerParams(
            dimension_semantics=("parallel", "arbitrary"),
            vmem_limit_bytes=_vmem_limit(pipelined, 0)),
        name="down_resnorm",
    )(a, w, x, g_post.reshape(1, d), g_next.reshape(1, d))
    return outs if emit_next else (outs[0], None)


def _proj_kernel(*refs, n_seg, n_aux, epilogue):
    h_ref = refs[0]
    w_refs = refs[1:1 + n_seg]
    aux_refs = refs[1 + n_seg:1 + n_seg + n_aux]
    out_refs = refs[1 + n_seg + n_aux:]
    h = h_ref[...]
    ys = [_dot(h, w_ref[...]) for w_ref in w_refs]
    outs = epilogue(ys, [r[...] for r in aux_refs])
    for o_ref, o in zip(out_refs, outs):
        o_ref[...] = o.astype(o_ref.dtype)


def proj(h, w_in, col_starts, width, aux, epilogue, out_dtypes, *, tm=1024, tn=256):
    t, d = h.shape
    n_seg = len(col_starts)

    def w_spec(start):
        off = start // tn
        return pl.BlockSpec((d, tn), lambda i, j: (0, off + j))

    pipelined = (_nbytes((tm, d), BF16) + n_seg * _nbytes((d, tn), BF16)
                 + sum(_nbytes((tm, tn), dt) for dt in out_dtypes))
    return pl.pallas_call(
        functools.partial(_proj_kernel, n_seg=n_seg, n_aux=len(aux), epilogue=epilogue),
        grid=(t // tm, width // tn),
        in_specs=([pl.BlockSpec((tm, d), lambda i, j: (i, 0))]
                  + [w_spec(s) for s in col_starts]
                  + [pl.BlockSpec((1, tn), lambda i, j: (0, j)) for _ in aux]),
        out_specs=[pl.BlockSpec((tm, tn), lambda i, j: (i, j)) for _ in out_dtypes],
        out_shape=[jax.ShapeDtypeStruct((t, width), dt) for dt in out_dtypes],
        compiler_params=pltpu.CompilerParams(
            dimension_semantics=("parallel", "arbitrary"),
            vmem_limit_bytes=_vmem_limit(pipelined, (n_seg + 4) * _nbytes((tm, tn), F32))),
        name="proj",
    )(h, *([w_in] * n_seg), *[a.reshape(1, width) for a in aux])


def _hgrn_epilogue(ys, aux):
    q, fr, i, g = ys
    lb, = aux
    f = lb + (1.0 - lb) * jax.nn.sigmoid(fr)
    return _silu(q), jnp.log(f) * LOG2_E, 1.0 - f, i, _silu(g)


def _attn_epilogue(ys, aux):
    q, k, v = ys
    return q * (HEAD_DIM ** -0.5), k, v


def _gate_epilogue(ys, aux):
    ga, gb = ys
    ba, bb = aux
    return jax.nn.sigmoid(ga + ba), jax.nn.sigmoid(gb + bb)


def _lower_half_total(p_hb, level, row8):
    c, w = p_hb.shape
    hb = 1 << level
    if level >= 2:
        blk = min(2 * hb, c)
        return jnp.concatenate(
            [jnp.broadcast_to(p_hb[b * blk + hb - 1:b * blk + hb, :], (blk, w)) for b in range(c // blk)], axis=0)
    x = p_hb.reshape(c // 8, 8, w)
    odd = (row8 & 1) == 1
    if level == 0:
        g = jnp.where(odd, pltpu.roll(x, 1, 1), x)
    else:
        z = jnp.where(odd, x, pltpu.roll(x, 7, 1))
        g = jnp.where((row8 & 2) == 0, z, pltpu.roll(z, 2, 1))
    return g.reshape(c, w)


def _hgrn_kernel(qs_ref, lf_ref, kk_ref, v_ref, gs_ref, ng_ref, o_ref, st_ref, *, heads_per_block,
                 chunks_per_iter):
    c = CHUNK
    dk = HEAD_DIM
    w = qs_ref.shape[1]
    n_levels = 6

    @pl.when(pl.program_id(2) == 0)
    def _():
        st_ref[...] = jnp.zeros_like(st_ref)

    row = lax.broadcasted_iota(jnp.int32, (c, w), 0)
    row8 = lax.broadcasted_iota(jnp.int32, (c // 8, 8, w), 1)
    upper = [((row >> p) & 1) == 1 for p in range(n_levels)]
    ti = lax.broadcasted_iota(jnp.int32, (c, c), 0)
    si = lax.broadcasted_iota(jnp.int32, (c, c), 1)
    diag_mask = ti == si
    level_masks = [((ti >> (p + 1)) == (si >> (p + 1))) & (((ti >> p) & 1) == 1) & (((si >> p) & 1) == 0)
                   for p in range(n_levels)]

    def iter_body(it, carry):
        pre = []
        for cc in range(chunks_per_iter):
            rows = pl.ds(pl.multiple_of((it * chunks_per_iter + cc) * c, c), c)
            p_hb = lf_ref[rows, :]
            e_q, e_k = [], []
            for p in range(n_levels + 1):
                g = _lower_half_total(p_hb, p, row8)
                e_q.append(jnp.exp2(p_hb))
                e_k.append(None if p == 0 else jnp.exp2(g - p_hb))
                if p < n_levels:
                    p_hb = p_hb + jnp.where(upper[p], g, 0.0)
            pre.append((rows, e_q, e_k))
        indep = {}
        for cc, (rows, e_q, e_k) in enumerate(pre):
            for hh in range(heads_per_block):
                cols = slice(hh * dk, (hh + 1) * dk)
                qb = qs_ref[rows, cols]
                kb = kk_ref[rows, cols]
                v = v_ref[rows, cols]
                q = qb.astype(F32)
                k = kb.astype(F32)
                parts = [_dot_nt(qb, kb)]
                for p in range(n_levels):
                    qt = (q * e_q[p][:, cols]).astype(BF16)
                    kt = kb if p == 0 else (k * e_k[p][:, cols]).astype(BF16)
                    parts.append(_dot_nt(qt, kt))
                e_in = e_q[n_levels][:, cols]
                e_out = e_k[n_levels][:, cols]
                kv = _dot_tn(v, (k * e_out).astype(BF16))
                indep[cc, hh] = (parts, kv, (q * e_in).astype(BF16), v, e_in[c - 1:c, :])
        for hh in range(heads_per_block):
            cols = slice(hh * dk, (hh + 1) * dk)
            st = st_ref[hh]
            for cc, (rows, _, _) in enumerate(pre):
                parts, kv, q_in, v, decay_all = indep[cc, hh]
                scores = jnp.where(diag_mask, parts[0], 0.0)
                for p in range(n_levels):
                    scores = jnp.where(level_masks[p], parts[p + 1], scores)
                o = _dot(scores.astype(BF16), v) + _dot_nt(q_in, st.astype(BF16))
                st = st * decay_all + kv
                inv = lax.rsqrt(jnp.mean(o * o, axis=-1, keepdims=True) + NORM_EPS)
                o = o * inv * ng_ref[:, cols] * gs_ref[rows, cols].astype(F32)
                o_ref[rows, cols] = o.astype(o_ref.dtype)
            st_ref[hh] = st
        return carry

    lax.fori_loop(0, qs_ref.shape[0] // (c * chunks_per_iter), iter_body, 0)


def hgrn(qs, lf, kk, v, gs, norm_g, *, batch, seq, block_len=512, heads_per_block=4, chunks_per_iter=2):
    t, width = qs.shape
    bw = heads_per_block * HEAD_DIM
    n_l = seq // block_len
    tok = pl.BlockSpec((block_len, bw), lambda b, h, l: (b * n_l + l, h))
    pipelined = 4 * _nbytes((block_len, bw), BF16) + _nbytes((block_len, bw), F32)
    return pl.pallas_call(
        functools.partial(_hgrn_kernel, heads_per_block=heads_per_block, chunks_per_iter=chunks_per_iter),
        grid=(batch, width // bw, n_l),
        in_specs=[tok, tok, tok, tok, tok,
                  pl.BlockSpec((1, bw), lambda b, h, l: (0, h))],
        out_specs=tok,
        out_shape=jax.ShapeDtypeStruct((t, width), BF16),
        scratch_shapes=[pltpu.VMEM((heads_per_block, HEAD_DIM, HEAD_DIM), F32)],
        compiler_params=pltpu.CompilerParams(
            dimension_semantics=("parallel", "parallel", "arbitrary"),
            vmem_limit_bytes=_vmem_limit(pipelined, 0)),
        name="hgrn",
    )(qs, lf, kk, v, gs, norm_g.reshape(1, width))


ATTN_Q_BLOCK = CHUNK
ATTN_WINDOW = 640
ATTN_LEAD = ATTN_WINDOW - ATTN_Q_BLOCK
ATTN_BIAS_LANES = 768


def _attn_bias_row(rel_bias):
    m = np.arange(ATTN_BIAS_LANES)
    m = np.where(m >= ATTN_WINDOW, m - ATTN_BIAS_LANES, m)
    idx = np.clip(ATTN_LEAD - m, -MAX_REL, MAX_REL) + MAX_REL
    runs, start = [], 0
    for pos in range(1, len(idx) + 1):
        if pos == len(idx) or idx[pos] != idx[pos - 1] - 1:
            runs.append((start, pos))
            start = pos
    pieces = [jnp.flip(rel_bias[:, int(idx[hi - 1]):int(idx[lo]) + 1], axis=1) for lo, hi in runs]
    return jnp.concatenate(pieces, axis=1)


def _attn_kernel(q_ref, k_ref, v_ref, brow_ref, o_ref, kpad_ref, vpad_ref, *, group):
    seq = q_ref.shape[0]
    kpad_ref[0:ATTN_LEAD, :] = jnp.zeros((ATTN_LEAD, HEAD_DIM), kpad_ref.dtype)
    vpad_ref[0:ATTN_LEAD, :] = jnp.zeros((ATTN_LEAD, HEAD_DIM), vpad_ref.dtype)
    kpad_ref[ATTN_LEAD:, :] = k_ref[...]
    vpad_ref[ATTN_LEAD:, :] = v_ref[...]
    brow = jnp.broadcast_to(brow_ref[0], (ATTN_Q_BLOCK, ATTN_BIAS_LANES))
    toep = pltpu.roll(brow, 0, 1, stride=1, stride_axis=0)[:, :ATTN_WINDOW]
    qi = lax.broadcasted_iota(jnp.int32, (ATTN_Q_BLOCK, ATTN_WINDOW), 0)
    col = lax.broadcasted_iota(jnp.int32, (ATTN_Q_BLOCK, ATTN_WINDOW), 1)
    qc = qi // CHUNK
    kc = col // CHUNK
    lead = ATTN_LEAD // CHUNK
    bias = jnp.where((kc >= qc + lead - LEFT_CHUNKS) & (kc <= qc + lead), toep, MASK_VALUE)

    def do_group(base, masked):
        starts = [base + u * ATTN_Q_BLOCK for u in range(group)]
        scores = []
        for start in starts:
            q = q_ref[pl.ds(start, ATTN_Q_BLOCK), :]
            kw = kpad_ref[pl.ds(start, ATTN_WINDOW), :]
            s = _dot_nt(q, kw) + bias
            if masked:
                s = jnp.where(col >= ATTN_LEAD - start, s, MASK_VALUE)
            scores.append(s)
        probs = []
        for s in scores:
            m = jnp.max(s, axis=-1, keepdims=True)
            p = jnp.exp(s - m)
            probs.append((p.astype(BF16), jnp.sum(p, axis=-1, keepdims=True)))
        for start, (p, denom) in zip(starts, probs):
            vw = vpad_ref[pl.ds(start, ATTN_WINDOW), :]
            o = _dot(p, vw) / denom
            o_ref[pl.ds(start, ATTN_Q_BLOCK), :] = o.astype(o_ref.dtype)

    span = group * ATTN_Q_BLOCK
    n_masked = pl.cdiv(ATTN_LEAD, span)
    for g in range(n_masked):
        do_group(g * span, True)

    def body(g, carry):
        do_group(pl.multiple_of(g * span, span), False)
        return carry

    lax.fori_loop(n_masked, seq // span, body, 0)


def band_attn(q, k, v, bias_row, *, batch, seq, group=8):
    t, width = q.shape
    tok = pl.BlockSpec((seq, HEAD_DIM), lambda b, h: (b, h))
    return pl.pallas_call(
        functools.partial(_attn_kernel, group=group),
        grid=(batch, width // HEAD_DIM),
        in_specs=[tok, tok, tok,
                  pl.BlockSpec((1, 1, ATTN_BIAS_LANES), lambda b, h: (h, 0, 0))],
        out_specs=tok,
        out_shape=jax.ShapeDtypeStruct((t, width), BF16),
        scratch_shapes=[pltpu.VMEM((seq + ATTN_LEAD, HEAD_DIM), BF16),
                        pltpu.VMEM((seq + ATTN_LEAD, HEAD_DIM), BF16)],
        compiler_params=pltpu.CompilerParams(
            dimension_semantics=("parallel", "parallel"),
            vmem_limit_bytes=_vmem_limit(
                4 * _nbytes((seq, HEAD_DIM), BF16),
                2 * _nbytes((seq + ATTN_LEAD, HEAD_DIM), BF16))),
        name="band_attn",
    )(q, k, v, bias_row.reshape(bias_row.shape[0], 1, ATTN_BIAS_LANES))


def _mix_up_kernel(oa_ref, ob_ref, wa_ref, wb_ref, ga_ref, gb_ref, o_ref):
    ya = _dot(oa_ref[...], wa_ref[...])
    yb = _dot(ob_ref[...], wb_ref[...])
    o_ref[...] = (ga_ref[...].astype(F32) * ya + gb_ref[...].astype(F32) * yb).astype(o_ref.dtype)


def mix_up(oa, ob, wa, wb, ga, gb, *, tm=1024, tn=512):
    t, kdim = oa.shape
    d = wa.shape[1]
    pipelined = (2 * _nbytes((tm, kdim), BF16) + 2 * _nbytes((kdim, tn), BF16) + 3 * _nbytes((tm, tn), BF16))
    return pl.pallas_call(
        _mix_up_kernel,
        grid=(t // tm, d // tn),
        in_specs=[pl.BlockSpec((tm, kdim), lambda i, j: (i, 0)),
                  pl.BlockSpec((tm, kdim), lambda i, j: (i, 0)),
                  pl.BlockSpec((kdim, tn), lambda i, j: (0, j)),
                  pl.BlockSpec((kdim, tn), lambda i, j: (0, j)),
                  pl.BlockSpec((tm, tn), lambda i, j: (i, j)),
                  pl.BlockSpec((tm, tn), lambda i, j: (i, j))],
        out_specs=pl.BlockSpec((tm, tn), lambda i, j: (i, j)),
        out_shape=jax.ShapeDtypeStruct((t, d), BF16),
        compiler_params=pltpu.CompilerParams(
            dimension_semantics=("parallel", "arbitrary"),
            vmem_limit_bytes=_vmem_limit(pipelined, 4 * _nbytes((tm, tn), F32))),
        name="mix_up",
    )(oa, ob, wa, wb, ga, gb)


def _ffn(x, h, w1, w3, w2, post_g, next_g, emit_next):
    g = glu_up(h, w1, w3)
    y = matmul_kres(g, w2.astype(BF16))
    return resnorm(y, x, post_g, next_g, scale=MACARON_WEIGHT, emit_next=emit_next)


def kernel(x, ffn1_pre_g, ffn1_post_g, ffn1_w1, ffn1_w3, ffn1_w2, mix_pre_g, mix_post_g, w_in, b_gate,
           hgrn_lb_logits, hgrn_norm_g, rel_bias, w_up_a, w_up_b, w_out,
           ffn2_pre_g, ffn2_post_g, ffn2_w1, ffn2_w3, ffn2_w2):
    batch, seq, d = x.shape
    depth = ffn1_w1.shape[0]
    d_half = d // 2
    lower_bounds = jnp.cumsum(jax.nn.softmax(hgrn_lb_logits.astype(F32), axis=0), axis=0)
    xt = x.reshape(batch * seq, d)
    h = norm_cast(xt, ffn1_pre_g[0])
    for layer in range(depth):
        xt, h = _ffn(xt, h, ffn1_w1[layer], ffn1_w3[layer], ffn1_w2[layer],
                     ffn1_post_g[layer], mix_pre_g[layer], True)

        w_in_b = w_in[layer].astype(BF16)
        qs, lf, kk, vv, gs = proj(h, w_in_b, [0, d_half, 2 * d_half, 3 * d_half], d_half,
                                  [lower_bounds[layer]], _hgrn_epilogue, [BF16, F32, BF16, BF16, BF16])
        qb, kb, vb = proj(h, w_in_b, [4 * d_half, 5 * d_half, 6 * d_half], d_half,
                          [], _attn_epilogue, [BF16, BF16, BF16])
        ga, gb = proj(h, w_in_b, [7 * d_half, 7 * d_half + d], d,
                      [b_gate[layer, 0], b_gate[layer, 1]], _gate_epilogue, [BF16, BF16])

        oa = hgrn(qs, lf, kk, vv, gs, hgrn_norm_g[layer], batch=batch, seq=seq)
        ob = band_attn(qb, kb, vb, _attn_bias_row(rel_bias[layer].astype(F32)), batch=batch, seq=seq)

        m = mix_up(oa, ob, w_up_a[layer].astype(BF16), w_up_b[layer].astype(BF16), ga, gb)
        last = layer == depth - 1
        xt, h = down_resnorm(m, w_out[layer].astype(BF16), xt, mix_post_g[layer], ffn2_pre_g[layer],
                             scale=1.0, emit_next=True)
        next_pre = ffn1_pre_g[layer + 1] if not last else ffn2_pre_g[layer]
        xt, h = _ffn(xt, h, ffn2_w1[layer], ffn2_w3[layer], ffn2_w2[layer],
                     ffn2_post_g[layer], next_pre, not last)
    return xt.reshape(batch, seq, d)
```

```python
import functools

import numpy as np
import jax
import jax.numpy as jnp
from jax import lax
from jax.experimental import pallas as pl
from jax.experimental.pallas import tpu as pltpu

F32 = jnp.float32
BF16 = jnp.bfloat16

NORM_EPS = 1e-6
LOG2_E = 1.4426950408889634
MACARON_WEIGHT = 0.5
CHUNK = 64
HEAD_DIM = 128
LEFT_CHUNKS = 8
MAX_REL = 256
MASK_VALUE = -1e30

V7X_VMEM_BYTES = 64 * 1024 * 1024
VMEM_LIMIT_CAP = V7X_VMEM_BYTES - 6 * 1024 * 1024


def _vmem_limit(pipelined_bytes, resident_bytes=0):
    est = 2 * pipelined_bytes + resident_bytes + 8 * 1024 * 1024
    return int(min(max(est, 32 * 1024 * 1024), VMEM_LIMIT_CAP))


def _nbytes(shape, dtype):
    return int(np.prod(shape)) * jnp.dtype(dtype).itemsize


def _dot(a, b):
    return jnp.dot(a, b, preferred_element_type=F32)


def _dot_nt(a, b):
    return lax.dot_general(a, b, (((1,), (1,)), ((), ())), preferred_element_type=F32)


def _dot_tn(a, b):
    return lax.dot_general(a, b, (((0,), (0,)), ((), ())), preferred_element_type=F32)


def _silu(x):
    return x * jax.nn.sigmoid(x)


def _norm_cast_kernel(x_ref, g_ref, o_ref):
    x = x_ref[...]
    inv = lax.rsqrt(jnp.mean(x * x, axis=-1, keepdims=True) + NORM_EPS)
    o_ref[...] = (x * inv * g_ref[...]).astype(o_ref.dtype)


def norm_cast(x, g, *, tm=256):
    t, d = x.shape
    return pl.pallas_call(
        _norm_cast_kernel,
        grid=(t // tm,),
        in_specs=[pl.BlockSpec((tm, d), lambda i: (i, 0)),
                  pl.BlockSpec((1, d), lambda i: (0, 0))],
        out_specs=pl.BlockSpec((tm, d), lambda i: (i, 0)),
        out_shape=jax.ShapeDtypeStruct((t, d), BF16),
        compiler_params=pltpu.CompilerParams(
            dimension_semantics=("parallel",),
            vmem_limit_bytes=_vmem_limit(_nbytes((tm, d), F32) + _nbytes((tm, d), BF16),
                                         3 * _nbytes((tm, d), F32))),
        name="norm_cast",
    )(x, g.reshape(1, d))


def _glu_up_kernel(h_ref, w1_ref, w3_ref, o_ref):
    h = h_ref[...]
    a = _dot(h, w1_ref[...].astype(BF16))
    b = _dot(h, w3_ref[...].astype(BF16))
    o_ref[...] = (_silu(a) * b).astype(o_ref.dtype)


def glu_up(h, w1, w3, *, tm=2048, tn=256):
    t, d = h.shape
    f = w1.shape[1]
    return pl.pallas_call(
        _glu_up_kernel,
        grid=(t // tm, f // tn),
        in_specs=[pl.BlockSpec((tm, d), lambda i, j: (i, 0), pipeline_mode=pl.Buffered(1)),
                  pl.BlockSpec((d, tn), lambda i, j: (0, j)),
                  pl.BlockSpec((d, tn), lambda i, j: (0, j))],
        out_specs=pl.BlockSpec((tm, tn), lambda i, j: (i, j)),
        out_shape=jax.ShapeDtypeStruct((t, f), BF16),
        compiler_params=pltpu.CompilerParams(
            dimension_semantics=("parallel", "arbitrary"),
            vmem_limit_bytes=_vmem_limit(
                2 * _nbytes((d, tn), w1.dtype) + _nbytes((tm, tn), BF16),
                _nbytes((tm, d), BF16) + 4 * _nbytes((tm, tn), F32) + 2 * _nbytes((d, tn), BF16))),
        name="glu_up",
    )(h, w1, w3)


def _matmul_kernel(a_ref, w_ref, o_ref):
    o_ref[...] = _dot(a_ref[...], w_ref[...]).astype(o_ref.dtype)


def matmul_kres(a, w, *, tm=512, tn=512):
    t, kdim = a.shape
    n = w.shape[1]
    return pl.pallas_call(
        _matmul_kernel,
        grid=(t // tm, n // tn),
        in_specs=[pl.BlockSpec((tm, kdim), lambda i, j: (i, 0)),
                  pl.BlockSpec((kdim, tn), lambda i, j: (0, j))],
        out_specs=pl.BlockSpec((tm, tn), lambda i, j: (i, j)),
        out_shape=jax.ShapeDtypeStruct((t, n), BF16),
        compiler_params=pltpu.CompilerParams(
            dimension_semantics=("parallel", "arbitrary"),
            vmem_limit_bytes=_vmem_limit(
                _nbytes((tm, kdim), BF16) + _nbytes((kdim, tn), BF16) + _nbytes((tm, tn), BF16),
                2 * _nbytes((tm, tn), F32))),
        name="matmul_kres",
    )(a, w)


def _resnorm_kernel(y_ref, x_ref, gpost_ref, gnext_ref, xo_ref, *maybe_h_ref, scale):
    y = y_ref[...].astype(F32)
    inv = lax.rsqrt(jnp.mean(y * y, axis=-1, keepdims=True) + NORM_EPS)
    xn = x_ref[...] + scale * (y * inv * gpost_ref[...])
    xo_ref[...] = xn
    if maybe_h_ref:
        inv2 = lax.rsqrt(jnp.mean(xn * xn, axis=-1, keepdims=True) + NORM_EPS)
        maybe_h_ref[0][...] = (xn * inv2 * gnext_ref[...]).astype(BF16)


def resnorm(y, x, g_post, g_next, *, scale, emit_next, tm=256):
    t, d = x.shape
    row = pl.BlockSpec((tm, d), lambda i: (i, 0))
    vec = pl.BlockSpec((1, d), lambda i: (0, 0))
    out_shape = [jax.ShapeDtypeStruct((t, d), F32)]
    if emit_next:
        out_shape.append(jax.ShapeDtypeStruct((t, d), BF16))
    pipelined = (_nbytes((tm, d), y.dtype) + 2 * _nbytes((tm, d), F32)
                 + (_nbytes((tm, d), BF16) if emit_next else 0))
    outs = pl.pallas_call(
        functools.partial(_resnorm_kernel, scale=scale),
        grid=(t // tm,),
        in_specs=[row, row, vec, vec],
        out_specs=[row] * len(out_shape),
        out_shape=out_shape,
        compiler_params=pltpu.CompilerParams(
            dimension_semantics=("parallel",),
            vmem_limit_bytes=_vmem_limit(pipelined, 3 * _nbytes((tm, d), F32))),
        name="resnorm",
    )(y, x, g_post.reshape(1, d), g_next.reshape(1, d))
    return outs if emit_next else (outs[0], None)


def _proj_kernel(*refs, n_seg, n_aux, epilogue):
    h_ref = refs[0]
    w_refs = refs[1:1 + n_seg]
    aux_refs = refs[1 + n_seg:1 + n_seg + n_aux]
    out_refs = refs[1 + n_seg + n_aux:]
    h = h_ref[...]
    ys = [_dot(h, w_ref[...]) for w_ref in w_refs]
    outs = epilogue(ys, [r[...] for r in aux_refs])
    for o_ref, o in zip(out_refs, outs):
        o_ref[...] = o.astype(o_ref.dtype)


def proj(h, w_in, col_starts, width, aux, epilogue, out_dtypes, *, tm=1024, tn=256):
    t, d = h.shape
    n_seg = len(col_starts)

    def w_spec(start):
        off = start // tn
        return pl.BlockSpec((d, tn), lambda i, j: (0, off + j))

    pipelined = (_nbytes((tm, d), BF16) + n_seg * _nbytes((d, tn), BF16)
                 + sum(_nbytes((tm, tn), dt) for dt in out_dtypes))
    return pl.pallas_call(
        functools.partial(_proj_kernel, n_seg=n_seg, n_aux=len(aux), epilogue=epilogue),
        grid=(t // tm, width // tn),
        in_specs=([pl.BlockSpec((tm, d), lambda i, j: (i, 0))]
                  + [w_spec(s) for s in col_starts]
                  + [pl.BlockSpec((1, tn), lambda i, j: (0, j)) for _ in aux]),
        out_specs=[pl.BlockSpec((tm, tn), lambda i, j: (i, j)) for _ in out_dtypes],
        out_shape=[jax.ShapeDtypeStruct((t, width), dt) for dt in out_dtypes],
        compiler_params=pltpu.CompilerParams(
            dimension_semantics=("parallel", "arbitrary"),
            vmem_limit_bytes=_vmem_limit(pipelined, (n_seg + 4) * _nbytes((tm, tn), F32))),
        name="proj",
    )(h, *([w_in] * n_seg), *[a.reshape(1, width) for a in aux])


def _hgrn_epilogue(ys, aux):
    q, fr, i, g = ys
    lb, = aux
    f = lb + (1.0 - lb) * jax.nn.sigmoid(fr)
    return _silu(q), jnp.log(f) * LOG2_E, 1.0 - f, i, _silu(g)


def _attn_epilogue(ys, aux):
    q, k, v = ys
    return q * (HEAD_DIM ** -0.5), k, v


def _gate_epilogue(ys, aux):
    ga, gb = ys
    ba, bb = aux
    return jax.nn.sigmoid(ga + ba), jax.nn.sigmoid(gb + bb)


def _lower_half_total(p_hb, level, row8):
    c, w = p_hb.shape
    hb = 1 << level
    if level >= 2:
        blk = min(2 * hb, c)
        return jnp.concatenate(
            [jnp.broadcast_to(p_hb[b * blk + hb - 1:b * blk + hb, :], (blk, w)) for b in range(c // blk)], axis=0)
    x = p_hb.reshape(c // 8, 8, w)
    odd = (row8 & 1) == 1
    if level == 0:
        g = jnp.where(odd, pltpu.roll(x, 1, 1), x)
    else:
        z = jnp.where(odd, x, pltpu.roll(x, 7, 1))
        g = jnp.where((row8 & 2) == 0, z, pltpu.roll(z, 2, 1))
    return g.reshape(c, w)


def _hgrn_kernel(qs_ref, lf_ref, kk_ref, v_ref, gs_ref, ng_ref, o_ref, st_ref, *, heads_per_block,
                 chunks_per_iter):
    c = CHUNK
    dk = HEAD_DIM
    w = qs_ref.shape[1]
    n_levels = 6

    @pl.when(pl.program_id(2) == 0)
    def _():
        st_ref[...] = jnp.zeros_like(st_ref)

    row = lax.broadcasted_iota(jnp.int32, (c, w), 0)
    row8 = lax.broadcasted_iota(jnp.int32, (c // 8, 8, w), 1)
    upper = [((row >> p) & 1) == 1 for p in range(n_levels)]
    ti = lax.broadcasted_iota(jnp.int32, (c, c), 0)
    si = lax.broadcasted_iota(jnp.int32, (c, c), 1)
    diag_mask = ti == si
    level_masks = [((ti >> (p + 1)) == (si >> (p + 1))) & (((ti >> p) & 1) == 1) & (((si >> p) & 1) == 0)
                   for p in range(n_levels)]

    def iter_body(it, carry):
        pre = []
        for cc in range(chunks_per_iter):
            rows = pl.ds(pl.multiple_of((it * chunks_per_iter + cc) * c, c), c)
            p_hb = lf_ref[rows, :]
            e_q, e_k = [], []
            for p in range(n_levels + 1):
                g = _lower_half_total(p_hb, p, row8)
                e_q.append(jnp.exp2(p_hb))
                e_k.append(None if p == 0 else jnp.exp2(g - p_hb))
                if p < n_levels:
                    p_hb = p_hb + jnp.where(upper[p], g, 0.0)
            pre.append((rows, e_q, e_k))
        indep = {}
        for cc, (rows, e_q, e_k) in enumerate(pre):
            for hh in range(heads_per_block):
                cols = slice(hh * dk, (hh + 1) * dk)
                qb = qs_ref[rows, cols]
                kb = kk_ref[rows, cols]
                v = v_ref[rows, cols]
                q = qb.astype(F32)
                k = kb.astype(F32)
                parts = [_dot_nt(qb, kb)]
                for p in range(n_levels):
                    qt = (q * e_q[p][:, cols]).astype(BF16)
                    kt = kb if p == 0 else (k * e_k[p][:, cols]).astype(BF16)
                    parts.append(_dot_nt(qt, kt))
                e_in = e_q[n_levels][:, cols]
                e_out = e_k[n_levels][:, cols]
                kv = _dot_tn(v, (k * e_out).astype(BF16))
                indep[cc, hh] = (parts, kv, (q * e_in).astype(BF16), v, e_in[c - 1:c, :])
        for hh in range(heads_per_block):
            cols = slice(hh * dk, (hh + 1) * dk)
            st = st_ref[hh]
            for cc, (rows, _, _) in enumerate(pre):
                parts, kv, q_in, v, decay_all = indep[cc, hh]
                scores = jnp.where(diag_mask, parts[0], 0.0)
                for p in range(n_levels):
                    scores = jnp.where(level_masks[p], parts[p + 1], scores)
                o = _dot(scores.astype(BF16), v) + _dot_nt(q_in, st.astype(BF16))
                st = st * decay_all + kv
                inv = lax.rsqrt(jnp.mean(o * o, axis=-1, keepdims=True) + NORM_EPS)
                o = o * inv * ng_ref[:, cols] * gs_ref[rows, cols].astype(F32)
                o_ref[rows, cols] = o.astype(o_ref.dtype)
            st_ref[hh] = st
        return carry

    lax.fori_loop(0, qs_ref.shape[0] // (c * chunks_per_iter), iter_body, 0)


def hgrn(qs, lf, kk, v, gs, norm_g, *, batch, seq, block_len=512, heads_per_block=4, chunks_per_iter=2):
    t, width = qs.shape
    bw = heads_per_block * HEAD_DIM
    n_l = seq // block_len
    tok = pl.BlockSpec((block_len, bw), lambda b, h, l: (b * n_l + l, h))
    pipelined = 4 * _nbytes((block_len, bw), BF16) + _nbytes((block_len, bw), F32)
    return pl.pallas_call(
        functools.partial(_hgrn_kernel, heads_per_block=heads_per_block, chunks_per_iter=chunks_per_iter),
        grid=(batch, width // bw, n_l),
        in_specs=[tok, tok, tok, tok, tok,
                  pl.BlockSpec((1, bw), lambda b, h, l: (0, h))],
        out_specs=tok,
        out_shape=jax.ShapeDtypeStruct((t, width), BF16),
        scratch_shapes=[pltpu.VMEM((heads_per_block, HEAD_DIM, HEAD_DIM), F32)],
        compiler_params=pltpu.CompilerParams(
            dimension_semantics=("parallel", "parallel", "arbitrary"),
            vmem_limit_bytes=_vmem_limit(pipelined, 0)),
        name="hgrn",
    )(qs, lf, kk, v, gs, norm_g.reshape(1, width))


ATTN_Q_BLOCK = CHUNK
ATTN_WINDOW = 640
ATTN_LEAD = ATTN_WINDOW - ATTN_Q_BLOCK
ATTN_BIAS_LANES = 768


def _attn_bias_row(rel_bias):
    m = np.arange(ATTN_BIAS_LANES)
    m = np.where(m >= ATTN_WINDOW, m - ATTN_BIAS_LANES, m)
    idx = np.clip(ATTN_LEAD - m, -MAX_REL, MAX_REL) + MAX_REL
    runs, start = [], 0
    for pos in range(1, len(idx) + 1):
        if pos == len(idx) or idx[pos] != idx[pos - 1] - 1:
            runs.append((start, pos))
            start = pos
    pieces = [jnp.flip(rel_bias[:, int(idx[hi - 1]):int(idx[lo]) + 1], axis=1) for lo, hi in runs]
    return jnp.concatenate(pieces, axis=1)


def _attn_kernel(q_ref, k_ref, v_ref, brow_ref, o_ref, kpad_ref, vpad_ref, *, group):
    seq = q_ref.shape[0]
    kpad_ref[0:ATTN_LEAD, :] = jnp.zeros((ATTN_LEAD, HEAD_DIM), kpad_ref.dtype)
    vpad_ref[0:ATTN_LEAD, :] = jnp.zeros((ATTN_LEAD, HEAD_DIM), vpad_ref.dtype)
    kpad_ref[ATTN_LEAD:, :] = k_ref[...]
    vpad_ref[ATTN_LEAD:, :] = v_ref[...]
    brow = jnp.broadcast_to(brow_ref[0], (ATTN_Q_BLOCK, ATTN_BIAS_LANES))
    toep = pltpu.roll(brow, 0, 1, stride=1, stride_axis=0)[:, :ATTN_WINDOW]
    qi = lax.broadcasted_iota(jnp.int32, (ATTN_Q_BLOCK, ATTN_WINDOW), 0)
    col = lax.broadcasted_iota(jnp.int32, (ATTN_Q_BLOCK, ATTN_WINDOW), 1)
    qc = qi // CHUNK
    kc = col // CHUNK
    lead = ATTN_LEAD // CHUNK
    bias = jnp.where((kc >= qc + lead - LEFT_CHUNKS) & (kc <= qc + lead), toep, MASK_VALUE)

    def do_group(base, masked):
        starts = [base + u * ATTN_Q_BLOCK for u in range(group)]
        scores = []
        for start in starts:
            q = q_ref[pl.ds(start, ATTN_Q_BLOCK), :]
            kw = kpad_ref[pl.ds(start, ATTN_WINDOW), :]
            s = _dot_nt(q, kw) + bias
            if masked:
                s = jnp.where(col >= ATTN_LEAD - start, s, MASK_VALUE)
            scores.append(s)
        probs = []
        for s in scores:
            m = jnp.max(s, axis=-1, keepdims=True)
            p = jnp.exp(s - m)
            probs.append((p.astype(BF16), jnp.sum(p, axis=-1, keepdims=True)))
        for start, (p, denom) in zip(starts, probs):
            vw = vpad_ref[pl.ds(start, ATTN_WINDOW), :]
            o = _dot(p, vw) / denom
            o_ref[pl.ds(start, ATTN_Q_BLOCK), :] = o.astype(o_ref.dtype)

    span = group * ATTN_Q_BLOCK
    n_masked = pl.cdiv(ATTN_LEAD, span)
    for g in range(n_masked):
        do_group(g * span, True)

    def body(g, carry):
        do_group(pl.multiple_of(g * span, span), False)
        return carry

    lax.fori_loop(n_masked, seq // span, body, 0)


def band_attn(q, k, v, bias_row, *, batch, seq, group=8):
    t, width = q.shape
    tok = pl.BlockSpec((seq, HEAD_DIM), lambda b, h: (b, h))
    return pl.pallas_call(
        functools.partial(_attn_kernel, group=group),
        grid=(batch, width // HEAD_DIM),
        in_specs=[tok, tok, tok,
                  pl.BlockSpec((1, 1, ATTN_BIAS_LANES), lambda b, h: (h, 0, 0))],
        out_specs=tok,
        out_shape=jax.ShapeDtypeStruct((t, width), BF16),
        scratch_shapes=[pltpu.VMEM((seq + ATTN_LEAD, HEAD_DIM), BF16),
                        pltpu.VMEM((seq + ATTN_LEAD, HEAD_DIM), BF16)],
        compiler_params=pltpu.CompilerParams(
            dimension_semantics=("parallel", "parallel"),
            vmem_limit_bytes=_vmem_limit(
                4 * _nbytes((seq, HEAD_DIM), BF16),
                2 * _nbytes((seq + ATTN_LEAD, HEAD_DIM), BF16))),
        name="band_attn",
    )(q, k, v, bias_row.reshape(bias_row.shape[0], 1, ATTN_BIAS_LANES))


def _mix_up_kernel(oa_ref, ob_ref, wa_ref, wb_ref, ga_ref, gb_ref, o_ref):
    ya = _dot(oa_ref[...], wa_ref[...])
    yb = _dot(ob_ref[...], wb_ref[...])
    o_ref[...] = (ga_ref[...].astype(F32) * ya + gb_ref[...].astype(F32) * yb).astype(o_ref.dtype)


def mix_up(oa, ob, wa, wb, ga, gb, *, tm=1024, tn=512):
    t, kdim = oa.shape
    d = wa.shape[1]
    pipelined = (2 * _nbytes((tm, kdim), BF16) + 2 * _nbytes((kdim, tn), BF16) + 3 * _nbytes((tm, tn), BF16))
    return pl.pallas_call(
        _mix_up_kernel,
        grid=(t // tm, d // tn),
        in_specs=[pl.BlockSpec((tm, kdim), lambda i, j: (i, 0)),
                  pl.BlockSpec((tm, kdim), lambda i, j: (i, 0)),
                  pl.BlockSpec((kdim, tn), lambda i, j: (0, j)),
                  pl.BlockSpec((kdim, tn), lambda i, j: (0, j)),
                  pl.BlockSpec((tm, tn), lambda i, j: (i, j)),
                  pl.BlockSpec((tm, tn), lambda i, j: (i, j))],
        out_specs=pl.BlockSpec((tm, tn), lambda i, j: (i, j)),
        out_shape=jax.ShapeDtypeStruct((t, d), BF16),
        compiler_params=pltpu.CompilerParams(
            dimension_semantics=("parallel", "arbitrary"),
            vmem_limit_bytes=_vmem_limit(pipelined, 4 * _nbytes((tm, tn), F32))),
        name="mix_up",
    )(oa, ob, wa, wb, ga, gb)


def _ffn(x, h, w1, w3, w2, post_g, next_g, emit_next):
    g = glu_up(h, w1, w3)
    y = matmul_kres(g, w2.astype(BF16))
    return resnorm(y, x, post_g, next_g, scale=MACARON_WEIGHT, emit_next=emit_next)


def kernel(x, ffn1_pre_g, ffn1_post_g, ffn1_w1, ffn1_w3, ffn1_w2, mix_pre_g, mix_post_g, w_in, b_gate,
           hgrn_lb_logits, hgrn_norm_g, rel_bias, w_up_a, w_up_b, w_out,
           ffn2_pre_g, ffn2_post_g, ffn2_w1, ffn2_w3, ffn2_w2):
    batch, seq, d = x.shape
    depth = ffn1_w1.shape[0]
    d_half = d // 2
    lower_bounds = jnp.cumsum(jax.nn.softmax(hgrn_lb_logits.astype(F32), axis=0), axis=0)
    xt = x.reshape(batch * seq, d)
    h = norm_cast(xt, ffn1_pre_g[0])
    for layer in range(depth):
        xt, h = _ffn(xt, h, ffn1_w1[layer], ffn1_w3[layer], ffn1_w2[layer],
                     ffn1_post_g[layer], mix_pre_g[layer], True)

        w_in_b = w_in[layer].astype(BF16)
        qs, lf, kk, vv, gs = proj(h, w_in_b, [0, d_half, 2 * d_half, 3 * d_half], d_half,
                                  [lower_bounds[layer]], _hgrn_epilogue, [BF16, F32, BF16, BF16, BF16])
        qb, kb, vb = proj(h, w_in_b, [4 * d_half, 5 * d_half, 6 * d_half], d_half,
                          [], _attn_epilogue, [BF16, BF16, BF16])
        ga, gb = proj(h, w_in_b, [7 * d_half, 7 * d_half + d], d,
                      [b_gate[layer, 0], b_gate[layer, 1]], _gate_epilogue, [BF16, BF16], tn=512)

        oa = hgrn(qs, lf, kk, vv, gs, hgrn_norm_g[layer], batch=batch, seq=seq)
        ob = band_attn(qb, kb, vb, _attn_bias_row(rel_bias[layer].astype(F32)), batch=batch, seq=seq)

        m = mix_up(oa, ob, w_up_a[layer].astype(BF16), w_up_b[layer].astype(BF16), ga, gb)
        last = layer == depth - 1
        y = matmul_kres(m, w_out[layer].astype(BF16), tm=1024)
        xt, h = resnorm(y, xt, mix_post_g[layer], ffn2_pre_g[layer], scale=1.0, emit_next=True)
        next_pre = ffn1_pre_g[layer + 1] if not last else ffn2_pre_g[layer]
        xt, h = _ffn(xt, h, ffn2_w1[layer], ffn2_w3[layer], ffn2_w2[layer],
                     ffn2_post_g[layer], next_pre, not last)
    return xt.reshape(batch, seq, d)
```

```python
import functools

import numpy as np
import jax
import jax.numpy as jnp
from jax import lax
from jax.experimental import pallas as pl
from jax.experimental.pallas import tpu as pltpu

F32 = jnp.float32
BF16 = jnp.bfloat16

NORM_EPS = 1e-6
LOG2_E = 1.4426950408889634
MACARON_WEIGHT = 0.5
CHUNK = 64
HEAD_DIM = 128
BF16_SUBLANES = 16
LEFT_CHUNKS = 8
MAX_REL = 256
MASK_VALUE = -1e30

V7X_VMEM_BYTES = 64 * 1024 * 1024
VMEM_LIMIT_CAP = V7X_VMEM_BYTES - 6 * 1024 * 1024


def _vmem_limit(pipelined_bytes, resident_bytes=0):
    est = 2 * pipelined_bytes + resident_bytes + 8 * 1024 * 1024
    return int(min(max(est, 32 * 1024 * 1024), VMEM_LIMIT_CAP))


def _nbytes(shape, dtype):
    return int(np.prod(shape)) * jnp.dtype(dtype).itemsize


def _dot(a, b):
    return jnp.dot(a, b, preferred_element_type=F32)


def _dot_nt(a, b):
    return lax.dot_general(a, b, (((1,), (1,)), ((), ())), preferred_element_type=F32)


def _dot_tn(a, b):
    return lax.dot_general(a, b, (((0,), (0,)), ((), ())), preferred_element_type=F32)


def _silu(x):
    return x * jax.nn.sigmoid(x)


def _side_cast_specs(side, n_i, n_j):
    rows, cols = side.shape
    slab, rem = divmod(rows, n_i * n_j)
    assert rem == 0 and slab % BF16_SUBLANES == 0, (rows, n_i, n_j)
    spec = pl.BlockSpec((slab, cols), lambda i, j: (i * n_j + j, 0))
    return spec, jax.ShapeDtypeStruct((rows, cols), BF16), _nbytes((slab, cols), F32) + _nbytes((slab, cols), BF16)


def _norm_cast_kernel(x_ref, g_ref, o_ref):
    x = x_ref[...]
    inv = lax.rsqrt(jnp.mean(x * x, axis=-1, keepdims=True) + NORM_EPS)
    o_ref[...] = (x * inv * g_ref[...]).astype(o_ref.dtype)


def norm_cast(x, g, *, tm=256):
    t, d = x.shape
    return pl.pallas_call(
        _norm_cast_kernel,
        grid=(t // tm,),
        in_specs=[pl.BlockSpec((tm, d), lambda i: (i, 0)),
                  pl.BlockSpec((1, d), lambda i: (0, 0))],
        out_specs=pl.BlockSpec((tm, d), lambda i: (i, 0)),
        out_shape=jax.ShapeDtypeStruct((t, d), BF16),
        compiler_params=pltpu.CompilerParams(
            dimension_semantics=("parallel",),
            vmem_limit_bytes=_vmem_limit(_nbytes((tm, d), F32) + _nbytes((tm, d), BF16),
                                         3 * _nbytes((tm, d), F32))),
        name="norm_cast",
    )(x, g.reshape(1, d))


def _glu_up_kernel(h_ref, w1_ref, w3_ref, side_ref, o_ref, side_o_ref):
    h = h_ref[...]
    a = _dot(h, w1_ref[...].astype(BF16))
    b = _dot(h, w3_ref[...].astype(BF16))
    o_ref[...] = (_silu(a) * b).astype(o_ref.dtype)
    side_o_ref[...] = side_ref[...].astype(side_o_ref.dtype)


def glu_up(h, w1, w3, side, *, tm=2048, tn=256):
    t, d = h.shape
    f = w1.shape[1]
    grid = (t // tm, f // tn)
    side_spec, side_shape, side_bytes = _side_cast_specs(side, *grid)
    return pl.pallas_call(
        _glu_up_kernel,
        grid=grid,
        in_specs=[pl.BlockSpec((tm, d), lambda i, j: (i, 0), pipeline_mode=pl.Buffered(1)),
                  pl.BlockSpec((d, tn), lambda i, j: (0, j)),
                  pl.BlockSpec((d, tn), lambda i, j: (0, j)),
                  side_spec],
        out_specs=[pl.BlockSpec((tm, tn), lambda i, j: (i, j)), side_spec],
        out_shape=[jax.ShapeDtypeStruct((t, f), BF16), side_shape],
        compiler_params=pltpu.CompilerParams(
            dimension_semantics=("parallel", "arbitrary"),
            vmem_limit_bytes=_vmem_limit(
                2 * _nbytes((d, tn), w1.dtype) + _nbytes((tm, tn), BF16) + side_bytes,
                _nbytes((tm, d), BF16) + 4 * _nbytes((tm, tn), F32) + 2 * _nbytes((d, tn), BF16))),
        name="glu_up",
    )(h, w1, w3, side)


def _matmul_kernel(a_ref, w_ref, *rest):
    maybe_side_ref, o_ref, maybe_side_o_ref = rest if len(rest) == 3 else (None, rest[0], None)
    o_ref[...] = _dot(a_ref[...], w_ref[...]).astype(o_ref.dtype)
    if maybe_side_ref is not None:
        maybe_side_o_ref[...] = maybe_side_ref[...].astype(maybe_side_o_ref.dtype)


def matmul_kres(a, w, side=None, *, tm=512, tn=512):
    t, kdim = a.shape
    n = w.shape[1]
    grid = (t // tm, n // tn)
    in_specs = [pl.BlockSpec((tm, kdim), lambda i, j: (i, 0)),
                pl.BlockSpec((kdim, tn), lambda i, j: (0, j))]
    out_specs = [pl.BlockSpec((tm, tn), lambda i, j: (i, j))]
    out_shape = [jax.ShapeDtypeStruct((t, n), BF16)]
    operands = [a, w]
    side_bytes = 0
    if side is not None:
        side_spec, side_shape, side_bytes = _side_cast_specs(side, *grid)
        in_specs.append(side_spec)
        out_specs.append(side_spec)
        out_shape.append(side_shape)
        operands.append(side)
    outs = pl.pallas_call(
        _matmul_kernel,
        grid=grid,
        in_specs=in_specs,
        out_specs=out_specs,
        out_shape=out_shape,
        compiler_params=pltpu.CompilerParams(
            dimension_semantics=("parallel", "arbitrary"),
            vmem_limit_bytes=_vmem_limit(
                _nbytes((tm, kdim), BF16) + _nbytes((kdim, tn), BF16) + _nbytes((tm, tn), BF16) + side_bytes,
                2 * _nbytes((tm, tn), F32))),
        name="matmul_kres",
    )(*operands)
    return outs if side is not None else outs[0]


def _resnorm_kernel(y_ref, x_ref, gpost_ref, gnext_ref, xo_ref, *maybe_h_ref, scale):
    y = y_ref[...].astype(F32)
    inv = lax.rsqrt(jnp.mean(y * y, axis=-1, keepdims=True) + NORM_EPS)
    xn = x_ref[...] + scale * (y * inv * gpost_ref[...])
    xo_ref[...] = xn
    if maybe_h_ref:
        inv2 = lax.rsqrt(jnp.mean(xn * xn, axis=-1, keepdims=True) + NORM_EPS)
        maybe_h_ref[0][...] = (xn * inv2 * gnext_ref[...]).astype(BF16)


def resnorm(y, x, g_post, g_next, *, scale, emit_next, tm=256):
    t, d = x.shape
    row = pl.BlockSpec((tm, d), lambda i: (i, 0))
    vec = pl.BlockSpec((1, d), lambda i: (0, 0))
    out_shape = [jax.ShapeDtypeStruct((t, d), F32)]
    if emit_next:
        out_shape.append(jax.ShapeDtypeStruct((t, d), BF16))
    pipelined = (_nbytes((tm, d), y.dtype) + 2 * _nbytes((tm, d), F32)
                 + (_nbytes((tm, d), BF16) if emit_next else 0))
    outs = pl.pallas_call(
        functools.partial(_resnorm_kernel, scale=scale),
        grid=(t // tm,),
        in_specs=[row, row, vec, vec],
        out_specs=[row] * len(out_shape),
        out_shape=out_shape,
        compiler_params=pltpu.CompilerParams(
            dimension_semantics=("parallel",),
            vmem_limit_bytes=_vmem_limit(pipelined, 3 * _nbytes((tm, d), F32))),
        name="resnorm",
    )(y, x, g_post.reshape(1, d), g_next.reshape(1, d))
    return outs if emit_next else (outs[0], None)


def _proj_kernel(*refs, n_seg, n_aux, epilogue):
    h_ref = refs[0]
    w_refs = refs[1:1 + n_seg]
    aux_refs = refs[1 + n_seg:1 + n_seg + n_aux]
    side_ref = refs[1 + n_seg + n_aux]
    out_refs = refs[2 + n_seg + n_aux:-1]
    side_o_ref = refs[-1]
    h = h_ref[...]
    ys = [_dot(h, w_ref[...]) for w_ref in w_refs]
    outs = epilogue(ys, [r[...] for r in aux_refs])
    for o_ref, o in zip(out_refs, outs):
        o_ref[...] = o.astype(o_ref.dtype)
    side_o_ref[...] = side_ref[...].astype(side_o_ref.dtype)


def proj(h, w_in, col_starts, width, aux, epilogue, out_dtypes, side, *, tm=1024, tn=256):
    t, d = h.shape
    n_seg = len(col_starts)
    grid = (t // tm, width // tn)
    side_spec, side_shape, side_bytes = _side_cast_specs(side, *grid)

    def w_spec(start):
        off = start // tn
        return pl.BlockSpec((d, tn), lambda i, j: (0, off + j))

    pipelined = (_nbytes((tm, d), BF16) + n_seg * _nbytes((d, tn), BF16)
                 + sum(_nbytes((tm, tn), dt) for dt in out_dtypes) + side_bytes)
    return pl.pallas_call(
        functools.partial(_proj_kernel, n_seg=n_seg, n_aux=len(aux), epilogue=epilogue),
        grid=grid,
        in_specs=([pl.BlockSpec((tm, d), lambda i, j: (i, 0))]
                  + [w_spec(s) for s in col_starts]
                  + [pl.BlockSpec((1, tn), lambda i, j: (0, j)) for _ in aux]
                  + [side_spec]),
        out_specs=[pl.BlockSpec((tm, tn), lambda i, j: (i, j)) for _ in out_dtypes] + [side_spec],
        out_shape=[jax.ShapeDtypeStruct((t, width), dt) for dt in out_dtypes] + [side_shape],
        compiler_params=pltpu.CompilerParams(
            dimension_semantics=("parallel", "arbitrary"),
            vmem_limit_bytes=_vmem_limit(pipelined, (n_seg + 4) * _nbytes((tm, tn), F32))),
        name="proj",
    )(h, *([w_in] * n_seg), *[a.reshape(1, width) for a in aux], side)


def _hgrn_epilogue(ys, aux):
    q, fr, i, g = ys
    lb, = aux
    f = lb + (1.0 - lb) * jax.nn.sigmoid(fr)
    return _silu(q), jnp.log(f) * LOG2_E, 1.0 - f, i, _silu(g)


def _attn_epilogue(ys, aux):
    q, k, v = ys
    return q * (HEAD_DIM ** -0.5), k, v


def _gate_epilogue(ys, aux):
    ga, gb = ys
    ba, bb = aux
    return jax.nn.sigmoid(ga + ba), jax.nn.sigmoid(gb + bb)


def _lower_half_total(p_hb, level, row8):
    c, w = p_hb.shape
    hb = 1 << level
    if level >= 2:
        blk = min(2 * hb, c)
        return jnp.concatenate(
            [jnp.broadcast_to(p_hb[b * blk + hb - 1:b * blk + hb, :], (blk, w)) for b in range(c // blk)], axis=0)
    x = p_hb.reshape(c // 8, 8, w)
    odd = (row8 & 1) == 1
    if level == 0:
        g = jnp.where(odd, pltpu.roll(x, 1, 1), x)
    else:
        z = jnp.where(odd, x, pltpu.roll(x, 7, 1))
        g = jnp.where((row8 & 2) == 0, z, pltpu.roll(z, 2, 1))
    return g.reshape(c, w)


def _hgrn_kernel(qs_ref, lf_ref, kk_ref, v_ref, gs_ref, ng_ref, o_ref, st_ref, *, heads_per_block,
                 chunks_per_iter):
    c = CHUNK
    dk = HEAD_DIM
    w = qs_ref.shape[1]
    n_levels = 6

    @pl.when(pl.program_id(2) == 0)
    def _():
        st_ref[...] = jnp.zeros_like(st_ref)

    row = lax.broadcasted_iota(jnp.int32, (c, w), 0)
    row8 = lax.broadcasted_iota(jnp.int32, (c // 8, 8, w), 1)
    upper = [((row >> p) & 1) == 1 for p in range(n_levels)]
    ti = lax.broadcasted_iota(jnp.int32, (c, c), 0)
    si = lax.broadcasted_iota(jnp.int32, (c, c), 1)
    diag_mask = ti == si
    level_masks = [((ti >> (p + 1)) == (si >> (p + 1))) & (((ti >> p) & 1) == 1) & (((si >> p) & 1) == 0)
                   for p in range(n_levels)]

    def iter_body(it, carry):
        pre = []
        for cc in range(chunks_per_iter):
            rows = pl.ds(pl.multiple_of((it * chunks_per_iter + cc) * c, c), c)
            p_hb = lf_ref[rows, :]
            e_q, e_k = [], []
            for p in range(n_levels + 1):
                g = _lower_half_total(p_hb, p, row8)
                e_q.append(jnp.exp2(p_hb))
                e_k.append(None if p == 0 else jnp.exp2(g - p_hb))
                if p < n_levels:
                    p_hb = p_hb + jnp.where(upper[p], g, 0.0)
            pre.append((rows, e_q, e_k))
        indep = {}
        for cc, (rows, e_q, e_k) in enumerate(pre):
            for hh in range(heads_per_block):
                cols = slice(hh * dk, (hh + 1) * dk)
                qb = qs_ref[rows, cols]
                kb = kk_ref[rows, cols]
                v = v_ref[rows, cols]
                q = qb.astype(F32)
                k = kb.astype(F32)
                parts = [_dot_nt(qb, kb)]
                for p in range(n_levels):
                    qt = (q * e_q[p][:, cols]).astype(BF16)
                    kt = kb if p == 0 else (k * e_k[p][:, cols]).astype(BF16)
                    parts.append(_dot_nt(qt, kt))
                e_in = e_q[n_levels][:, cols]
                e_out = e_k[n_levels][:, cols]
                kv = _dot_tn(v, (k * e_out).astype(BF16))
                indep[cc, hh] = (parts, kv, (q * e_in).astype(BF16), v, e_in[c - 1:c, :])
        for hh in range(heads_per_block):
            cols = slice(hh * dk, (hh + 1) * dk)
            st = st_ref[hh]
            for cc, (rows, _, _) in enumerate(pre):
                parts, kv, q_in, v, decay_all = indep[cc, hh]
                scores = jnp.where(diag_mask, parts[0], 0.0)
                for p in range(n_levels):
                    scores = jnp.where(level_masks[p], parts[p + 1], scores)
                o = _dot(scores.astype(BF16), v) + _dot_nt(q_in, st.astype(BF16))
                st = st * decay_all + kv
                inv = lax.rsqrt(jnp.mean(o * o, axis=-1, keepdims=True) + NORM_EPS)
                o = o * inv * ng_ref[:, cols] * gs_ref[rows, cols].astype(F32)
                o_ref[rows, cols] = o.astype(o_ref.dtype)
            st_ref[hh] = st
        return carry

    lax.fori_loop(0, qs_ref.shape[0] // (c * chunks_per_iter), iter_body, 0)


def hgrn(qs, lf, kk, v, gs, norm_g, *, batch, seq, block_len=512, heads_per_block=4, chunks_per_iter=2):
    t, width = qs.shape
    bw = heads_per_block * HEAD_DIM
    n_l = seq // block_len
    tok = pl.BlockSpec((block_len, bw), lambda b, h, l: (b * n_l + l, h))
    pipelined = 4 * _nbytes((block_len, bw), BF16) + _nbytes((block_len, bw), F32)
    return pl.pallas_call(
        functools.partial(_hgrn_kernel, heads_per_block=heads_per_block, chunks_per_iter=chunks_per_iter),
        grid=(batch, width // bw, n_l),
        in_specs=[tok, tok, tok, tok, tok,
                  pl.BlockSpec((1, bw), lambda b, h, l: (0, h))],
        out_specs=tok,
        out_shape=jax.ShapeDtypeStruct((t, width), BF16),
        scratch_shapes=[pltpu.VMEM((heads_per_block, HEAD_DIM, HEAD_DIM), F32)],
        compiler_params=pltpu.CompilerParams(
            dimension_semantics=("parallel", "parallel", "arbitrary"),
            vmem_limit_bytes=_vmem_limit(pipelined, 0)),
        name="hgrn",
    )(qs, lf, kk, v, gs, norm_g.reshape(1, width))


ATTN_Q_BLOCK = CHUNK
ATTN_WINDOW = 640
ATTN_LEAD = ATTN_WINDOW - ATTN_Q_BLOCK
ATTN_BIAS_LANES = 768


def _attn_bias_row(rel_bias):
    m = np.arange(ATTN_BIAS_LANES)
    m = np.where(m >= ATTN_WINDOW, m - ATTN_BIAS_LANES, m)
    idx = np.clip(ATTN_LEAD - m, -MAX_REL, MAX_REL) + MAX_REL
    runs, start = [], 0
    for pos in range(1, len(idx) + 1):
        if pos == len(idx) or idx[pos] != idx[pos - 1] - 1:
            runs.append((start, pos))
            start = pos
    pieces = [jnp.flip(rel_bias[:, int(idx[hi - 1]):int(idx[lo]) + 1], axis=1) for lo, hi in runs]
    return jnp.concatenate(pieces, axis=1)


def _attn_kernel(q_ref, k_ref, v_ref, brow_ref, o_ref, kpad_ref, vpad_ref, *, group):
    seq = q_ref.shape[0]
    kpad_ref[0:ATTN_LEAD, :] = jnp.zeros((ATTN_LEAD, HEAD_DIM), kpad_ref.dtype)
    vpad_ref[0:ATTN_LEAD, :] = jnp.zeros((ATTN_LEAD, HEAD_DIM), vpad_ref.dtype)
    kpad_ref[ATTN_LEAD:, :] = k_ref[...]
    vpad_ref[ATTN_LEAD:, :] = v_ref[...]
    brow = jnp.broadcast_to(brow_ref[0], (ATTN_Q_BLOCK, ATTN_BIAS_LANES))
    toep = pltpu.roll(brow, 0, 1, stride=1, stride_axis=0)[:, :ATTN_WINDOW]
    qi = lax.broadcasted_iota(jnp.int32, (ATTN_Q_BLOCK, ATTN_WINDOW), 0)
    col = lax.broadcasted_iota(jnp.int32, (ATTN_Q_BLOCK, ATTN_WINDOW), 1)
    qc = qi // CHUNK
    kc = col // CHUNK
    lead = ATTN_LEAD // CHUNK
    bias = jnp.where((kc >= qc + lead - LEFT_CHUNKS) & (kc <= qc + lead), toep, MASK_VALUE)

    def do_group(base, masked):
        starts = [base + u * ATTN_Q_BLOCK for u in range(group)]
        scores = []
        for start in starts:
            q = q_ref[pl.ds(start, ATTN_Q_BLOCK), :]
            kw = kpad_ref[pl.ds(start, ATTN_WINDOW), :]
            s = _dot_nt(q, kw) + bias
            if masked:
                s = jnp.where(col >= ATTN_LEAD - start, s, MASK_VALUE)
            scores.append(s)
        probs = []
        for s in scores:
            m = jnp.max(s, axis=-1, keepdims=True)
            p = jnp.exp(s - m)
            probs.append((p.astype(BF16), jnp.sum(p, axis=-1, keepdims=True)))
        for start, (p, denom) in zip(starts, probs):
            vw = vpad_ref[pl.ds(start, ATTN_WINDOW), :]
            o = _dot(p, vw) / denom
            o_ref[pl.ds(start, ATTN_Q_BLOCK), :] = o.astype(o_ref.dtype)

    span = group * ATTN_Q_BLOCK
    n_masked = pl.cdiv(ATTN_LEAD, span)
    for g in range(n_masked):
        do_group(g * span, True)

    def body(g, carry):
        do_group(pl.multiple_of(g * span, span), False)
        return carry

    lax.fori_loop(n_masked, seq // span, body, 0)


def band_attn(q, k, v, bias_row, *, batch, seq, group=8):
    t, width = q.shape
    tok = pl.BlockSpec((seq, HEAD_DIM), lambda b, h: (b, h))
    return pl.pallas_call(
        functools.partial(_attn_kernel, group=group),
        grid=(batch, width // HEAD_DIM),
        in_specs=[tok, tok, tok,
                  pl.BlockSpec((1, 1, ATTN_BIAS_LANES), lambda b, h: (h, 0, 0))],
        out_specs=tok,
        out_shape=jax.ShapeDtypeStruct((t, width), BF16),
        scratch_shapes=[pltpu.VMEM((seq + ATTN_LEAD, HEAD_DIM), BF16),
                        pltpu.VMEM((seq + ATTN_LEAD, HEAD_DIM), BF16)],
        compiler_params=pltpu.CompilerParams(
            dimension_semantics=("parallel", "parallel"),
            vmem_limit_bytes=_vmem_limit(
                4 * _nbytes((seq, HEAD_DIM), BF16),
                2 * _nbytes((seq + ATTN_LEAD, HEAD_DIM), BF16))),
        name="band_attn",
    )(q, k, v, bias_row.reshape(bias_row.shape[0], 1, ATTN_BIAS_LANES))


def _mix_up_kernel(oa_ref, ob_ref, wa_ref, wb_ref, ga_ref, gb_ref, o_ref):
    ya = _dot(oa_ref[...], wa_ref[...])
    yb = _dot(ob_ref[...], wb_ref[...])
    o_ref[...] = (ga_ref[...].astype(F32) * ya + gb_ref[...].astype(F32) * yb).astype(o_ref.dtype)


def mix_up(oa, ob, wa, wb, ga, gb, *, tm=1024, tn=512):
    t, kdim = oa.shape
    d = wa.shape[1]
    pipelined = (2 * _nbytes((tm, kdim), BF16) + 2 * _nbytes((kdim, tn), BF16) + 3 * _nbytes((tm, tn), BF16))
    return pl.pallas_call(
        _mix_up_kernel,
        grid=(t // tm, d // tn),
        in_specs=[pl.BlockSpec((tm, kdim), lambda i, j: (i, 0)),
                  pl.BlockSpec((tm, kdim), lambda i, j: (i, 0)),
                  pl.BlockSpec((kdim, tn), lambda i, j: (0, j)),
                  pl.BlockSpec((kdim, tn), lambda i, j: (0, j)),
                  pl.BlockSpec((tm, tn), lambda i, j: (i, j)),
                  pl.BlockSpec((tm, tn), lambda i, j: (i, j))],
        out_specs=pl.BlockSpec((tm, tn), lambda i, j: (i, j)),
        out_shape=jax.ShapeDtypeStruct((t, d), BF16),
        compiler_params=pltpu.CompilerParams(
            dimension_semantics=("parallel", "arbitrary"),
            vmem_limit_bytes=_vmem_limit(pipelined, 4 * _nbytes((tm, tn), F32))),
        name="mix_up",
    )(oa, ob, wa, wb, ga, gb)


def _ffn(x, h, w1, w3, w2, post_g, next_g, emit_next, side=None):
    g, w2_b = glu_up(h, w1, w3, w2)
    if side is None:
        y, side_b = matmul_kres(g, w2_b), None
    else:
        y, side_b = matmul_kres(g, w2_b, side)
    x_new, h_next = resnorm(y, x, post_g, next_g, scale=MACARON_WEIGHT, emit_next=emit_next)
    return x_new, h_next, side_b


def kernel(x, ffn1_pre_g, ffn1_post_g, ffn1_w1, ffn1_w3, ffn1_w2, mix_pre_g, mix_post_g, w_in, b_gate,
           hgrn_lb_logits, hgrn_norm_g, rel_bias, w_up_a, w_up_b, w_out,
           ffn2_pre_g, ffn2_post_g, ffn2_w1, ffn2_w3, ffn2_w2):
    batch, seq, d = x.shape
    depth = ffn1_w1.shape[0]
    d_half = d // 2
    lower_bounds = jnp.cumsum(jax.nn.softmax(hgrn_lb_logits.astype(F32), axis=0), axis=0)
    xt = x.reshape(batch * seq, d)
    h = norm_cast(xt, ffn1_pre_g[0])
    for layer in range(depth):
        xt, h, w_in_b = _ffn(xt, h, ffn1_w1[layer], ffn1_w3[layer], ffn1_w2[layer],
                             ffn1_post_g[layer], mix_pre_g[layer], True, side=w_in[layer])

        qs, lf, kk, vv, gs, w_up_a_b = proj(h, w_in_b, [0, d_half, 2 * d_half, 3 * d_half], d_half,
                                            [lower_bounds[layer]], _hgrn_epilogue,
                                            [BF16, F32, BF16, BF16, BF16], w_up_a[layer])
        qb, kb, vb, w_up_b_b = proj(h, w_in_b, [4 * d_half, 5 * d_half, 6 * d_half], d_half,
                                    [], _attn_epilogue, [BF16, BF16, BF16], w_up_b[layer])
        ga, gb, w_out_b = proj(h, w_in_b, [7 * d_half, 7 * d_half + d], d,
                               [b_gate[layer, 0], b_gate[layer, 1]], _gate_epilogue, [BF16, BF16],
                               w_out[layer], tn=512)

        oa = hgrn(qs, lf, kk, vv, gs, hgrn_norm_g[layer], batch=batch, seq=seq)
        ob = band_attn(qb, kb, vb, _attn_bias_row(rel_bias[layer].astype(F32)), batch=batch, seq=seq)

        m = mix_up(oa, ob, w_up_a_b, w_up_b_b, ga, gb)
        last = layer == depth - 1
        y = matmul_kres(m, w_out_b, tm=1024)
        xt, h = resnorm(y, xt, mix_post_g[layer], ffn2_pre_g[layer], scale=1.0, emit_next=True)
        next_pre = ffn1_pre_g[layer + 1] if not last else ffn2_pre_g[layer]
        xt, h, _ = _ffn(xt, h, ffn2_w1[layer], ffn2_w3[layer], ffn2_w2[layer],
                        ffn2_post_g[layer], next_pre, not last)
    return xt.reshape(batch, seq, d)
```

```python
import functools

import numpy as np
import jax
import jax.numpy as jnp
from jax import lax
from jax.experimental import pallas as pl
from jax.experimental.pallas import tpu as pltpu

F32 = jnp.float32
BF16 = jnp.bfloat16

NORM_EPS = 1e-6
LOG2_E = 1.4426950408889634
MACARON_WEIGHT = 0.5
CHUNK = 64
HEAD_DIM = 128
BF16_SUBLANES = 16
LEFT_CHUNKS = 8
MAX_REL = 256
MASK_VALUE = -1e30

V7X_VMEM_BYTES = 64 * 1024 * 1024
VMEM_LIMIT_CAP = V7X_VMEM_BYTES - 6 * 1024 * 1024


def _vmem_limit(pipelined_bytes, resident_bytes=0):
    est = 2 * pipelined_bytes + resident_bytes + 8 * 1024 * 1024
    return int(min(max(est, 32 * 1024 * 1024), VMEM_LIMIT_CAP))


def _nbytes(shape, dtype):
    return int(np.prod(shape)) * jnp.dtype(dtype).itemsize


def _dot(a, b):
    return jnp.dot(a, b, preferred_element_type=F32)


def _dot_nt(a, b):
    return lax.dot_general(a, b, (((1,), (1,)), ((), ())), preferred_element_type=F32)


def _dot_tn(a, b):
    return lax.dot_general(a, b, (((0,), (0,)), ((), ())), preferred_element_type=F32)


def _silu(x):
    return x * jax.nn.sigmoid(x)


def _side_cast_specs(side, n_i, n_j):
    rows, cols = side.shape
    steps = n_i * n_j
    slab = max(rows // steps, BF16_SUBLANES)
    n_slabs, rem = divmod(rows, slab)
    steps_per_slab, rem2 = divmod(steps, n_slabs)
    assert rem == 0 and rem2 == 0 and slab % BF16_SUBLANES == 0, (rows, n_i, n_j)
    spec = pl.BlockSpec((slab, cols),
                        lambda i, j: (jnp.where(i < n_i, (i * n_j + j) // steps_per_slab, n_slabs - 1), 0))
    return spec, jax.ShapeDtypeStruct((rows, cols), BF16), _nbytes((slab, cols), F32) + _nbytes((slab, cols), BF16)


def _norm_cast_kernel(x_ref, g_ref, o_ref):
    x = x_ref[...]
    inv = lax.rsqrt(jnp.mean(x * x, axis=-1, keepdims=True) + NORM_EPS)
    o_ref[...] = (x * inv * g_ref[...]).astype(o_ref.dtype)


def norm_cast(x, g, *, tm=256):
    t, d = x.shape
    return pl.pallas_call(
        _norm_cast_kernel,
        grid=(t // tm,),
        in_specs=[pl.BlockSpec((tm, d), lambda i: (i, 0)),
                  pl.BlockSpec((1, d), lambda i: (0, 0))],
        out_specs=pl.BlockSpec((tm, d), lambda i: (i, 0)),
        out_shape=jax.ShapeDtypeStruct((t, d), BF16),
        compiler_params=pltpu.CompilerParams(
            dimension_semantics=("parallel",),
            vmem_limit_bytes=_vmem_limit(_nbytes((tm, d), F32) + _nbytes((tm, d), BF16),
                                         3 * _nbytes((tm, d), F32))),
        name="norm_cast",
    )(x, g.reshape(1, d))


def _glu_up_kernel(h_ref, w1_ref, w3_ref, side_ref, o_ref, side_o_ref):
    h = h_ref[...]
    a = _dot(h, w1_ref[...].astype(BF16))
    b = _dot(h, w3_ref[...].astype(BF16))
    o_ref[...] = (_silu(a) * b).astype(o_ref.dtype)
    side_o_ref[...] = side_ref[...].astype(side_o_ref.dtype)


def glu_up(h, w1, w3, side, *, tm=2048, tn=256):
    t, d = h.shape
    f = w1.shape[1]
    grid = (t // tm, f // tn)
    side_spec, side_shape, side_bytes = _side_cast_specs(side, *grid)
    return pl.pallas_call(
        _glu_up_kernel,
        grid=grid,
        in_specs=[pl.BlockSpec((tm, d), lambda i, j: (i, 0), pipeline_mode=pl.Buffered(1)),
                  pl.BlockSpec((d, tn), lambda i, j: (0, j)),
                  pl.BlockSpec((d, tn), lambda i, j: (0, j)),
                  side_spec],
        out_specs=[pl.BlockSpec((tm, tn), lambda i, j: (i, j)), side_spec],
        out_shape=[jax.ShapeDtypeStruct((t, f), BF16), side_shape],
        compiler_params=pltpu.CompilerParams(
            dimension_semantics=("parallel", "arbitrary"),
            vmem_limit_bytes=_vmem_limit(
                2 * _nbytes((d, tn), w1.dtype) + _nbytes((tm, tn), BF16) + side_bytes,
                _nbytes((tm, d), BF16) + 4 * _nbytes((tm, tn), F32) + 2 * _nbytes((d, tn), BF16))),
        name="glu_up",
    )(h, w1, w3, side)


def _matmul_resnorm_kernel(a_ref, w_ref, x_ref, gpost_ref, gnext_ref, *rest, scale, emit_next, has_side):
    rest = list(rest)
    side_ref = rest.pop(0) if has_side else None
    xo_ref = rest.pop(0)
    h_ref = rest.pop(0) if emit_next else None
    side_o_ref = rest.pop(0) if has_side else None
    y_even_ref, y_odd_ref = rest
    i = pl.program_id(0)
    j = pl.program_id(1)
    n_j, _, tn = y_even_ref.shape
    slab, d = x_ref.shape

    @pl.when((i == 0) & (j == 0))
    def _():
        y_odd_ref[...] = jnp.zeros_like(y_odd_ref)

    def step(y_cur_ref, y_prev_ref):
        y_cur_ref[j] = _dot(a_ref[...], w_ref[...]).astype(y_cur_ref.dtype)

        rows = pl.ds(pl.multiple_of(j * slab, slab), slab)
        ys = [y_prev_ref[jj, rows, :].astype(F32) for jj in range(n_j)]
        sq = ys[0] * ys[0]
        for y in ys[1:]:
            sq = sq + y * y
        inv = lax.rsqrt(jnp.sum(sq, axis=-1, keepdims=True) / d + NORM_EPS)
        xns = []
        for jj, y in enumerate(ys):
            cols = slice(jj * tn, (jj + 1) * tn)
            xn = x_ref[:, cols] + scale * (y * inv * gpost_ref[:, cols])
            xo_ref[:, cols] = xn
            xns.append(xn)
        if emit_next:
            sq = xns[0] * xns[0]
            for xn in xns[1:]:
                sq = sq + xn * xn
            inv2 = lax.rsqrt(jnp.sum(sq, axis=-1, keepdims=True) / d + NORM_EPS)
            for jj, xn in enumerate(xns):
                cols = slice(jj * tn, (jj + 1) * tn)
                h_ref[:, cols] = (xn * inv2 * gnext_ref[:, cols]).astype(h_ref.dtype)
        if has_side:
            side_o_ref[...] = side_ref[...].astype(side_o_ref.dtype)

    @pl.when(i % 2 == 0)
    def _():
        step(y_even_ref, y_odd_ref)

    @pl.when(i % 2 == 1)
    def _():
        step(y_odd_ref, y_even_ref)


def matmul_resnorm(a, w, x, g_post, g_next, *, scale, emit_next, side=None, tm=512, tn=256):
    t, kdim = a.shape
    d = w.shape[1]
    n_i, n_j = t // tm, d // tn
    slab, rem = divmod(tm, n_j)
    assert rem == 0 and slab % BF16_SUBLANES == 0, (tm, n_j)

    def finished_slab(i, j):
        return (jnp.where(i == 0, 0, (i - 1) * n_j + j), 0)

    vec = pl.BlockSpec((1, d), lambda i, j: (0, 0))
    in_specs = [pl.BlockSpec((tm, kdim), lambda i, j: (jnp.minimum(i, n_i - 1), 0)),
                pl.BlockSpec((kdim, tn), lambda i, j: (0, jnp.where(i < n_i, j, n_j - 1))),
                pl.BlockSpec((slab, d), finished_slab), vec, vec]
    operands = [a, w, x, g_post.reshape(1, d), g_next.reshape(1, d)]
    out_specs = [pl.BlockSpec((slab, d), finished_slab)]
    out_shape = [jax.ShapeDtypeStruct((t, d), F32)]
    pipelined = _nbytes((tm, kdim), BF16) + _nbytes((kdim, tn), BF16) + 2 * _nbytes((slab, d), F32)
    if emit_next:
        out_specs.append(pl.BlockSpec((slab, d), finished_slab))
        out_shape.append(jax.ShapeDtypeStruct((t, d), BF16))
        pipelined += _nbytes((slab, d), BF16)
    if side is not None:
        side_spec, side_shape, side_bytes = _side_cast_specs(side, n_i, n_j)
        in_specs.append(side_spec)
        operands.append(side)
        out_specs.append(side_spec)
        out_shape.append(side_shape)
        pipelined += side_bytes
    ybuf_shape = (n_j, tm, tn)
    outs = pl.pallas_call(
        functools.partial(_matmul_resnorm_kernel, scale=scale, emit_next=emit_next, has_side=side is not None),
        grid=(n_i + 1, n_j),
        in_specs=in_specs,
        out_specs=out_specs,
        out_shape=out_shape,
        scratch_shapes=[pltpu.VMEM(ybuf_shape, BF16), pltpu.VMEM(ybuf_shape, BF16)],
        compiler_params=pltpu.CompilerParams(
            dimension_semantics=("arbitrary", "arbitrary"),
            vmem_limit_bytes=_vmem_limit(
                pipelined, 2 * _nbytes(ybuf_shape, BF16) + _nbytes((tm, tn), F32) + 3 * _nbytes((slab, d), F32))),
        name="matmul_resnorm",
    )(*operands)
    x_new = outs[0]
    h_next = outs[1] if emit_next else None
    side_b = outs[-1] if side is not None else None
    return x_new, h_next, side_b


def _proj_kernel(*refs, n_seg, n_aux, epilogue):
    h_ref = refs[0]
    w_refs = refs[1:1 + n_seg]
    aux_refs = refs[1 + n_seg:1 + n_seg + n_aux]
    side_ref = refs[1 + n_seg + n_aux]
    out_refs = refs[2 + n_seg + n_aux:-1]
    side_o_ref = refs[-1]
    h = h_ref[...]
    ys = [_dot(h, w_ref[...]) for w_ref in w_refs]
    outs = epilogue(ys, [r[...] for r in aux_refs])
    for o_ref, o in zip(out_refs, outs):
        o_ref[...] = o.astype(o_ref.dtype)
    side_o_ref[...] = side_ref[...].astype(side_o_ref.dtype)


def proj(h, w_in, col_starts, width, aux, epilogue, out_dtypes, side, *, tm=1024, tn=256):
    t, d = h.shape
    n_seg = len(col_starts)
    grid = (t // tm, width // tn)
    side_spec, side_shape, side_bytes = _side_cast_specs(side, *grid)

    def w_spec(start):
        off = start // tn
        return pl.BlockSpec((d, tn), lambda i, j: (0, off + j))

    pipelined = (_nbytes((tm, d), BF16) + n_seg * _nbytes((d, tn), BF16)
                 + sum(_nbytes((tm, tn), dt) for dt in out_dtypes) + side_bytes)
    return pl.pallas_call(
        functools.partial(_proj_kernel, n_seg=n_seg, n_aux=len(aux), epilogue=epilogue),
        grid=grid,
        in_specs=([pl.BlockSpec((tm, d), lambda i, j: (i, 0))]
                  + [w_spec(s) for s in col_starts]
                  + [pl.BlockSpec((1, tn), lambda i, j: (0, j)) for _ in aux]
                  + [side_spec]),
        out_specs=[pl.BlockSpec((tm, tn), lambda i, j: (i, j)) for _ in out_dtypes] + [side_spec],
        out_shape=[jax.ShapeDtypeStruct((t, width), dt) for dt in out_dtypes] + [side_shape],
        compiler_params=pltpu.CompilerParams(
            dimension_semantics=("parallel", "arbitrary"),
            vmem_limit_bytes=_vmem_limit(pipelined, (n_seg + 4) * _nbytes((tm, tn), F32))),
        name="proj",
    )(h, *([w_in] * n_seg), *[a.reshape(1, width) for a in aux], side)


def _hgrn_epilogue(ys, aux):
    q, fr, i, g = ys
    lb, = aux
    f = lb + (1.0 - lb) * jax.nn.sigmoid(fr)
    return _silu(q), jnp.log(f) * LOG2_E, 1.0 - f, i, _silu(g)


def _attn_epilogue(ys, aux):
    q, k, v = ys
    return q * (HEAD_DIM ** -0.5), k, v


def _gate_epilogue(ys, aux):
    ga, gb = ys
    ba, bb = aux
    return jax.nn.sigmoid(ga + ba), jax.nn.sigmoid(gb + bb)


def _lower_half_total(p_hb, level, row8):
    c, w = p_hb.shape
    hb = 1 << level
    if level >= 2:
        blk = min(2 * hb, c)
        return jnp.concatenate(
            [jnp.broadcast_to(p_hb[b * blk + hb - 1:b * blk + hb, :], (blk, w)) for b in range(c // blk)], axis=0)
    x = p_hb.reshape(c // 8, 8, w)
    odd = (row8 & 1) == 1
    if level == 0:
        g = jnp.where(odd, pltpu.roll(x, 1, 1), x)
    else:
        z = jnp.where(odd, x, pltpu.roll(x, 7, 1))
        g = jnp.where((row8 & 2) == 0, z, pltpu.roll(z, 2, 1))
    return g.reshape(c, w)


def _hgrn_kernel(qs_ref, lf_ref, kk_ref, v_ref, gs_ref, ng_ref, o_ref, st_ref, *, heads_per_block,
                 chunks_per_iter):
    c = CHUNK
    dk = HEAD_DIM
    w = qs_ref.shape[1]
    n_levels = 6

    @pl.when(pl.program_id(2) == 0)
    def _():
        st_ref[...] = jnp.zeros_like(st_ref)

    row = lax.broadcasted_iota(jnp.int32, (c, w), 0)
    row8 = lax.broadcasted_iota(jnp.int32, (c // 8, 8, w), 1)
    upper = [((row >> p) & 1) == 1 for p in range(n_levels)]
    ti = lax.broadcasted_iota(jnp.int32, (c, c), 0)
    si = lax.broadcasted_iota(jnp.int32, (c, c), 1)
    diag_mask = ti == si
    level_masks = [((ti >> (p + 1)) == (si >> (p + 1))) & (((ti >> p) & 1) == 1) & (((si >> p) & 1) == 0)
                   for p in range(n_levels)]

    def iter_body(it, carry):
        pre = []
        for cc in range(chunks_per_iter):
            rows = pl.ds(pl.multiple_of((it * chunks_per_iter + cc) * c, c), c)
            p_hb = lf_ref[rows, :]
            e_q, e_k = [], []
            for p in range(n_levels + 1):
                g = _lower_half_total(p_hb, p, row8)
                e_q.append(jnp.exp2(p_hb))
                e_k.append(None if p == 0 else jnp.exp2(g - p_hb))
                if p < n_levels:
                    p_hb = p_hb + jnp.where(upper[p], g, 0.0)
            pre.append((rows, e_q, e_k))
        indep = {}
        for cc, (rows, e_q, e_k) in enumerate(pre):
            for hh in range(heads_per_block):
                cols = slice(hh * dk, (hh + 1) * dk)
                qb = qs_ref[rows, cols]
                kb = kk_ref[rows, cols]
                v = v_ref[rows, cols]
                q = qb.astype(F32)
                k = kb.astype(F32)
                parts = [_dot_nt(qb, kb)]
                for p in range(n_levels):
                    qt = (q * e_q[p][:, cols]).astype(BF16)
                    kt = kb if p == 0 else (k * e_k[p][:, cols]).astype(BF16)
                    parts.append(_dot_nt(qt, kt))
                e_in = e_q[n_levels][:, cols]
                e_out = e_k[n_levels][:, cols]
                kv = _dot_tn(v, (k * e_out).astype(BF16))
                indep[cc, hh] = (parts, kv, (q * e_in).astype(BF16), v, e_in[c - 1:c, :])
        for hh in range(heads_per_block):
            cols = slice(hh * dk, (hh + 1) * dk)
            st = st_ref[hh]
            for cc, (rows, _, _) in enumerate(pre):
                parts, kv, q_in, v, decay_all = indep[cc, hh]
                scores = jnp.where(diag_mask, parts[0], 0.0)
                for p in range(n_levels):
                    scores = jnp.where(level_masks[p], parts[p + 1], scores)
                o = _dot(scores.astype(BF16), v) + _dot_nt(q_in, st.astype(BF16))
                st = st * decay_all + kv
                inv = lax.rsqrt(jnp.mean(o * o, axis=-1, keepdims=True) + NORM_EPS)
                o = o * inv * ng_ref[:, cols] * gs_ref[rows, cols].astype(F32)
                o_ref[rows, cols] = o.astype(o_ref.dtype)
            st_ref[hh] = st
        return carry

    lax.fori_loop(0, qs_ref.shape[0] // (c * chunks_per_iter), iter_body, 0)


def hgrn(qs, lf, kk, v, gs, norm_g, *, batch, seq, block_len=512, heads_per_block=4, chunks_per_iter=2):
    t, width = qs.shape
    bw = heads_per_block * HEAD_DIM
    n_l = seq // block_len
    tok = pl.BlockSpec((block_len, bw), lambda b, h, l: (b * n_l + l, h))
    pipelined = 4 * _nbytes((block_len, bw), BF16) + _nbytes((block_len, bw), F32)
    return pl.pallas_call(
        functools.partial(_hgrn_kernel, heads_per_block=heads_per_block, chunks_per_iter=chunks_per_iter),
        grid=(batch, width // bw, n_l),
        in_specs=[tok, tok, tok, tok, tok,
                  pl.BlockSpec((1, bw), lambda b, h, l: (0, h))],
        out_specs=tok,
        out_shape=jax.ShapeDtypeStruct((t, width), BF16),
        scratch_shapes=[pltpu.VMEM((heads_per_block, HEAD_DIM, HEAD_DIM), F32)],
        compiler_params=pltpu.CompilerParams(
            dimension_semantics=("parallel", "parallel", "arbitrary"),
            vmem_limit_bytes=_vmem_limit(pipelined, 0)),
        name="hgrn",
    )(qs, lf, kk, v, gs, norm_g.reshape(1, width))


ATTN_Q_BLOCK = CHUNK
ATTN_WINDOW = 640
ATTN_LEAD = ATTN_WINDOW - ATTN_Q_BLOCK
ATTN_BIAS_LANES = 768


def _attn_bias_row(rel_bias):
    m = np.arange(ATTN_BIAS_LANES)
    m = np.where(m >= ATTN_WINDOW, m - ATTN_BIAS_LANES, m)
    idx = np.clip(ATTN_LEAD - m, -MAX_REL, MAX_REL) + MAX_REL
    runs, start = [], 0
    for pos in range(1, len(idx) + 1):
        if pos == len(idx) or idx[pos] != idx[pos - 1] - 1:
            runs.append((start, pos))
            start = pos
    pieces = [jnp.flip(rel_bias[:, int(idx[hi - 1]):int(idx[lo]) + 1], axis=1) for lo, hi in runs]
    return jnp.concatenate(pieces, axis=1)


def _attn_kernel(q_ref, k_ref, v_ref, brow_ref, o_ref, kpad_ref, vpad_ref, *, group):
    seq = q_ref.shape[0]
    kpad_ref[0:ATTN_LEAD, :] = jnp.zeros((ATTN_LEAD, HEAD_DIM), kpad_ref.dtype)
    vpad_ref[0:ATTN_LEAD, :] = jnp.zeros((ATTN_LEAD, HEAD_DIM), vpad_ref.dtype)
    kpad_ref[ATTN_LEAD:, :] = k_ref[...]
    vpad_ref[ATTN_LEAD:, :] = v_ref[...]
    brow = jnp.broadcast_to(brow_ref[0], (ATTN_Q_BLOCK, ATTN_BIAS_LANES))
    toep = pltpu.roll(brow, 0, 1, stride=1, stride_axis=0)[:, :ATTN_WINDOW]
    qi = lax.broadcasted_iota(jnp.int32, (ATTN_Q_BLOCK, ATTN_WINDOW), 0)
    col = lax.broadcasted_iota(jnp.int32, (ATTN_Q_BLOCK, ATTN_WINDOW), 1)
    qc = qi // CHUNK
    kc = col // CHUNK
    lead = ATTN_LEAD // CHUNK
    bias = jnp.where((kc >= qc + lead - LEFT_CHUNKS) & (kc <= qc + lead), toep, MASK_VALUE)

    def do_group(base, masked):
        starts = [base + u * ATTN_Q_BLOCK for u in range(group)]
        scores = []
        for start in starts:
            q = q_ref[pl.ds(start, ATTN_Q_BLOCK), :]
            kw = kpad_ref[pl.ds(start, ATTN_WINDOW), :]
            s = _dot_nt(q, kw) + bias
            if masked:
                s = jnp.where(col >= ATTN_LEAD - start, s, MASK_VALUE)
            scores.append(s)
        probs = []
        for s in scores:
            m = jnp.max(s, axis=-1, keepdims=True)
            p = jnp.exp(s - m)
            probs.append((p.astype(BF16), jnp.sum(p, axis=-1, keepdims=True)))
        for start, (p, denom) in zip(starts, probs):
            vw = vpad_ref[pl.ds(start, ATTN_WINDOW), :]
            o = _dot(p, vw) / denom
            o_ref[pl.ds(start, ATTN_Q_BLOCK), :] = o.astype(o_ref.dtype)

    span = group * ATTN_Q_BLOCK
    n_masked = pl.cdiv(ATTN_LEAD, span)
    for g in range(n_masked):
        do_group(g * span, True)

    def body(g, carry):
        do_group(pl.multiple_of(g * span, span), False)
        return carry

    lax.fori_loop(n_masked, seq // span, body, 0)


def band_attn(q, k, v, bias_row, *, batch, seq, group=8):
    t, width = q.shape
    tok = pl.BlockSpec((seq, HEAD_DIM), lambda b, h: (b, h))
    return pl.pallas_call(
        functools.partial(_attn_kernel, group=group),
        grid=(batch, width // HEAD_DIM),
        in_specs=[tok, tok, tok,
                  pl.BlockSpec((1, 1, ATTN_BIAS_LANES), lambda b, h: (h, 0, 0))],
        out_specs=tok,
        out_shape=jax.ShapeDtypeStruct((t, width), BF16),
        scratch_shapes=[pltpu.VMEM((seq + ATTN_LEAD, HEAD_DIM), BF16),
                        pltpu.VMEM((seq + ATTN_LEAD, HEAD_DIM), BF16)],
        compiler_params=pltpu.CompilerParams(
            dimension_semantics=("parallel", "parallel"),
            vmem_limit_bytes=_vmem_limit(
                4 * _nbytes((seq, HEAD_DIM), BF16),
                2 * _nbytes((seq + ATTN_LEAD, HEAD_DIM), BF16))),
        name="band_attn",
    )(q, k, v, bias_row.reshape(bias_row.shape[0], 1, ATTN_BIAS_LANES))


def _mix_up_kernel(oa_ref, ob_ref, wa_ref, wb_ref, ga_ref, gb_ref, o_ref):
    ya = _dot(oa_ref[...], wa_ref[...])
    yb = _dot(ob_ref[...], wb_ref[...])
    o_ref[...] = (ga_ref[...].astype(F32) * ya + gb_ref[...].astype(F32) * yb).astype(o_ref.dtype)


def mix_up(oa, ob, wa, wb, ga, gb, *, tm=1024, tn=512):
    t, kdim = oa.shape
    d = wa.shape[1]
    pipelined = (2 * _nbytes((tm, kdim), BF16) + 2 * _nbytes((kdim, tn), BF16) + 3 * _nbytes((tm, tn), BF16))
    return pl.pallas_call(
        _mix_up_kernel,
        grid=(t // tm, d // tn),
        in_specs=[pl.BlockSpec((tm, kdim), lambda i, j: (i, 0)),
                  pl.BlockSpec((tm, kdim), lambda i, j: (i, 0)),
                  pl.BlockSpec((kdim, tn), lambda i, j: (0, j)),
                  pl.BlockSpec((kdim, tn), lambda i, j: (0, j)),
                  pl.BlockSpec((tm, tn), lambda i, j: (i, j)),
                  pl.BlockSpec((tm, tn), lambda i, j: (i, j))],
        out_specs=pl.BlockSpec((tm, tn), lambda i, j: (i, j)),
        out_shape=jax.ShapeDtypeStruct((t, d), BF16),
        compiler_params=pltpu.CompilerParams(
            dimension_semantics=("parallel", "arbitrary"),
            vmem_limit_bytes=_vmem_limit(pipelined, 4 * _nbytes((tm, tn), F32))),
        name="mix_up",
    )(oa, ob, wa, wb, ga, gb)


def _ffn(x, h, w1, w3, w2, post_g, next_g, emit_next, side=None):
    g, w2_b = glu_up(h, w1, w3, w2)
    return matmul_resnorm(g, w2_b, x, post_g, next_g, scale=MACARON_WEIGHT, emit_next=emit_next, side=side)


def kernel(x, ffn1_pre_g, ffn1_post_g, ffn1_w1, ffn1_w3, ffn1_w2, mix_pre_g, mix_post_g, w_in, b_gate,
           hgrn_lb_logits, hgrn_norm_g, rel_bias, w_up_a, w_up_b, w_out,
           ffn2_pre_g, ffn2_post_g, ffn2_w1, ffn2_w3, ffn2_w2):
    batch, seq, d = x.shape
    depth = ffn1_w1.shape[0]
    d_half = d // 2
    lower_bounds = jnp.cumsum(jax.nn.softmax(hgrn_lb_logits.astype(F32), axis=0), axis=0)
    xt = x.reshape(batch * seq, d)
    h = norm_cast(xt, ffn1_pre_g[0])
    for layer in range(depth):
        xt, h, w_in_b = _ffn(xt, h, ffn1_w1[layer], ffn1_w3[layer], ffn1_w2[layer],
                             ffn1_post_g[layer], mix_pre_g[layer], True, side=w_in[layer])

        qs, lf, kk, vv, gs, w_up_a_b = proj(h, w_in_b, [0, d_half, 2 * d_half, 3 * d_half], d_half,
                                            [lower_bounds[layer]], _hgrn_epilogue,
                                            [BF16, F32, BF16, BF16, BF16], w_up_a[layer])
        qb, kb, vb, w_up_b_b = proj(h, w_in_b, [4 * d_half, 5 * d_half, 6 * d_half], d_half,
                                    [], _attn_epilogue, [BF16, BF16, BF16], w_up_b[layer])
        ga, gb, w_out_b = proj(h, w_in_b, [7 * d_half, 7 * d_half + d], d,
                               [b_gate[layer, 0], b_gate[layer, 1]], _gate_epilogue, [BF16, BF16],
                               w_out[layer], tn=512)

        oa = hgrn(qs, lf, kk, vv, gs, hgrn_norm_g[layer], batch=batch, seq=seq)
        ob = band_attn(qb, kb, vb, _attn_bias_row(rel_bias[layer].astype(F32)), batch=batch, seq=seq)

        m = mix_up(oa, ob, w_up_a_b, w_up_b_b, ga, gb)
        last = layer == depth - 1
        xt, h, _ = matmul_resnorm(m, w_out_b, xt, mix_post_g[layer], ffn2_pre_g[layer], scale=1.0, emit_next=True,
                                  tn=512)
        next_pre = ffn1_pre_g[layer + 1] if not last else ffn2_pre_g[layer]
        xt, h, _ = _ffn(xt, h, ffn2_w1[layer], ffn2_w3[layer], ffn2_w2[layer],
                        ffn2_post_g[layer], next_pre, not last)
    return xt.reshape(batch, seq, d)
```

```python
import functools

import numpy as np
import jax
import jax.numpy as jnp
from jax import lax
from jax.experimental import pallas as pl
from jax.experimental.pallas import tpu as pltpu

F32 = jnp.float32
BF16 = jnp.bfloat16

NORM_EPS = 1e-6
LOG2_E = 1.4426950408889634
MACARON_WEIGHT = 0.5
CHUNK = 64
HEAD_DIM = 128
BF16_SUBLANES = 16
LEFT_CHUNKS = 8
MAX_REL = 256
MASK_VALUE = -1e30

V7X_VMEM_BYTES = 64 * 1024 * 1024
VMEM_LIMIT_CAP = V7X_VMEM_BYTES - 6 * 1024 * 1024


def _vmem_limit(pipelined_bytes, resident_bytes=0):
    est = 2 * pipelined_bytes + resident_bytes + 8 * 1024 * 1024
    return int(min(max(est, 32 * 1024 * 1024), VMEM_LIMIT_CAP))


def _nbytes(shape, dtype):
    return int(np.prod(shape)) * jnp.dtype(dtype).itemsize


def _dot(a, b):
    return jnp.dot(a, b, preferred_element_type=F32)


def _dot_nt(a, b):
    return lax.dot_general(a, b, (((1,), (1,)), ((), ())), preferred_element_type=F32)


def _dot_tn(a, b):
    return lax.dot_general(a, b, (((0,), (0,)), ((), ())), preferred_element_type=F32)


def _silu(x):
    return x * jax.nn.sigmoid(x)


def _side_cast_specs(side, n_i, n_j):
    rows, cols = side.shape
    steps = n_i * n_j
    slab = max(rows // steps, BF16_SUBLANES)
    n_slabs, rem = divmod(rows, slab)
    steps_per_slab, rem2 = divmod(steps, n_slabs)
    assert rem == 0 and rem2 == 0 and slab % BF16_SUBLANES == 0, (rows, n_i, n_j)
    spec = pl.BlockSpec((slab, cols), lambda i, j: ((i * n_j + j) // steps_per_slab, 0))
    return spec, jax.ShapeDtypeStruct((rows, cols), BF16), _nbytes((slab, cols), F32) + _nbytes((slab, cols), BF16)


def _norm_cast_kernel(x_ref, g_ref, o_ref):
    x = x_ref[...]
    inv = lax.rsqrt(jnp.mean(x * x, axis=-1, keepdims=True) + NORM_EPS)
    o_ref[...] = (x * inv * g_ref[...]).astype(o_ref.dtype)


def norm_cast(x, g, *, tm=256):
    t, d = x.shape
    return pl.pallas_call(
        _norm_cast_kernel,
        grid=(t // tm,),
        in_specs=[pl.BlockSpec((tm, d), lambda i: (i, 0)),
                  pl.BlockSpec((1, d), lambda i: (0, 0))],
        out_specs=pl.BlockSpec((tm, d), lambda i: (i, 0)),
        out_shape=jax.ShapeDtypeStruct((t, d), BF16),
        compiler_params=pltpu.CompilerParams(
            dimension_semantics=("parallel",),
            vmem_limit_bytes=_vmem_limit(_nbytes((tm, d), F32) + _nbytes((tm, d), BF16),
                                         3 * _nbytes((tm, d), F32))),
        name="norm_cast",
    )(x, g.reshape(1, d))


def _glu_up_kernel(h_ref, w1_ref, w3_ref, side_ref, o_ref, side_o_ref):
    h = h_ref[...]
    a = _dot(h, w1_ref[...].astype(BF16))
    b = _dot(h, w3_ref[...].astype(BF16))
    o_ref[...] = (_silu(a) * b).astype(o_ref.dtype)
    side_o_ref[...] = side_ref[...].astype(side_o_ref.dtype)


def glu_up(h, w1, w3, side, *, tm=2048, tn=256):
    t, d = h.shape
    f = w1.shape[1]
    grid = (t // tm, f // tn)
    side_spec, side_shape, side_bytes = _side_cast_specs(side, *grid)
    return pl.pallas_call(
        _glu_up_kernel,
        grid=grid,
        in_specs=[pl.BlockSpec((tm, d), lambda i, j: (i, 0), pipeline_mode=pl.Buffered(1)),
                  pl.BlockSpec((d, tn), lambda i, j: (0, j)),
                  pl.BlockSpec((d, tn), lambda i, j: (0, j)),
                  side_spec],
        out_specs=[pl.BlockSpec((tm, tn), lambda i, j: (i, j)), side_spec],
        out_shape=[jax.ShapeDtypeStruct((t, f), BF16), side_shape],
        compiler_params=pltpu.CompilerParams(
            dimension_semantics=("parallel", "arbitrary"),
            vmem_limit_bytes=_vmem_limit(
                2 * _nbytes((d, tn), w1.dtype) + _nbytes((tm, tn), BF16) + side_bytes,
                _nbytes((tm, d), BF16) + 4 * _nbytes((tm, tn), F32) + 2 * _nbytes((d, tn), BF16))),
        name="glu_up",
    )(h, w1, w3, side)


def _matmul_kernel(a_ref, w_ref, *rest):
    maybe_side_ref, o_ref, maybe_side_o_ref = rest if len(rest) == 3 else (None, rest[0], None)
    o_ref[...] = _dot(a_ref[...], w_ref[...]).astype(o_ref.dtype)
    if maybe_side_ref is not None:
        maybe_side_o_ref[...] = maybe_side_ref[...].astype(maybe_side_o_ref.dtype)


def matmul_kres(a, w, side=None, *, tm=512, tn=1024):
    t, kdim = a.shape
    n = w.shape[1]
    grid = (n // tn, t // tm)
    in_specs = [pl.BlockSpec((tm, kdim), lambda j, i: (i, 0)),
                pl.BlockSpec((kdim, tn), lambda j, i: (0, j), pipeline_mode=pl.Buffered(1))]
    out_specs = [pl.BlockSpec((tm, tn), lambda j, i: (i, j))]
    out_shape = [jax.ShapeDtypeStruct((t, n), BF16)]
    operands = [a, w]
    side_bytes = 0
    if side is not None:
        side_spec, side_shape, side_bytes = _side_cast_specs(side, *grid)
        in_specs.append(side_spec)
        out_specs.append(side_spec)
        out_shape.append(side_shape)
        operands.append(side)
    outs = pl.pallas_call(
        _matmul_kernel,
        grid=grid,
        in_specs=in_specs,
        out_specs=out_specs,
        out_shape=out_shape,
        compiler_params=pltpu.CompilerParams(
            dimension_semantics=("parallel", "arbitrary"),
            vmem_limit_bytes=_vmem_limit(
                _nbytes((tm, kdim), BF16) + _nbytes((tm, tn), BF16) + side_bytes,
                _nbytes((kdim, tn), BF16) + 2 * _nbytes((tm, tn), F32))),
        name="matmul_kres",
    )(*operands)
    return outs if side is not None else outs[0]


def _resnorm_kernel(y_ref, x_ref, gpost_ref, gnext_ref, xo_ref, *maybe_h_ref, scale):
    y = y_ref[...].astype(F32)
    inv = lax.rsqrt(jnp.mean(y * y, axis=-1, keepdims=True) + NORM_EPS)
    xn = x_ref[...] + scale * (y * inv * gpost_ref[...])
    xo_ref[...] = xn
    if maybe_h_ref:
        inv2 = lax.rsqrt(jnp.mean(xn * xn, axis=-1, keepdims=True) + NORM_EPS)
        maybe_h_ref[0][...] = (xn * inv2 * gnext_ref[...]).astype(BF16)


def resnorm(y, x, g_post, g_next, *, scale, emit_next, tm=256):
    t, d = x.shape
    row = pl.BlockSpec((tm, d), lambda i: (i, 0))
    vec = pl.BlockSpec((1, d), lambda i: (0, 0))
    out_shape = [jax.ShapeDtypeStruct((t, d), F32)]
    if emit_next:
        out_shape.append(jax.ShapeDtypeStruct((t, d), BF16))
    pipelined = (_nbytes((tm, d), y.dtype) + 2 * _nbytes((tm, d), F32)
                 + (_nbytes((tm, d), BF16) if emit_next else 0))
    outs = pl.pallas_call(
        functools.partial(_resnorm_kernel, scale=scale),
        grid=(t // tm,),
        in_specs=[row, row, vec, vec],
        out_specs=[row] * len(out_shape),
        out_shape=out_shape,
        compiler_params=pltpu.CompilerParams(
            dimension_semantics=("parallel",),
            vmem_limit_bytes=_vmem_limit(pipelined, 3 * _nbytes((tm, d), F32))),
        name="resnorm",
    )(y, x, g_post.reshape(1, d), g_next.reshape(1, d))
    return outs if emit_next else (outs[0], None)


def _proj_kernel(*refs, n_seg, n_aux, epilogue):
    h_ref = refs[0]
    w_refs = refs[1:1 + n_seg]
    aux_refs = refs[1 + n_seg:1 + n_seg + n_aux]
    side_ref = refs[1 + n_seg + n_aux]
    out_refs = refs[2 + n_seg + n_aux:-1]
    side_o_ref = refs[-1]
    h = h_ref[...]
    ys = [_dot(h, w_ref[...]) for w_ref in w_refs]
    outs = epilogue(ys, [r[...] for r in aux_refs])
    for o_ref, o in zip(out_refs, outs):
        o_ref[...] = o.astype(o_ref.dtype)
    side_o_ref[...] = side_ref[...].astype(side_o_ref.dtype)


def proj(h, w_in, col_starts, width, aux, epilogue, out_dtypes, side, *, tm=1024, tn=256):
    t, d = h.shape
    n_seg = len(col_starts)
    grid = (t // tm, width // tn)
    side_spec, side_shape, side_bytes = _side_cast_specs(side, *grid)

    def w_spec(start):
        off = start // tn
        return pl.BlockSpec((d, tn), lambda i, j: (0, off + j))

    pipelined = (_nbytes((tm, d), BF16) + n_seg * _nbytes((d, tn), BF16)
                 + sum(_nbytes((tm, tn), dt) for dt in out_dtypes) + side_bytes)
    return pl.pallas_call(
        functools.partial(_proj_kernel, n_seg=n_seg, n_aux=len(aux), epilogue=epilogue),
        grid=grid,
        in_specs=([pl.BlockSpec((tm, d), lambda i, j: (i, 0))]
                  + [w_spec(s) for s in col_starts]
                  + [pl.BlockSpec((1, tn), lambda i, j: (0, j)) for _ in aux]
                  + [side_spec]),
        out_specs=[pl.BlockSpec((tm, tn), lambda i, j: (i, j)) for _ in out_dtypes] + [side_spec],
        out_shape=[jax.ShapeDtypeStruct((t, width), dt) for dt in out_dtypes] + [side_shape],
        compiler_params=pltpu.CompilerParams(
            dimension_semantics=("parallel", "arbitrary"),
            vmem_limit_bytes=_vmem_limit(pipelined, (n_seg + 4) * _nbytes((tm, tn), F32))),
        name="proj",
    )(h, *([w_in] * n_seg), *[a.reshape(1, width) for a in aux], side)


def _hgrn_epilogue(ys, aux):
    q, fr, i, g = ys
    lb, = aux
    f = lb + (1.0 - lb) * jax.nn.sigmoid(fr)
    return _silu(q), jnp.log(f) * LOG2_E, 1.0 - f, i, _silu(g)


def _attn_epilogue(ys, aux):
    q, k, v = ys
    return q * (HEAD_DIM ** -0.5), k, v


def _gate_epilogue(ys, aux):
    ga, gb = ys
    ba, bb = aux
    return jax.nn.sigmoid(ga + ba), jax.nn.sigmoid(gb + bb)


def _lower_half_total(p_hb, level, row8):
    c, w = p_hb.shape
    hb = 1 << level
    if level >= 2:
        blk = min(2 * hb, c)
        return jnp.concatenate(
            [jnp.broadcast_to(p_hb[b * blk + hb - 1:b * blk + hb, :], (blk, w)) for b in range(c // blk)], axis=0)
    x = p_hb.reshape(c // 8, 8, w)
    odd = (row8 & 1) == 1
    if level == 0:
        g = jnp.where(odd, pltpu.roll(x, 1, 1), x)
    else:
        z = jnp.where(odd, x, pltpu.roll(x, 7, 1))
        g = jnp.where((row8 & 2) == 0, z, pltpu.roll(z, 2, 1))
    return g.reshape(c, w)


def _hgrn_kernel(qs_ref, lf_ref, kk_ref, v_ref, gs_ref, ng_ref, o_ref, st_ref, *, heads_per_block,
                 chunks_per_iter):
    c = CHUNK
    dk = HEAD_DIM
    w = qs_ref.shape[1]
    n_levels = 6

    @pl.when(pl.program_id(2) == 0)
    def _():
        st_ref[...] = jnp.zeros_like(st_ref)

    row = lax.broadcasted_iota(jnp.int32, (c, w), 0)
    row8 = lax.broadcasted_iota(jnp.int32, (c // 8, 8, w), 1)
    upper = [((row >> p) & 1) == 1 for p in range(n_levels)]
    ti = lax.broadcasted_iota(jnp.int32, (c, c), 0)
    si = lax.broadcasted_iota(jnp.int32, (c, c), 1)
    diag_mask = ti == si
    level_masks = [((ti >> (p + 1)) == (si >> (p + 1))) & (((ti >> p) & 1) == 1) & (((si >> p) & 1) == 0)
                   for p in range(n_levels)]

    def iter_body(it, carry):
        pre = []
        for cc in range(chunks_per_iter):
            rows = pl.ds(pl.multiple_of((it * chunks_per_iter + cc) * c, c), c)
            p_hb = lf_ref[rows, :]
            e_q, e_k = [], []
            for p in range(n_levels + 1):
                g = _lower_half_total(p_hb, p, row8)
                e_q.append(jnp.exp2(p_hb))
                e_k.append(None if p == 0 else jnp.exp2(g - p_hb))
                if p < n_levels:
                    p_hb = p_hb + jnp.where(upper[p], g, 0.0)
            pre.append((rows, e_q, e_k))
        indep = {}
        for cc, (rows, e_q, e_k) in enumerate(pre):
            for hh in range(heads_per_block):
                cols = slice(hh * dk, (hh + 1) * dk)
                qb = qs_ref[rows, cols]
                kb = kk_ref[rows, cols]
                v = v_ref[rows, cols]
                q = qb.astype(F32)
                k = kb.astype(F32)
                parts = [_dot_nt(qb, kb)]
                for p in range(n_levels):
                    qt = (q * e_q[p][:, cols]).astype(BF16)
                    kt = kb if p == 0 else (k * e_k[p][:, cols]).astype(BF16)
                    parts.append(_dot_nt(qt, kt))
                e_in = e_q[n_levels][:, cols]
                e_out = e_k[n_levels][:, cols]
                kv = _dot_tn(v, (k * e_out).astype(BF16))
                indep[cc, hh] = (parts, kv, (q * e_in).astype(BF16), v, e_in[c - 1:c, :])
        for hh in range(heads_per_block):
            cols = slice(hh * dk, (hh + 1) * dk)
            st = st_ref[hh]
            for cc, (rows, _, _) in enumerate(pre):
                parts, kv, q_in, v, decay_all = indep[cc, hh]
                scores = jnp.where(diag_mask, parts[0], 0.0)
                for p in range(n_levels):
                    scores = jnp.where(level_masks[p], parts[p + 1], scores)
                o = _dot(scores.astype(BF16), v) + _dot_nt(q_in, st.astype(BF16))
                st = st * decay_all + kv
                inv = lax.rsqrt(jnp.mean(o * o, axis=-1, keepdims=True) + NORM_EPS)
                o = o * inv * ng_ref[:, cols] * gs_ref[rows, cols].astype(F32)
                o_ref[rows, cols] = o.astype(o_ref.dtype)
            st_ref[hh] = st
        return carry

    lax.fori_loop(0, qs_ref.shape[0] // (c * chunks_per_iter), iter_body, 0)


def hgrn(qs, lf, kk, v, gs, norm_g, *, batch, seq, block_len=512, heads_per_block=4, chunks_per_iter=2):
    t, width = qs.shape
    bw = heads_per_block * HEAD_DIM
    n_l = seq // block_len
    tok = pl.BlockSpec((block_len, bw), lambda b, h, l: (b * n_l + l, h))
    pipelined = 4 * _nbytes((block_len, bw), BF16) + _nbytes((block_len, bw), F32)
    return pl.pallas_call(
        functools.partial(_hgrn_kernel, heads_per_block=heads_per_block, chunks_per_iter=chunks_per_iter),
        grid=(batch, width // bw, n_l),
        in_specs=[tok, tok, tok, tok, tok,
                  pl.BlockSpec((1, bw), lambda b, h, l: (0, h))],
        out_specs=tok,
        out_shape=jax.ShapeDtypeStruct((t, width), BF16),
        scratch_shapes=[pltpu.VMEM((heads_per_block, HEAD_DIM, HEAD_DIM), F32)],
        compiler_params=pltpu.CompilerParams(
            dimension_semantics=("parallel", "parallel", "arbitrary"),
            vmem_limit_bytes=_vmem_limit(pipelined, 0)),
        name="hgrn",
    )(qs, lf, kk, v, gs, norm_g.reshape(1, width))


ATTN_Q_BLOCK = CHUNK
ATTN_WINDOW = 640
ATTN_LEAD = ATTN_WINDOW - ATTN_Q_BLOCK
ATTN_BIAS_LANES = 768


def _attn_bias_row(rel_bias):
    m = np.arange(ATTN_BIAS_LANES)
    m = np.where(m >= ATTN_WINDOW, m - ATTN_BIAS_LANES, m)
    idx = np.clip(ATTN_LEAD - m, -MAX_REL, MAX_REL) + MAX_REL
    runs, start = [], 0
    for pos in range(1, len(idx) + 1):
        if pos == len(idx) or idx[pos] != idx[pos - 1] - 1:
            runs.append((start, pos))
            start = pos
    pieces = [jnp.flip(rel_bias[:, int(idx[hi - 1]):int(idx[lo]) + 1], axis=1) for lo, hi in runs]
    return jnp.concatenate(pieces, axis=1)


def _attn_kernel(q_ref, k_ref, v_ref, brow_ref, o_ref, kpad_ref, vpad_ref, *, group):
    seq = q_ref.shape[0]
    kpad_ref[0:ATTN_LEAD, :] = jnp.zeros((ATTN_LEAD, HEAD_DIM), kpad_ref.dtype)
    vpad_ref[0:ATTN_LEAD, :] = jnp.zeros((ATTN_LEAD, HEAD_DIM), vpad_ref.dtype)
    kpad_ref[ATTN_LEAD:, :] = k_ref[...]
    vpad_ref[ATTN_LEAD:, :] = v_ref[...]
    brow = jnp.broadcast_to(brow_ref[0], (ATTN_Q_BLOCK, ATTN_BIAS_LANES))
    toep = pltpu.roll(brow, 0, 1, stride=1, stride_axis=0)[:, :ATTN_WINDOW]
    qi = lax.broadcasted_iota(jnp.int32, (ATTN_Q_BLOCK, ATTN_WINDOW), 0)
    col = lax.broadcasted_iota(jnp.int32, (ATTN_Q_BLOCK, ATTN_WINDOW), 1)
    qc = qi // CHUNK
    kc = col // CHUNK
    lead = ATTN_LEAD // CHUNK
    bias = jnp.where((kc >= qc + lead - LEFT_CHUNKS) & (kc <= qc + lead), toep, MASK_VALUE)

    def do_group(base, masked):
        starts = [base + u * ATTN_Q_BLOCK for u in range(group)]
        scores = []
        for start in starts:
            q = q_ref[pl.ds(start, ATTN_Q_BLOCK), :]
            kw = kpad_ref[pl.ds(start, ATTN_WINDOW), :]
            s = _dot_nt(q, kw) + bias
            if masked:
                s = jnp.where(col >= ATTN_LEAD - start, s, MASK_VALUE)
            scores.append(s)
        probs = []
        for s in scores:
            m = jnp.max(s, axis=-1, keepdims=True)
            p = jnp.exp(s - m)
            probs.append((p.astype(BF16), jnp.sum(p, axis=-1, keepdims=True)))
        for start, (p, denom) in zip(starts, probs):
            vw = vpad_ref[pl.ds(start, ATTN_WINDOW), :]
            o = _dot(p, vw) / denom
            o_ref[pl.ds(start, ATTN_Q_BLOCK), :] = o.astype(o_ref.dtype)

    span = group * ATTN_Q_BLOCK
    n_masked = pl.cdiv(ATTN_LEAD, span)
    for g in range(n_masked):
        do_group(g * span, True)

    def body(g, carry):
        do_group(pl.multiple_of(g * span, span), False)
        return carry

    lax.fori_loop(n_masked, seq // span, body, 0)


def band_attn(q, k, v, bias_row, *, batch, seq, group=8):
    t, width = q.shape
    tok = pl.BlockSpec((seq, HEAD_DIM), lambda b, h: (b, h))
    return pl.pallas_call(
        functools.partial(_attn_kernel, group=group),
        grid=(batch, width // HEAD_DIM),
        in_specs=[tok, tok, tok,
                  pl.BlockSpec((1, 1, ATTN_BIAS_LANES), lambda b, h: (h, 0, 0))],
        out_specs=tok,
        out_shape=jax.ShapeDtypeStruct((t, width), BF16),
        scratch_shapes=[pltpu.VMEM((seq + ATTN_LEAD, HEAD_DIM), BF16),
                        pltpu.VMEM((seq + ATTN_LEAD, HEAD_DIM), BF16)],
        compiler_params=pltpu.CompilerParams(
            dimension_semantics=("parallel", "parallel"),
            vmem_limit_bytes=_vmem_limit(
                4 * _nbytes((seq, HEAD_DIM), BF16),
                2 * _nbytes((seq + ATTN_LEAD, HEAD_DIM), BF16))),
        name="band_attn",
    )(q, k, v, bias_row.reshape(bias_row.shape[0], 1, ATTN_BIAS_LANES))


def _mix_up_kernel(oa_ref, ob_ref, wa_ref, wb_ref, ga_ref, gb_ref, o_ref):
    ya = _dot(oa_ref[...], wa_ref[...])
    yb = _dot(ob_ref[...], wb_ref[...])
    o_ref[...] = (ga_ref[...].astype(F32) * ya + gb_ref[...].astype(F32) * yb).astype(o_ref.dtype)


def mix_up(oa, ob, wa, wb, ga, gb, *, tm=1024, tn=512):
    t, kdim = oa.shape
    d = wa.shape[1]
    pipelined = (2 * _nbytes((tm, kdim), BF16) + 2 * _nbytes((kdim, tn), BF16) + 3 * _nbytes((tm, tn), BF16))
    return pl.pallas_call(
        _mix_up_kernel,
        grid=(t // tm, d // tn),
        in_specs=[pl.BlockSpec((tm, kdim), lambda i, j: (i, 0)),
                  pl.BlockSpec((tm, kdim), lambda i, j: (i, 0)),
                  pl.BlockSpec((kdim, tn), lambda i, j: (0, j)),
                  pl.BlockSpec((kdim, tn), lambda i, j: (0, j)),
                  pl.BlockSpec((tm, tn), lambda i, j: (i, j)),
                  pl.BlockSpec((tm, tn), lambda i, j: (i, j))],
        out_specs=pl.BlockSpec((tm, tn), lambda i, j: (i, j)),
        out_shape=jax.ShapeDtypeStruct((t, d), BF16),
        compiler_params=pltpu.CompilerParams(
            dimension_semantics=("parallel", "arbitrary"),
            vmem_limit_bytes=_vmem_limit(pipelined, 4 * _nbytes((tm, tn), F32))),
        name="mix_up",
    )(oa, ob, wa, wb, ga, gb)


def _ffn(x, h, w1, w3, w2, post_g, next_g, emit_next, side=None):
    g, w2_b = glu_up(h, w1, w3, w2)
    if side is None:
        y, side_b = matmul_kres(g, w2_b), None
    else:
        y, side_b = matmul_kres(g, w2_b, side)
    x_new, h_next = resnorm(y, x, post_g, next_g, scale=MACARON_WEIGHT, emit_next=emit_next)
    return x_new, h_next, side_b


def kernel(x, ffn1_pre_g, ffn1_post_g, ffn1_w1, ffn1_w3, ffn1_w2, mix_pre_g, mix_post_g, w_in, b_gate,
           hgrn_lb_logits, hgrn_norm_g, rel_bias, w_up_a, w_up_b, w_out,
           ffn2_pre_g, ffn2_post_g, ffn2_w1, ffn2_w3, ffn2_w2):
    batch, seq, d = x.shape
    depth = ffn1_w1.shape[0]
    d_half = d // 2
    lower_bounds = jnp.cumsum(jax.nn.softmax(hgrn_lb_logits.astype(F32), axis=0), axis=0)
    xt = x.reshape(batch * seq, d)
    h = norm_cast(xt, ffn1_pre_g[0])
    for layer in range(depth):
        xt, h, w_in_b = _ffn(xt, h, ffn1_w1[layer], ffn1_w3[layer], ffn1_w2[layer],
                             ffn1_post_g[layer], mix_pre_g[layer], True, side=w_in[layer])

        qs, lf, kk, vv, gs, w_up_a_b = proj(h, w_in_b, [0, d_half, 2 * d_half, 3 * d_half], d_half,
                                            [lower_bounds[layer]], _hgrn_epilogue,
                                            [BF16, F32, BF16, BF16, BF16], w_up_a[layer])
        qb, kb, vb, w_up_b_b = proj(h, w_in_b, [4 * d_half, 5 * d_half, 6 * d_half], d_half,
                                    [], _attn_epilogue, [BF16, BF16, BF16], w_up_b[layer])
        ga, gb, w_out_b = proj(h, w_in_b, [7 * d_half, 7 * d_half + d], d,
                               [b_gate[layer, 0], b_gate[layer, 1]], _gate_epilogue, [BF16, BF16],
                               w_out[layer], tn=512)

        oa = hgrn(qs, lf, kk, vv, gs, hgrn_norm_g[layer], batch=batch, seq=seq)
        ob = band_attn(qb, kb, vb, _attn_bias_row(rel_bias[layer].astype(F32)), batch=batch, seq=seq)

        m = mix_up(oa, ob, w_up_a_b, w_up_b_b, ga, gb)
        last = layer == depth - 1
        y = matmul_kres(m, w_out_b, tm=1024)
        xt, h = resnorm(y, xt, mix_post_g[layer], ffn2_pre_g[layer], scale=1.0, emit_next=True)
        next_pre = ffn1_pre_g[layer + 1] if not last else ffn2_pre_g[layer]
        xt, h, _ = _ffn(xt, h, ffn2_w1[layer], ffn2_w3[layer], ffn2_w2[layer],
                        ffn2_post_g[layer], next_pre, not last)
    return xt.reshape(batch, seq, d)
```

```python
import functools

import numpy as np
import jax
import jax.numpy as jnp
from jax import lax
from jax.experimental import pallas as pl
from jax.experimental.pallas import tpu as pltpu

F32 = jnp.float32
BF16 = jnp.bfloat16

NORM_EPS = 1e-6
LOG2_E = 1.4426950408889634
MACARON_WEIGHT = 0.5
CHUNK = 64
HEAD_DIM = 128
LANES = 128
F32_SUBLANES = 8
BF16_SUBLANES = 16
LEFT_CHUNKS = 8
MAX_REL = 256
MASK_VALUE = -1e30

MIB = 1024 * 1024
V7X_VMEM_BYTES = 64 * MIB
VMEM_LIMIT_CAP = V7X_VMEM_BYTES - 6 * MIB
VMEM_LIMIT_FLOOR = 32 * MIB
VMEM_SPILL_BYTES = 8 * MIB


def _vmem_limit(pipelined_bytes, resident_bytes=0):
    est = 2 * pipelined_bytes + resident_bytes + VMEM_SPILL_BYTES
    return int(min(max(est, VMEM_LIMIT_FLOOR), VMEM_LIMIT_CAP))


def _nbytes(shape, dtype):
    return int(np.prod(shape)) * jnp.dtype(dtype).itemsize


def _dot(a, b):
    return jnp.dot(a, b, preferred_element_type=F32)


def _dot_nt(a, b):
    return lax.dot_general(a, b, (((1,), (1,)), ((), ())), preferred_element_type=F32)


def _dot_tn(a, b):
    return lax.dot_general(a, b, (((0,), (0,)), ((), ())), preferred_element_type=F32)


def _silu(x):
    return x * jax.nn.sigmoid(x)


def _side_cast_specs(side, n_i, n_j):
    rows, cols = side.shape
    slab, rem = divmod(rows, n_i * n_j)
    assert rem == 0 and slab % BF16_SUBLANES == 0, (rows, n_i, n_j)
    spec = pl.BlockSpec((slab, cols), lambda i, j: (i * n_j + j, 0))
    return spec, jax.ShapeDtypeStruct((rows, cols), BF16), _nbytes((slab, cols), F32) + _nbytes((slab, cols), BF16)


def _norm_cast_kernel(x_ref, g_ref, o_ref):
    x = x_ref[...]
    inv = lax.rsqrt(jnp.mean(x * x, axis=-1, keepdims=True) + NORM_EPS)
    o_ref[...] = (x * inv * g_ref[...]).astype(o_ref.dtype)


def norm_cast(x, g, *, tm=256):
    t, d = x.shape
    return pl.pallas_call(
        _norm_cast_kernel,
        grid=(t // tm,),
        in_specs=[pl.BlockSpec((tm, d), lambda i: (i, 0)),
                  pl.BlockSpec((1, d), lambda i: (0, 0))],
        out_specs=pl.BlockSpec((tm, d), lambda i: (i, 0)),
        out_shape=jax.ShapeDtypeStruct((t, d), BF16),
        compiler_params=pltpu.CompilerParams(
            dimension_semantics=("parallel",),
            vmem_limit_bytes=_vmem_limit(_nbytes((tm, d), F32) + _nbytes((tm, d), BF16),
                                         3 * _nbytes((tm, d), F32))),
        name="norm_cast",
    )(x, g.reshape(1, d))


def _glu_up_kernel(h_ref, w1_ref, w3_ref, side_ref, o_ref, side_o_ref):
    h = h_ref[...]
    a = _dot(h, w1_ref[...].astype(BF16))
    b = _dot(h, w3_ref[...].astype(BF16))
    o_ref[...] = (_silu(a) * b).astype(o_ref.dtype)
    side_o_ref[...] = side_ref[...].astype(side_o_ref.dtype)


def glu_up(h, w1, w3, side, *, tm=2048, tn=256):
    t, d = h.shape
    f = w1.shape[1]
    grid = (t // tm, f // tn)
    side_spec, side_shape, side_bytes = _side_cast_specs(side, *grid)
    return pl.pallas_call(
        _glu_up_kernel,
        grid=grid,
        in_specs=[pl.BlockSpec((tm, d), lambda i, j: (i, 0), pipeline_mode=pl.Buffered(1)),
                  pl.BlockSpec((d, tn), lambda i, j: (0, j)),
                  pl.BlockSpec((d, tn), lambda i, j: (0, j)),
                  side_spec],
        out_specs=[pl.BlockSpec((tm, tn), lambda i, j: (i, j)), side_spec],
        out_shape=[jax.ShapeDtypeStruct((t, f), BF16), side_shape],
        compiler_params=pltpu.CompilerParams(
            dimension_semantics=("parallel", "arbitrary"),
            vmem_limit_bytes=_vmem_limit(
                2 * _nbytes((d, tn), w1.dtype) + _nbytes((tm, tn), BF16) + side_bytes,
                _nbytes((tm, d), BF16) + 4 * _nbytes((tm, tn), F32) + 2 * _nbytes((d, tn), BF16))),
        name="glu_up",
    )(h, w1, w3, side)


def _matmul_kernel(a_ref, w_ref, *rest):
    maybe_side_ref, o_ref, maybe_side_o_ref = rest if len(rest) == 3 else (None, rest[0], None)
    o_ref[...] = _dot(a_ref[...], w_ref[...]).astype(o_ref.dtype)
    if maybe_side_ref is not None:
        maybe_side_o_ref[...] = maybe_side_ref[...].astype(maybe_side_o_ref.dtype)


def matmul_kres(a, w, side=None, *, tm=512, tn=1024):
    t, kdim = a.shape
    n = w.shape[1]
    grid = (n // tn, t // tm)
    in_specs = [pl.BlockSpec((tm, kdim), lambda j, i: (i, 0)),
                pl.BlockSpec((kdim, tn), lambda j, i: (0, j), pipeline_mode=pl.Buffered(1))]
    out_specs = [pl.BlockSpec((tm, tn), lambda j, i: (i, j))]
    out_shape = [jax.ShapeDtypeStruct((t, n), BF16)]
    operands = [a, w]
    side_bytes = 0
    if side is not None:
        side_spec, side_shape, side_bytes = _side_cast_specs(side, *grid)
        in_specs.append(side_spec)
        out_specs.append(side_spec)
        out_shape.append(side_shape)
        operands.append(side)
    outs = pl.pallas_call(
        _matmul_kernel,
        grid=grid,
        in_specs=in_specs,
        out_specs=out_specs,
        out_shape=out_shape,
        compiler_params=pltpu.CompilerParams(
            dimension_semantics=("parallel", "arbitrary"),
            vmem_limit_bytes=_vmem_limit(
                _nbytes((tm, kdim), BF16) + _nbytes((tm, tn), BF16) + side_bytes,
                _nbytes((kdim, tn), BF16) + 2 * _nbytes((tm, tn), F32))),
        name="matmul_kres",
    )(*operands)
    return outs if side is not None else outs[0]


def _resnorm_kernel(y_ref, x_ref, gpost_ref, gnext_ref, xo_ref, *maybe_h_ref, scale):
    y = y_ref[...].astype(F32)
    inv = lax.rsqrt(jnp.mean(y * y, axis=-1, keepdims=True) + NORM_EPS)
    xn = x_ref[...] + scale * (y * inv * gpost_ref[...])
    xo_ref[...] = xn
    if maybe_h_ref:
        inv2 = lax.rsqrt(jnp.mean(xn * xn, axis=-1, keepdims=True) + NORM_EPS)
        maybe_h_ref[0][...] = (xn * inv2 * gnext_ref[...]).astype(BF16)


def resnorm(y, x, g_post, g_next, *, scale, emit_next, tm=256):
    t, d = x.shape
    row = pl.BlockSpec((tm, d), lambda i: (i, 0))
    vec = pl.BlockSpec((1, d), lambda i: (0, 0))
    out_shape = [jax.ShapeDtypeStruct((t, d), F32)]
    if emit_next:
        out_shape.append(jax.ShapeDtypeStruct((t, d), BF16))
    pipelined = (_nbytes((tm, d), y.dtype) + 2 * _nbytes((tm, d), F32)
                 + (_nbytes((tm, d), BF16) if emit_next else 0))
    outs = pl.pallas_call(
        functools.partial(_resnorm_kernel, scale=scale),
        grid=(t // tm,),
        in_specs=[row, row, vec, vec],
        out_specs=[row] * len(out_shape),
        out_shape=out_shape,
        compiler_params=pltpu.CompilerParams(
            dimension_semantics=("parallel",),
            vmem_limit_bytes=_vmem_limit(pipelined, 3 * _nbytes((tm, d), F32))),
        name="resnorm",
    )(y, x, g_post.reshape(1, d), g_next.reshape(1, d))
    return outs if emit_next else (outs[0], None)


def _proj_kernel(*refs, n_seg, n_aux, epilogue):
    h_ref = refs[0]
    w_refs = refs[1:1 + n_seg]
    aux_refs = refs[1 + n_seg:1 + n_seg + n_aux]
    side_ref = refs[1 + n_seg + n_aux]
    out_refs = refs[2 + n_seg + n_aux:-1]
    side_o_ref = refs[-1]
    h = h_ref[...]
    ys = [_dot(h, w_ref[...]) for w_ref in w_refs]
    outs = epilogue(ys, [r[...] for r in aux_refs])
    for o_ref, o in zip(out_refs, outs):
        o_ref[...] = o.astype(o_ref.dtype)
    side_o_ref[...] = side_ref[...].astype(side_o_ref.dtype)


def proj(h, w_in, col_starts, width, aux, epilogue, out_dtypes, side, *, tm=1024, tn=256):
    t, d = h.shape
    n_seg = len(col_starts)
    grid = (t // tm, width // tn)
    side_spec, side_shape, side_bytes = _side_cast_specs(side, *grid)

    def w_spec(start):
        off = start // tn
        return pl.BlockSpec((d, tn), lambda i, j: (0, off + j))

    pipelined = (_nbytes((tm, d), BF16) + n_seg * _nbytes((d, tn), BF16)
                 + sum(_nbytes((tm, tn), dt) for dt in out_dtypes) + side_bytes)
    return pl.pallas_call(
        functools.partial(_proj_kernel, n_seg=n_seg, n_aux=len(aux), epilogue=epilogue),
        grid=grid,
        in_specs=([pl.BlockSpec((tm, d), lambda i, j: (i, 0))]
                  + [w_spec(s) for s in col_starts]
                  + [pl.BlockSpec((1, tn), lambda i, j: (0, j)) for _ in aux]
                  + [side_spec]),
        out_specs=[pl.BlockSpec((tm, tn), lambda i, j: (i, j)) for _ in out_dtypes] + [side_spec],
        out_shape=[jax.ShapeDtypeStruct((t, width), dt) for dt in out_dtypes] + [side_shape],
        compiler_params=pltpu.CompilerParams(
            dimension_semantics=("parallel", "arbitrary"),
            vmem_limit_bytes=_vmem_limit(pipelined, (n_seg + 4) * _nbytes((tm, tn), F32))),
        name="proj",
    )(h, *([w_in] * n_seg), *[a.reshape(1, width) for a in aux], side)


def _hgrn_epilogue(ys, aux):
    q, fr, i, g = ys
    lb, = aux
    f = lb + (1.0 - lb) * jax.nn.sigmoid(fr)
    return _silu(q), jnp.log(f) * LOG2_E, 1.0 - f, i, _silu(g)


def _attn_epilogue(ys, aux):
    q, k, v = ys
    return q * (HEAD_DIM ** -0.5), k, v


def _gate_epilogue(ys, aux):
    ga, gb = ys
    ba, bb = aux
    return jax.nn.sigmoid(ga + ba), jax.nn.sigmoid(gb + bb)


def _lower_half_total(p_hb, level, row8):
    c, w = p_hb.shape
    hb = 1 << level
    if 2 * hb >= F32_SUBLANES:
        blk = min(2 * hb, c)
        return jnp.concatenate(
            [jnp.broadcast_to(p_hb[b * blk + hb - 1:b * blk + hb, :], (blk, w)) for b in range(c // blk)], axis=0)
    x = p_hb.reshape(c // F32_SUBLANES, F32_SUBLANES, w)
    odd = (row8 & 1) == 1
    if level == 0:
        g = jnp.where(odd, pltpu.roll(x, 1, 1), x)
    else:
        z = jnp.where(odd, x, pltpu.roll(x, 7, 1))
        g = jnp.where((row8 & 2) == 0, z, pltpu.roll(z, 2, 1))
    return g.reshape(c, w)


def _hgrn_kernel(qs_ref, lf_ref, kk_ref, v_ref, gs_ref, ng_ref, o_ref, st_ref, *, heads_per_block,
                 chunks_per_iter):
    c = CHUNK
    dk = HEAD_DIM
    w = qs_ref.shape[1]
    n_levels = CHUNK.bit_length() - 1

    @pl.when(pl.program_id(2) == 0)
    def _():
        st_ref[...] = jnp.zeros_like(st_ref)

    row = lax.broadcasted_iota(jnp.int32, (c, w), 0)
    row8 = lax.broadcasted_iota(jnp.int32, (c // F32_SUBLANES, F32_SUBLANES, w), 1)
    upper = [((row >> p) & 1) == 1 for p in range(n_levels)]
    ti = lax.broadcasted_iota(jnp.int32, (c, c), 0)
    si = lax.broadcasted_iota(jnp.int32, (c, c), 1)
    diag_mask = ti == si
    level_masks = [((ti >> (p + 1)) == (si >> (p + 1))) & (((ti >> p) & 1) == 1) & (((si >> p) & 1) == 0)
                   for p in range(n_levels)]

    def iter_body(it, carry):
        pre = []
        for cc in range(chunks_per_iter):
            rows = pl.ds(pl.multiple_of((it * chunks_per_iter + cc) * c, c), c)
            p_hb = lf_ref[rows, :]
            e_q, e_k = [], []
            for p in range(n_levels + 1):
                g = _lower_half_total(p_hb, p, row8)
                e_q.append(jnp.exp2(p_hb))
                e_k.append(None if p == 0 else jnp.exp2(g - p_hb))
                if p < n_levels:
                    p_hb = p_hb + jnp.where(upper[p], g, 0.0)
            pre.append((rows, e_q, e_k))
        indep = {}
        for cc, (rows, e_q, e_k) in enumerate(pre):
            for hh in range(heads_per_block):
                cols = slice(hh * dk, (hh + 1) * dk)
                qb = qs_ref[rows, cols]
                kb = kk_ref[rows, cols]
                v = v_ref[rows, cols]
                q = qb.astype(F32)
                k = kb.astype(F32)
                parts = [_dot_nt(qb, kb)]
                for p in range(n_levels):
                    qt = (q * e_q[p][:, cols]).astype(BF16)
                    kt = kb if p == 0 else (k * e_k[p][:, cols]).astype(BF16)
                    parts.append(_dot_nt(qt, kt))
                e_in = e_q[n_levels][:, cols]
                e_out = e_k[n_levels][:, cols]
                kv = _dot_tn(v, (k * e_out).astype(BF16))
                indep[cc, hh] = (parts, kv, (q * e_in).astype(BF16), v, e_in[c - 1:c, :])
        for hh in range(heads_per_block):
            cols = slice(hh * dk, (hh + 1) * dk)
            st = st_ref[hh]
            for cc, (rows, _, _) in enumerate(pre):
                parts, kv, q_in, v, decay_all = indep[cc, hh]
                scores = jnp.where(diag_mask, parts[0], 0.0)
                for p in range(n_levels):
                    scores = jnp.where(level_masks[p], parts[p + 1], scores)
                o = _dot(scores.astype(BF16), v) + _dot_nt(q_in, st.astype(BF16))
                st = st * decay_all + kv
                inv = lax.rsqrt(jnp.mean(o * o, axis=-1, keepdims=True) + NORM_EPS)
                o = o * inv * ng_ref[:, cols] * gs_ref[rows, cols].astype(F32)
                o_ref[rows, cols] = o.astype(o_ref.dtype)
            st_ref[hh] = st
        return carry

    lax.fori_loop(0, qs_ref.shape[0] // (c * chunks_per_iter), iter_body, 0)


def hgrn(qs, lf, kk, v, gs, norm_g, *, batch, seq, block_len=512, heads_per_block=4, chunks_per_iter=2):
    t, width = qs.shape
    bw = heads_per_block * HEAD_DIM
    n_l = seq // block_len
    tok = pl.BlockSpec((block_len, bw), lambda b, h, l: (b * n_l + l, h))
    pipelined = 4 * _nbytes((block_len, bw), BF16) + _nbytes((block_len, bw), F32)
    return pl.pallas_call(
        functools.partial(_hgrn_kernel, heads_per_block=heads_per_block, chunks_per_iter=chunks_per_iter),
        grid=(batch, width // bw, n_l),
        in_specs=[tok, tok, tok, tok, tok,
                  pl.BlockSpec((1, bw), lambda b, h, l: (0, h))],
        out_specs=tok,
        out_shape=jax.ShapeDtypeStruct((t, width), BF16),
        scratch_shapes=[pltpu.VMEM((heads_per_block, HEAD_DIM, HEAD_DIM), F32)],
        compiler_params=pltpu.CompilerParams(
            dimension_semantics=("parallel", "parallel", "arbitrary"),
            vmem_limit_bytes=_vmem_limit(pipelined, 0)),
        name="hgrn",
    )(qs, lf, kk, v, gs, norm_g.reshape(1, width))


ATTN_Q_BLOCK = CHUNK
ATTN_WINDOW = 640
ATTN_LEAD = ATTN_WINDOW - ATTN_Q_BLOCK
ATTN_BIAS_LANES = pl.cdiv(ATTN_WINDOW + ATTN_Q_BLOCK - 1, LANES) * LANES


def _attn_bias_row(rel_bias):
    m = np.arange(ATTN_BIAS_LANES)
    m = np.where(m >= ATTN_WINDOW, m - ATTN_BIAS_LANES, m)
    idx = np.clip(ATTN_LEAD - m, -MAX_REL, MAX_REL) + MAX_REL
    runs, start = [], 0
    for pos in range(1, len(idx) + 1):
        if pos == len(idx) or idx[pos] != idx[pos - 1] - 1:
            runs.append((start, pos))
            start = pos
    pieces = [jnp.flip(rel_bias[:, int(idx[hi - 1]):int(idx[lo]) + 1], axis=1) for lo, hi in runs]
    return jnp.concatenate(pieces, axis=1)


def _attn_kernel(q_ref, k_ref, v_ref, brow_ref, o_ref, kpad_ref, vpad_ref, *, group):
    seq = q_ref.shape[0]
    kpad_ref[0:ATTN_LEAD, :] = jnp.zeros((ATTN_LEAD, HEAD_DIM), kpad_ref.dtype)
    vpad_ref[0:ATTN_LEAD, :] = jnp.zeros((ATTN_LEAD, HEAD_DIM), vpad_ref.dtype)
    kpad_ref[ATTN_LEAD:, :] = k_ref[...]
    vpad_ref[ATTN_LEAD:, :] = v_ref[...]
    brow = jnp.broadcast_to(brow_ref[0], (ATTN_Q_BLOCK, ATTN_BIAS_LANES))
    toep = pltpu.roll(brow, 0, 1, stride=1, stride_axis=0)[:, :ATTN_WINDOW]
    qi = lax.broadcasted_iota(jnp.int32, (ATTN_Q_BLOCK, ATTN_WINDOW), 0)
    col = lax.broadcasted_iota(jnp.int32, (ATTN_Q_BLOCK, ATTN_WINDOW), 1)
    qc = qi // CHUNK
    kc = col // CHUNK
    lead = ATTN_LEAD // CHUNK
    bias = jnp.where((kc >= qc + lead - LEFT_CHUNKS) & (kc <= qc + lead), toep, MASK_VALUE)

    def do_group(base, masked):
        starts = [base + u * ATTN_Q_BLOCK for u in range(group)]
        scores = []
        for start in starts:
            q = q_ref[pl.ds(start, ATTN_Q_BLOCK), :]
            kw = kpad_ref[pl.ds(start, ATTN_WINDOW), :]
            s = _dot_nt(q, kw) + bias
            if masked:
                s = jnp.where(col >= ATTN_LEAD - start, s, MASK_VALUE)
            scores.append(s)
        probs = []
        for s in scores:
            m = jnp.max(s, axis=-1, keepdims=True)
            p = jnp.exp(s - m)
            probs.append((p.astype(BF16), jnp.sum(p, axis=-1, keepdims=True)))
        for start, (p, denom) in zip(starts, probs):
            vw = vpad_ref[pl.ds(start, ATTN_WINDOW), :]
            o = _dot(p, vw) / denom
            o_ref[pl.ds(start, ATTN_Q_BLOCK), :] = o.astype(o_ref.dtype)

    span = group * ATTN_Q_BLOCK
    n_masked = pl.cdiv(ATTN_LEAD, span)
    for g in range(n_masked):
        do_group(g * span, True)

    def body(g, carry):
        do_group(pl.multiple_of(g * span, span), False)
        return carry

    lax.fori_loop(n_masked, seq // span, body, 0)


def band_attn(q, k, v, bias_row, *, batch, seq, group=8):
    t, width = q.shape
    tok = pl.BlockSpec((seq, HEAD_DIM), lambda b, h: (b, h))
    return pl.pallas_call(
        functools.partial(_attn_kernel, group=group),
        grid=(batch, width // HEAD_DIM),
        in_specs=[tok, tok, tok,
                  pl.BlockSpec((1, 1, ATTN_BIAS_LANES), lambda b, h: (h, 0, 0))],
        out_specs=tok,
        out_shape=jax.ShapeDtypeStruct((t, width), BF16),
        scratch_shapes=[pltpu.VMEM((seq + ATTN_LEAD, HEAD_DIM), BF16),
                        pltpu.VMEM((seq + ATTN_LEAD, HEAD_DIM), BF16)],
        compiler_params=pltpu.CompilerParams(
            dimension_semantics=("parallel", "parallel"),
            vmem_limit_bytes=_vmem_limit(
                4 * _nbytes((seq, HEAD_DIM), BF16),
                2 * _nbytes((seq + ATTN_LEAD, HEAD_DIM), BF16))),
        name="band_attn",
    )(q, k, v, bias_row.reshape(bias_row.shape[0], 1, ATTN_BIAS_LANES))


def _mix_up_kernel(oa_ref, ob_ref, wa_ref, wb_ref, ga_ref, gb_ref, o_ref):
    ya = _dot(oa_ref[...], wa_ref[...])
    yb = _dot(ob_ref[...], wb_ref[...])
    o_ref[...] = (ga_ref[...].astype(F32) * ya + gb_ref[...].astype(F32) * yb).astype(o_ref.dtype)


def mix_up(oa, ob, wa, wb, ga, gb, *, tm=2048, tn=512):
    t, kdim = oa.shape
    d = wa.shape[1]
    pipelined = 2 * _nbytes((kdim, tn), BF16) + 3 * _nbytes((tm, tn), BF16)
    return pl.pallas_call(
        _mix_up_kernel,
        grid=(t // tm, d // tn),
        in_specs=[pl.BlockSpec((tm, kdim), lambda i, j: (i, 0), pipeline_mode=pl.Buffered(1)),
                  pl.BlockSpec((tm, kdim), lambda i, j: (i, 0), pipeline_mode=pl.Buffered(1)),
                  pl.BlockSpec((kdim, tn), lambda i, j: (0, j)),
                  pl.BlockSpec((kdim, tn), lambda i, j: (0, j)),
                  pl.BlockSpec((tm, tn), lambda i, j: (i, j)),
                  pl.BlockSpec((tm, tn), lambda i, j: (i, j))],
        out_specs=pl.BlockSpec((tm, tn), lambda i, j: (i, j)),
        out_shape=jax.ShapeDtypeStruct((t, d), BF16),
        compiler_params=pltpu.CompilerParams(
            dimension_semantics=("parallel", "arbitrary"),
            vmem_limit_bytes=_vmem_limit(pipelined, 2 * _nbytes((tm, kdim), BF16) + 4 * _nbytes((tm, tn), F32))),
        name="mix_up",
    )(oa, ob, wa, wb, ga, gb)


def _ffn(x, h, w1, w3, w2, post_g, next_g, emit_next, side=None):
    g, w2_b = glu_up(h, w1, w3, w2)
    if side is None:
        y, side_b = matmul_kres(g, w2_b), None
    else:
        y, side_b = matmul_kres(g, w2_b, side)
    x_new, h_next = resnorm(y, x, post_g, next_g, scale=MACARON_WEIGHT, emit_next=emit_next)
    return x_new, h_next, side_b


def kernel(x, ffn1_pre_g, ffn1_post_g, ffn1_w1, ffn1_w3, ffn1_w2, mix_pre_g, mix_post_g, w_in, b_gate,
           hgrn_lb_logits, hgrn_norm_g, rel_bias, w_up_a, w_up_b, w_out,
           ffn2_pre_g, ffn2_post_g, ffn2_w1, ffn2_w3, ffn2_w2):
    batch, seq, d = x.shape
    depth = ffn1_w1.shape[0]
    d_half = d // 2
    lower_bounds = jnp.cumsum(jax.nn.softmax(hgrn_lb_logits.astype(F32), axis=0), axis=0)
    xt = x.reshape(batch * seq, d)
    h = norm_cast(xt, ffn1_pre_g[0])
    for layer in range(depth):
        xt, h, w_in_b = _ffn(xt, h, ffn1_w1[layer], ffn1_w3[layer], ffn1_w2[layer],
                             ffn1_post_g[layer], mix_pre_g[layer], True, side=w_in[layer])

        qs, lf, kk, vv, gs, w_up_a_b = proj(h, w_in_b, [0, d_half, 2 * d_half, 3 * d_half], d_half,
                                            [lower_bounds[layer]], _hgrn_epilogue,
                                            [BF16, F32, BF16, BF16, BF16], w_up_a[layer])
        qb, kb, vb, w_up_b_b = proj(h, w_in_b, [4 * d_half, 5 * d_half, 6 * d_half], d_half,
                                    [], _attn_epilogue, [BF16, BF16, BF16], w_up_b[layer])
        ga, gb, w_out_b = proj(h, w_in_b, [7 * d_half, 7 * d_half + d], d,
                               [b_gate[layer, 0], b_gate[layer, 1]], _gate_epilogue, [BF16, BF16],
                               w_out[layer], tn=512)

        oa = hgrn(qs, lf, kk, vv, gs, hgrn_norm_g[layer], batch=batch, seq=seq)
        ob = band_attn(qb, kb, vb, _attn_bias_row(rel_bias[layer].astype(F32)), batch=batch, seq=seq)

        m = mix_up(oa, ob, w_up_a_b, w_up_b_b, ga, gb)
        last = layer == depth - 1
        y = matmul_kres(m, w_out_b, tm=1024)
        xt, h = resnorm(y, xt, mix_post_g[layer], ffn2_pre_g[layer], scale=1.0, emit_next=True)
        next_pre = ffn1_pre_g[layer + 1] if not last else ffn2_pre_g[layer]
        xt, h, _ = _ffn(xt, h, ffn2_w1[layer], ffn2_w3[layer], ffn2_w2[layer],
                        ffn2_post_g[layer], next_pre, not last)
    return xt.reshape(batch, seq, d)
```

```python
import functools

import numpy as np
import jax
import jax.numpy as jnp
from jax import lax
from jax.experimental import pallas as pl
from jax.experimental.pallas import tpu as pltpu

F32 = jnp.float32
BF16 = jnp.bfloat16

NORM_EPS = 1e-6
LOG2_E = 1.4426950408889634
MACARON_WEIGHT = 0.5
CHUNK = 64
HEAD_DIM = 128
LANES = 128
F32_SUBLANES = 8
BF16_SUBLANES = 16
LEFT_CHUNKS = 8
MAX_REL = 256
MASK_VALUE = -1e30

MIB = 1024 * 1024
V7X_VMEM_BYTES = 64 * MIB
VMEM_LIMIT_CAP = V7X_VMEM_BYTES - 6 * MIB
VMEM_LIMIT_FLOOR = 32 * MIB
VMEM_SPILL_BYTES = 8 * MIB


def _vmem_limit(pipelined_bytes, resident_bytes=0):
    est = 2 * pipelined_bytes + resident_bytes + VMEM_SPILL_BYTES
    return int(min(max(est, VMEM_LIMIT_FLOOR), VMEM_LIMIT_CAP))


def _nbytes(shape, dtype):
    return int(np.prod(shape)) * jnp.dtype(dtype).itemsize


def _dot(a, b):
    return jnp.dot(a, b, preferred_element_type=F32)


def _dot_nt(a, b):
    return lax.dot_general(a, b, (((1,), (1,)), ((), ())), preferred_element_type=F32)


def _dot_tn(a, b):
    return lax.dot_general(a, b, (((0,), (0,)), ((), ())), preferred_element_type=F32)


def _silu(x):
    return x * jax.nn.sigmoid(x)


def _side_cast_specs(side, n_i, n_j):
    rows, cols = side.shape
    slab, rem = divmod(rows, n_i * n_j)
    assert rem == 0 and slab % BF16_SUBLANES == 0, (rows, n_i, n_j)
    spec = pl.BlockSpec((slab, cols), lambda i, j: (i * n_j + j, 0))
    return spec, jax.ShapeDtypeStruct((rows, cols), BF16), _nbytes((slab, cols), F32) + _nbytes((slab, cols), BF16)


def _norm_cast_kernel(x_ref, g_ref, o_ref):
    x = x_ref[...]
    inv = lax.rsqrt(jnp.mean(x * x, axis=-1, keepdims=True) + NORM_EPS)
    o_ref[...] = (x * inv * g_ref[...]).astype(o_ref.dtype)


def norm_cast(x, g, *, tm=256):
    t, d = x.shape
    return pl.pallas_call(
        _norm_cast_kernel,
        grid=(t // tm,),
        in_specs=[pl.BlockSpec((tm, d), lambda i: (i, 0)),
                  pl.BlockSpec((1, d), lambda i: (0, 0))],
        out_specs=pl.BlockSpec((tm, d), lambda i: (i, 0)),
        out_shape=jax.ShapeDtypeStruct((t, d), BF16),
        compiler_params=pltpu.CompilerParams(
            dimension_semantics=("parallel",),
            vmem_limit_bytes=_vmem_limit(_nbytes((tm, d), F32) + _nbytes((tm, d), BF16),
                                         3 * _nbytes((tm, d), F32))),
        name="norm_cast",
    )(x, g.reshape(1, d))


def _glu_up_kernel(h_ref, w1_ref, w3_ref, side_ref, o_ref, side_o_ref):
    h = h_ref[...]
    a = _dot(h, w1_ref[...].astype(BF16))
    b = _dot(h, w3_ref[...].astype(BF16))
    o_ref[...] = (_silu(a) * b).astype(o_ref.dtype)
    side_o_ref[...] = side_ref[...].astype(side_o_ref.dtype)


def glu_up(h, w1, w3, side, *, tm=2048, tn=256):
    t, d = h.shape
    f = w1.shape[1]
    grid = (t // tm, f // tn)
    side_spec, side_shape, side_bytes = _side_cast_specs(side, *grid)
    return pl.pallas_call(
        _glu_up_kernel,
        grid=grid,
        in_specs=[pl.BlockSpec((tm, d), lambda i, j: (i, 0), pipeline_mode=pl.Buffered(1)),
                  pl.BlockSpec((d, tn), lambda i, j: (0, j)),
                  pl.BlockSpec((d, tn), lambda i, j: (0, j)),
                  side_spec],
        out_specs=[pl.BlockSpec((tm, tn), lambda i, j: (i, j)), side_spec],
        out_shape=[jax.ShapeDtypeStruct((t, f), BF16), side_shape],
        compiler_params=pltpu.CompilerParams(
            dimension_semantics=("parallel", "arbitrary"),
            vmem_limit_bytes=_vmem_limit(
                2 * _nbytes((d, tn), w1.dtype) + _nbytes((tm, tn), BF16) + side_bytes,
                _nbytes((tm, d), BF16) + 4 * _nbytes((tm, tn), F32) + 2 * _nbytes((d, tn), BF16))),
        name="glu_up",
    )(h, w1, w3, side)


def _matmul_kernel(a_ref, w_ref, *rest):
    maybe_side_ref, o_ref, maybe_side_o_ref = rest if len(rest) == 3 else (None, rest[0], None)
    o_ref[...] = _dot(a_ref[...], w_ref[...]).astype(o_ref.dtype)
    if maybe_side_ref is not None:
        maybe_side_o_ref[...] = maybe_side_ref[...].astype(maybe_side_o_ref.dtype)


def matmul_kres(a, w, side=None, *, tm=512, tn=1024):
    t, kdim = a.shape
    n = w.shape[1]
    grid = (n // tn, t // tm)
    in_specs = [pl.BlockSpec((tm, kdim), lambda j, i: (i, 0)),
                pl.BlockSpec((kdim, tn), lambda j, i: (0, j), pipeline_mode=pl.Buffered(1))]
    out_specs = [pl.BlockSpec((tm, tn), lambda j, i: (i, j))]
    out_shape = [jax.ShapeDtypeStruct((t, n), BF16)]
    operands = [a, w]
    side_bytes = 0
    if side is not None:
        side_spec, side_shape, side_bytes = _side_cast_specs(side, *grid)
        in_specs.append(side_spec)
        out_specs.append(side_spec)
        out_shape.append(side_shape)
        operands.append(side)
    outs = pl.pallas_call(
        _matmul_kernel,
        grid=grid,
        in_specs=in_specs,
        out_specs=out_specs,
        out_shape=out_shape,
        compiler_params=pltpu.CompilerParams(
            dimension_semantics=("parallel", "arbitrary"),
            vmem_limit_bytes=_vmem_limit(
                _nbytes((tm, kdim), BF16) + _nbytes((tm, tn), BF16) + side_bytes,
                _nbytes((kdim, tn), BF16) + 2 * _nbytes((tm, tn), F32))),
        name="matmul_kres",
    )(*operands)
    return outs if side is not None else outs[0]


def _resnorm_kernel(y_ref, x_ref, gpost_ref, gnext_ref, xo_ref, *maybe_h_ref, scale):
    y = y_ref[...].astype(F32)
    inv = lax.rsqrt(jnp.mean(y * y, axis=-1, keepdims=True) + NORM_EPS)
    xn = x_ref[...] + scale * (y * inv * gpost_ref[...])
    xo_ref[...] = xn
    if maybe_h_ref:
        inv2 = lax.rsqrt(jnp.mean(xn * xn, axis=-1, keepdims=True) + NORM_EPS)
        maybe_h_ref[0][...] = (xn * inv2 * gnext_ref[...]).astype(BF16)


def resnorm(y, x, g_post, g_next, *, scale, emit_next, tm=256):
    t, d = x.shape
    row = pl.BlockSpec((tm, d), lambda i: (i, 0))
    vec = pl.BlockSpec((1, d), lambda i: (0, 0))
    out_shape = [jax.ShapeDtypeStruct((t, d), F32)]
    if emit_next:
        out_shape.append(jax.ShapeDtypeStruct((t, d), BF16))
    pipelined = (_nbytes((tm, d), y.dtype) + 2 * _nbytes((tm, d), F32)
                 + (_nbytes((tm, d), BF16) if emit_next else 0))
    outs = pl.pallas_call(
        functools.partial(_resnorm_kernel, scale=scale),
        grid=(t // tm,),
        in_specs=[row, row, vec, vec],
        out_specs=[row] * len(out_shape),
        out_shape=out_shape,
        compiler_params=pltpu.CompilerParams(
            dimension_semantics=("parallel",),
            vmem_limit_bytes=_vmem_limit(pipelined, 3 * _nbytes((tm, d), F32))),
        name="resnorm",
    )(y, x, g_post.reshape(1, d), g_next.reshape(1, d))
    return outs if emit_next else (outs[0], None)


def _proj_kernel(*refs, n_seg, n_aux, epilogue):
    h_ref = refs[0]
    w_refs = refs[1:1 + n_seg]
    aux_refs = refs[1 + n_seg:1 + n_seg + n_aux]
    side_ref = refs[1 + n_seg + n_aux]
    out_refs = refs[2 + n_seg + n_aux:-1]
    side_o_ref = refs[-1]
    h = h_ref[...]
    ys = [_dot(h, w_ref[...]) for w_ref in w_refs]
    outs = epilogue(ys, [r[...] for r in aux_refs])
    for o_ref, o in zip(out_refs, outs):
        o_ref[...] = o.astype(o_ref.dtype)
    side_o_ref[...] = side_ref[...].astype(side_o_ref.dtype)


def proj(h, w_in, col_starts, width, aux, epilogue, out_dtypes, side, *, tm=1024, tn=256):
    t, d = h.shape
    n_seg = len(col_starts)
    grid = (t // tm, width // tn)
    side_spec, side_shape, side_bytes = _side_cast_specs(side, *grid)

    def w_spec(start):
        off = start // tn
        return pl.BlockSpec((d, tn), lambda i, j: (0, off + j))

    pipelined = (_nbytes((tm, d), BF16) + n_seg * _nbytes((d, tn), BF16)
                 + sum(_nbytes((tm, tn), dt) for dt in out_dtypes) + side_bytes)
    return pl.pallas_call(
        functools.partial(_proj_kernel, n_seg=n_seg, n_aux=len(aux), epilogue=epilogue),
        grid=grid,
        in_specs=([pl.BlockSpec((tm, d), lambda i, j: (i, 0))]
                  + [w_spec(s) for s in col_starts]
                  + [pl.BlockSpec((1, tn), lambda i, j: (0, j)) for _ in aux]
                  + [side_spec]),
        out_specs=[pl.BlockSpec((tm, tn), lambda i, j: (i, j)) for _ in out_dtypes] + [side_spec],
        out_shape=[jax.ShapeDtypeStruct((t, width), dt) for dt in out_dtypes] + [side_shape],
        compiler_params=pltpu.CompilerParams(
            dimension_semantics=("parallel", "arbitrary"),
            vmem_limit_bytes=_vmem_limit(pipelined, (n_seg + 4) * _nbytes((tm, tn), F32))),
        name="proj",
    )(h, *([w_in] * n_seg), *[a.reshape(1, width) for a in aux], side)


def _hgrn_epilogue(ys, aux):
    q, fr, i, g = ys
    lb, = aux
    f = lb + (1.0 - lb) * jax.nn.sigmoid(fr)
    return _silu(q), jnp.log(f) * LOG2_E, 1.0 - f, i, _silu(g)


def _attn_epilogue(ys, aux):
    q, k, v = ys
    return q * (HEAD_DIM ** -0.5), k, v


def _gate_epilogue(ys, aux):
    ga, gb = ys
    ba, bb = aux
    return jax.nn.sigmoid(ga + ba), jax.nn.sigmoid(gb + bb)


def _lower_half_total(p_hb, level, row8):
    c, w = p_hb.shape
    hb = 1 << level
    if 2 * hb >= F32_SUBLANES:
        blk = min(2 * hb, c)
        return jnp.concatenate(
            [jnp.broadcast_to(p_hb[b * blk + hb - 1:b * blk + hb, :], (blk, w)) for b in range(c // blk)], axis=0)
    x = p_hb.reshape(c // F32_SUBLANES, F32_SUBLANES, w)
    odd = (row8 & 1) == 1
    if level == 0:
        g = jnp.where(odd, pltpu.roll(x, 1, 1), x)
    else:
        z = jnp.where(odd, x, pltpu.roll(x, 7, 1))
        g = jnp.where((row8 & 2) == 0, z, pltpu.roll(z, 2, 1))
    return g.reshape(c, w)


def _add_to_upper_halves(p_hb, g, level, upper_mask):
    c = p_hb.shape[0]
    hb = 1 << level
    if hb < F32_SUBLANES:
        return p_hb + jnp.where(upper_mask, g, 0.0)
    pieces = []
    for lo in range(0, c, 2 * hb):
        pieces += [p_hb[lo:lo + hb], p_hb[lo + hb:lo + 2 * hb] + g[lo + hb:lo + 2 * hb]]
    return jnp.concatenate(pieces, axis=0)


def _hgrn_kernel(qs_ref, lf_ref, kk_ref, v_ref, gs_ref, ng_ref, o_ref, st_ref, *, heads_per_block,
                 chunks_per_iter):
    c = CHUNK
    dk = HEAD_DIM
    w = qs_ref.shape[1]
    n_levels = CHUNK.bit_length() - 1

    @pl.when(pl.program_id(2) == 0)
    def _():
        st_ref[...] = jnp.zeros_like(st_ref)

    row = lax.broadcasted_iota(jnp.int32, (c, w), 0)
    row8 = lax.broadcasted_iota(jnp.int32, (c // F32_SUBLANES, F32_SUBLANES, w), 1)
    upper = [((row >> p) & 1) == 1 for p in range(n_levels)]
    ti = lax.broadcasted_iota(jnp.int32, (c, c), 0)
    si = lax.broadcasted_iota(jnp.int32, (c, c), 1)
    diag_mask = ti == si
    level_masks = [((ti >> (p + 1)) == (si >> (p + 1))) & (((ti >> p) & 1) == 1) & (((si >> p) & 1) == 0)
                   for p in range(n_levels)]

    def iter_body(it, carry):
        pre = []
        for cc in range(chunks_per_iter):
            rows = pl.ds(pl.multiple_of((it * chunks_per_iter + cc) * c, c), c)
            p_hb = lf_ref[rows, :]
            e_q, e_k = [], []
            for p in range(n_levels + 1):
                g = _lower_half_total(p_hb, p, row8)
                e_q.append(jnp.exp2(p_hb))
                e_k.append(None if p == 0 else jnp.exp2(g - p_hb))
                if p < n_levels:
                    p_hb = _add_to_upper_halves(p_hb, g, p, upper[p])
            pre.append((rows, e_q, e_k))
        indep = {}
        for cc, (rows, e_q, e_k) in enumerate(pre):
            for hh in range(heads_per_block):
                cols = slice(hh * dk, (hh + 1) * dk)
                qb = qs_ref[rows, cols]
                kb = kk_ref[rows, cols]
                v = v_ref[rows, cols]
                q = qb.astype(F32)
                k = kb.astype(F32)
                parts = [_dot_nt(qb, kb)]
                for p in range(n_levels):
                    qt = (q * e_q[p][:, cols]).astype(BF16)
                    kt = kb if p == 0 else (k * e_k[p][:, cols]).astype(BF16)
                    parts.append(_dot_nt(qt, kt))
                e_in = e_q[n_levels][:, cols]
                e_out = e_k[n_levels][:, cols]
                kv = _dot_tn(v, (k * e_out).astype(BF16))
                indep[cc, hh] = (parts, kv, (q * e_in).astype(BF16), v, e_in[c - 1:c, :])
        for hh in range(heads_per_block):
            cols = slice(hh * dk, (hh + 1) * dk)
            st = st_ref[hh]
            for cc, (rows, _, _) in enumerate(pre):
                parts, kv, q_in, v, decay_all = indep[cc, hh]
                scores = jnp.where(diag_mask, parts[0], 0.0)
                for p in range(n_levels):
                    scores = jnp.where(level_masks[p], parts[p + 1], scores)
                o = _dot(scores.astype(BF16), v) + _dot_nt(q_in, st.astype(BF16))
                st = st * decay_all + kv
                inv = lax.rsqrt(jnp.mean(o * o, axis=-1, keepdims=True) + NORM_EPS)
                o = o * inv * ng_ref[:, cols] * gs_ref[rows, cols].astype(F32)
                o_ref[rows, cols] = o.astype(o_ref.dtype)
            st_ref[hh] = st
        return carry

    lax.fori_loop(0, qs_ref.shape[0] // (c * chunks_per_iter), iter_body, 0)


def hgrn(qs, lf, kk, v, gs, norm_g, *, batch, seq, block_len=512, heads_per_block=4, chunks_per_iter=2):
    t, width = qs.shape
    bw = heads_per_block * HEAD_DIM
    n_l = seq // block_len
    tok = pl.BlockSpec((block_len, bw), lambda b, h, l: (b * n_l + l, h))
    pipelined = 4 * _nbytes((block_len, bw), BF16) + _nbytes((block_len, bw), F32)
    return pl.pallas_call(
        functools.partial(_hgrn_kernel, heads_per_block=heads_per_block, chunks_per_iter=chunks_per_iter),
        grid=(batch, width // bw, n_l),
        in_specs=[tok, tok, tok, tok, tok,
                  pl.BlockSpec((1, bw), lambda b, h, l: (0, h))],
        out_specs=tok,
        out_shape=jax.ShapeDtypeStruct((t, width), BF16),
        scratch_shapes=[pltpu.VMEM((heads_per_block, HEAD_DIM, HEAD_DIM), F32)],
        compiler_params=pltpu.CompilerParams(
            dimension_semantics=("parallel", "parallel", "arbitrary"),
            vmem_limit_bytes=_vmem_limit(pipelined, 0)),
        name="hgrn",
    )(qs, lf, kk, v, gs, norm_g.reshape(1, width))


ATTN_Q_BLOCK = 2 * CHUNK
ATTN_WINDOW = 640
ATTN_LEAD = ATTN_WINDOW - ATTN_Q_BLOCK
ATTN_BIAS_LANES = pl.cdiv(ATTN_WINDOW + ATTN_Q_BLOCK - 1, LANES) * LANES


def _attn_bias_row(rel_bias):
    m = np.arange(ATTN_BIAS_LANES)
    m = np.where(m >= ATTN_WINDOW, m - ATTN_BIAS_LANES, m)
    idx = np.clip(ATTN_LEAD - m, -MAX_REL, MAX_REL) + MAX_REL
    runs, start = [], 0
    for pos in range(1, len(idx) + 1):
        if pos == len(idx) or idx[pos] != idx[pos - 1] - 1:
            runs.append((start, pos))
            start = pos
    pieces = [jnp.flip(rel_bias[:, int(idx[hi - 1]):int(idx[lo]) + 1], axis=1) for lo, hi in runs]
    return jnp.concatenate(pieces, axis=1)


def _attn_kernel(q_ref, k_ref, v_ref, brow_ref, o_ref, kpad_ref, vpad_ref, *, group):
    seq = q_ref.shape[0]
    kpad_ref[0:ATTN_LEAD, :] = jnp.zeros((ATTN_LEAD, HEAD_DIM), kpad_ref.dtype)
    vpad_ref[0:ATTN_LEAD, :] = jnp.zeros((ATTN_LEAD, HEAD_DIM), vpad_ref.dtype)
    kpad_ref[ATTN_LEAD:, :] = k_ref[...]
    vpad_ref[ATTN_LEAD:, :] = v_ref[...]
    brow = jnp.broadcast_to(brow_ref[0], (ATTN_Q_BLOCK, ATTN_BIAS_LANES))
    toep = pltpu.roll(brow, 0, 1, stride=1, stride_axis=0)[:, :ATTN_WINDOW]
    qi = lax.broadcasted_iota(jnp.int32, (ATTN_Q_BLOCK, ATTN_WINDOW), 0)
    col = lax.broadcasted_iota(jnp.int32, (ATTN_Q_BLOCK, ATTN_WINDOW), 1)
    qc = qi // CHUNK
    kc = col // CHUNK
    lead = ATTN_LEAD // CHUNK
    bias = jnp.where((kc >= qc + lead - LEFT_CHUNKS) & (kc <= qc + lead), toep, MASK_VALUE)

    def do_group(base, masked):
        starts = [base + u * ATTN_Q_BLOCK for u in range(group)]
        scores = []
        for start in starts:
            q = q_ref[pl.ds(start, ATTN_Q_BLOCK), :]
            kw = kpad_ref[pl.ds(start, ATTN_WINDOW), :]
            s = _dot_nt(q, kw) + bias
            if masked:
                s = jnp.where(col >= ATTN_LEAD - start, s, MASK_VALUE)
            scores.append(s)
        probs = []
        for s in scores:
            m = jnp.max(s, axis=-1, keepdims=True)
            p = jnp.exp(s - m)
            probs.append((p.astype(BF16), jnp.sum(p, axis=-1, keepdims=True)))
        for start, (p, denom) in zip(starts, probs):
            vw = vpad_ref[pl.ds(start, ATTN_WINDOW), :]
            o = _dot(p, vw) / denom
            o_ref[pl.ds(start, ATTN_Q_BLOCK), :] = o.astype(o_ref.dtype)

    span = group * ATTN_Q_BLOCK
    n_masked = pl.cdiv(ATTN_LEAD, span)
    for g in range(n_masked):
        do_group(g * span, True)

    def body(g, carry):
        do_group(pl.multiple_of(g * span, span), False)
        return carry

    lax.fori_loop(n_masked, seq // span, body, 0)


def band_attn(q, k, v, bias_row, *, batch, seq, group=8):
    t, width = q.shape
    tok = pl.BlockSpec((seq, HEAD_DIM), lambda b, h: (b, h))
    return pl.pallas_call(
        functools.partial(_attn_kernel, group=group),
        grid=(batch, width // HEAD_DIM),
        in_specs=[tok, tok, tok,
                  pl.BlockSpec((1, 1, ATTN_BIAS_LANES), lambda b, h: (h, 0, 0))],
        out_specs=tok,
        out_shape=jax.ShapeDtypeStruct((t, width), BF16),
        scratch_shapes=[pltpu.VMEM((seq + ATTN_LEAD, HEAD_DIM), BF16),
                        pltpu.VMEM((seq + ATTN_LEAD, HEAD_DIM), BF16)],
        compiler_params=pltpu.CompilerParams(
            dimension_semantics=("parallel", "parallel"),
            vmem_limit_bytes=_vmem_limit(
                4 * _nbytes((seq, HEAD_DIM), BF16),
                2 * _nbytes((seq + ATTN_LEAD, HEAD_DIM), BF16))),
        name="band_attn",
    )(q, k, v, bias_row.reshape(bias_row.shape[0], 1, ATTN_BIAS_LANES))


def _mix_up_kernel(oa_ref, ob_ref, wa_ref, wb_ref, ga_ref, gb_ref, o_ref):
    ya = _dot(oa_ref[...], wa_ref[...])
    yb = _dot(ob_ref[...], wb_ref[...])
    o_ref[...] = (ga_ref[...].astype(F32) * ya + gb_ref[...].astype(F32) * yb).astype(o_ref.dtype)


def mix_up(oa, ob, wa, wb, ga, gb, *, tm=1024, tn=512):
    t, kdim = oa.shape
    d = wa.shape[1]
    pipelined = (2 * _nbytes((tm, kdim), BF16) + 2 * _nbytes((kdim, tn), BF16) + 3 * _nbytes((tm, tn), BF16))
    return pl.pallas_call(
        _mix_up_kernel,
        grid=(t // tm, d // tn),
        in_specs=[pl.BlockSpec((tm, kdim), lambda i, j: (i, 0)),
                  pl.BlockSpec((tm, kdim), lambda i, j: (i, 0)),
                  pl.BlockSpec((kdim, tn), lambda i, j: (0, j)),
                  pl.BlockSpec((kdim, tn), lambda i, j: (0, j)),
                  pl.BlockSpec((tm, tn), lambda i, j: (i, j)),
                  pl.BlockSpec((tm, tn), lambda i, j: (i, j))],
        out_specs=pl.BlockSpec((tm, tn), lambda i, j: (i, j)),
        out_shape=jax.ShapeDtypeStruct((t, d), BF16),
        compiler_params=pltpu.CompilerParams(
            dimension_semantics=("parallel", "arbitrary"),
            vmem_limit_bytes=_vmem_limit(pipelined, 4 * _nbytes((tm, tn), F32))),
        name="mix_up",
    )(oa, ob, wa, wb, ga, gb)


def _ffn(x, h, w1, w3, w2, post_g, next_g, emit_next, side=None):
    g, w2_b = glu_up(h, w1, w3, w2)
    if side is None:
        y, side_b = matmul_kres(g, w2_b), None
    else:
        y, side_b = matmul_kres(g, w2_b, side)
    x_new, h_next = resnorm(y, x, post_g, next_g, scale=MACARON_WEIGHT, emit_next=emit_next)
    return x_new, h_next, side_b


def kernel(x, ffn1_pre_g, ffn1_post_g, ffn1_w1, ffn1_w3, ffn1_w2, mix_pre_g, mix_post_g, w_in, b_gate,
           hgrn_lb_logits, hgrn_norm_g, rel_bias, w_up_a, w_up_b, w_out,
           ffn2_pre_g, ffn2_post_g, ffn2_w1, ffn2_w3, ffn2_w2):
    batch, seq, d = x.shape
    depth = ffn1_w1.shape[0]
    d_half = d // 2
    lower_bounds = jnp.cumsum(jax.nn.softmax(hgrn_lb_logits.astype(F32), axis=0), axis=0)
    xt = x.reshape(batch * seq, d)
    h = norm_cast(xt, ffn1_pre_g[0])
    for layer in range(depth):
        xt, h, w_in_b = _ffn(xt, h, ffn1_w1[layer], ffn1_w3[layer], ffn1_w2[layer],
                             ffn1_post_g[layer], mix_pre_g[layer], True, side=w_in[layer])

        qs, lf, kk, vv, gs, w_up_a_b = proj(h, w_in_b, [0, d_half, 2 * d_half, 3 * d_half], d_half,
                                            [lower_bounds[layer]], _hgrn_epilogue,
                                            [BF16, F32, BF16, BF16, BF16], w_up_a[layer])
        qb, kb, vb, w_up_b_b = proj(h, w_in_b, [4 * d_half, 5 * d_half, 6 * d_half], d_half,
                                    [], _attn_epilogue, [BF16, BF16, BF16], w_up_b[layer])
        ga, gb, w_out_b = proj(h, w_in_b, [7 * d_half, 7 * d_half + d], d,
                               [b_gate[layer, 0], b_gate[layer, 1]], _gate_epilogue, [BF16, BF16],
                               w_out[layer], tn=512)

        oa = hgrn(qs, lf, kk, vv, gs, hgrn_norm_g[layer], batch=batch, seq=seq)
        ob = band_attn(qb, kb, vb, _attn_bias_row(rel_bias[layer].astype(F32)), batch=batch, seq=seq)

        m = mix_up(oa, ob, w_up_a_b, w_up_b_b, ga, gb)
        last = layer == depth - 1
        y = matmul_kres(m, w_out_b, tm=1024)
        xt, h = resnorm(y, xt, mix_post_g[layer], ffn2_pre_g[layer], scale=1.0, emit_next=True)
        next_pre = ffn1_pre_g[layer + 1] if not last else ffn2_pre_g[layer]
        xt, h, _ = _ffn(xt, h, ffn2_w1[layer], ffn2_w3[layer], ffn2_w2[layer],
                        ffn2_post_g[layer], next_pre, not last)
    return xt.reshape(batch, seq, d)
```

```python
import functools

import numpy as np
import jax
import jax.numpy as jnp
from jax import lax
from jax.experimental import pallas as pl
from jax.experimental.pallas import tpu as pltpu

F32 = jnp.float32
BF16 = jnp.bfloat16

NORM_EPS = 1e-6
LOG2_E = 1.4426950408889634
MACARON_WEIGHT = 0.5
CHUNK = 64
HEAD_DIM = 128
LANES = 128
F32_SUBLANES = 8
BF16_SUBLANES = 16
LEFT_CHUNKS = 8
MAX_REL = 256
PROJ_TILE_COLS = 256
MASK_VALUE = -1e30

MIB = 1024 * 1024
V7X_VMEM_BYTES = 64 * MIB
VMEM_LIMIT_CAP = V7X_VMEM_BYTES - 6 * MIB
VMEM_LIMIT_FLOOR = 32 * MIB
VMEM_SPILL_BYTES = 8 * MIB


def _vmem_limit(pipelined_bytes, resident_bytes=0):
    est = 2 * pipelined_bytes + resident_bytes + VMEM_SPILL_BYTES
    return int(min(max(est, VMEM_LIMIT_FLOOR), VMEM_LIMIT_CAP))


def _nbytes(shape, dtype):
    return int(np.prod(shape)) * jnp.dtype(dtype).itemsize


def _dot(a, b):
    return jnp.dot(a, b, preferred_element_type=F32)


def _dot_nt(a, b):
    return lax.dot_general(a, b, (((1,), (1,)), ((), ())), preferred_element_type=F32)


def _dot_tn(a, b):
    return lax.dot_general(a, b, (((0,), (0,)), ((), ())), preferred_element_type=F32)


def _silu(x):
    return x * jax.nn.sigmoid(x)


def _side_cast_specs(side, n_i, n_j, tile_cols=None):
    rows, cols = side.shape
    slab, rem = divmod(rows, n_i * n_j)
    assert rem == 0 and slab % BF16_SUBLANES == 0, (rows, n_i, n_j)
    in_spec = pl.BlockSpec((slab, cols), lambda i, j: (i * n_j + j, 0))
    nbytes = _nbytes((slab, cols), F32) + _nbytes((slab, cols), BF16)
    if tile_cols is None:
        return in_spec, in_spec, jax.ShapeDtypeStruct((rows, cols), BF16), nbytes
    n_tiles, rem = divmod(cols, tile_cols)
    assert rem == 0, (cols, tile_cols)
    out_spec = pl.BlockSpec((n_tiles, slab, tile_cols), lambda i, j: (0, i * n_j + j, 0))
    return in_spec, out_spec, jax.ShapeDtypeStruct((n_tiles, rows, tile_cols), BF16), nbytes


def _side_cast(side_ref, side_o_ref):
    if len(side_o_ref.shape) == 2:
        side_o_ref[...] = side_ref[...].astype(side_o_ref.dtype)
    else:
        tile_cols = side_o_ref.shape[2]
        for tile in range(side_o_ref.shape[0]):
            side_o_ref[tile] = side_ref[:, tile * tile_cols:(tile + 1) * tile_cols].astype(side_o_ref.dtype)


def _norm_cast_kernel(x_ref, g_ref, o_ref):
    x = x_ref[...]
    inv = lax.rsqrt(jnp.mean(x * x, axis=-1, keepdims=True) + NORM_EPS)
    o_ref[...] = (x * inv * g_ref[...]).astype(o_ref.dtype)


def norm_cast(x, g, *, tm=256):
    t, d = x.shape
    return pl.pallas_call(
        _norm_cast_kernel,
        grid=(t // tm,),
        in_specs=[pl.BlockSpec((tm, d), lambda i: (i, 0)),
                  pl.BlockSpec((1, d), lambda i: (0, 0))],
        out_specs=pl.BlockSpec((tm, d), lambda i: (i, 0)),
        out_shape=jax.ShapeDtypeStruct((t, d), BF16),
        compiler_params=pltpu.CompilerParams(
            dimension_semantics=("parallel",),
            vmem_limit_bytes=_vmem_limit(_nbytes((tm, d), F32) + _nbytes((tm, d), BF16),
                                         3 * _nbytes((tm, d), F32))),
        name="norm_cast",
    )(x, g.reshape(1, d))


def _glu_up_kernel(h_ref, w1_ref, w3_ref, side_ref, o_ref, side_o_ref):
    h = h_ref[...]
    a = _dot(h, w1_ref[...].astype(BF16))
    b = _dot(h, w3_ref[...].astype(BF16))
    o_ref[...] = (_silu(a) * b).astype(o_ref.dtype)
    _side_cast(side_ref, side_o_ref)


def glu_up(h, w1, w3, side, *, tm=2048, tn=256):
    t, d = h.shape
    f = w1.shape[1]
    grid = (t // tm, f // tn)
    side_in_spec, side_out_spec, side_shape, side_bytes = _side_cast_specs(side, *grid)
    return pl.pallas_call(
        _glu_up_kernel,
        grid=grid,
        in_specs=[pl.BlockSpec((tm, d), lambda i, j: (i, 0), pipeline_mode=pl.Buffered(1)),
                  pl.BlockSpec((d, tn), lambda i, j: (0, j)),
                  pl.BlockSpec((d, tn), lambda i, j: (0, j)),
                  side_in_spec],
        out_specs=[pl.BlockSpec((tm, tn), lambda i, j: (i, j)), side_out_spec],
        out_shape=[jax.ShapeDtypeStruct((t, f), BF16), side_shape],
        compiler_params=pltpu.CompilerParams(
            dimension_semantics=("parallel", "arbitrary"),
            vmem_limit_bytes=_vmem_limit(
                2 * _nbytes((d, tn), w1.dtype) + _nbytes((tm, tn), BF16) + side_bytes,
                _nbytes((tm, d), BF16) + 4 * _nbytes((tm, tn), F32) + 2 * _nbytes((d, tn), BF16))),
        name="glu_up",
    )(h, w1, w3, side)


def _matmul_kernel(a_ref, w_ref, *rest):
    maybe_side_ref, o_ref, maybe_side_o_ref = rest if len(rest) == 3 else (None, rest[0], None)
    o_ref[...] = _dot(a_ref[...], w_ref[...]).astype(o_ref.dtype)
    if maybe_side_ref is not None:
        _side_cast(maybe_side_ref, maybe_side_o_ref)


def matmul_kres(a, w, side=None, *, side_tile_cols=None, tm=512, tn=1024):
    t, kdim = a.shape
    n = w.shape[1]
    grid = (n // tn, t // tm)
    in_specs = [pl.BlockSpec((tm, kdim), lambda j, i: (i, 0)),
                pl.BlockSpec((kdim, tn), lambda j, i: (0, j), pipeline_mode=pl.Buffered(1))]
    out_specs = [pl.BlockSpec((tm, tn), lambda j, i: (i, j))]
    out_shape = [jax.ShapeDtypeStruct((t, n), BF16)]
    operands = [a, w]
    side_bytes = 0
    if side is not None:
        side_in_spec, side_out_spec, side_shape, side_bytes = _side_cast_specs(side, *grid, tile_cols=side_tile_cols)
        in_specs.append(side_in_spec)
        out_specs.append(side_out_spec)
        out_shape.append(side_shape)
        operands.append(side)
    outs = pl.pallas_call(
        _matmul_kernel,
        grid=grid,
        in_specs=in_specs,
        out_specs=out_specs,
        out_shape=out_shape,
        compiler_params=pltpu.CompilerParams(
            dimension_semantics=("parallel", "arbitrary"),
            vmem_limit_bytes=_vmem_limit(
                _nbytes((tm, kdim), BF16) + _nbytes((tm, tn), BF16) + side_bytes,
                _nbytes((kdim, tn), BF16) + 2 * _nbytes((tm, tn), F32))),
        name="matmul_kres",
    )(*operands)
    return outs if side is not None else outs[0]


def _resnorm_kernel(y_ref, x_ref, gpost_ref, gnext_ref, xo_ref, *maybe_h_ref, scale):
    y = y_ref[...].astype(F32)
    inv = lax.rsqrt(jnp.mean(y * y, axis=-1, keepdims=True) + NORM_EPS)
    xn = x_ref[...] + scale * (y * inv * gpost_ref[...])
    xo_ref[...] = xn
    if maybe_h_ref:
        inv2 = lax.rsqrt(jnp.mean(xn * xn, axis=-1, keepdims=True) + NORM_EPS)
        maybe_h_ref[0][...] = (xn * inv2 * gnext_ref[...]).astype(BF16)


def resnorm(y, x, g_post, g_next, *, scale, emit_next, tm=256):
    t, d = x.shape
    row = pl.BlockSpec((tm, d), lambda i: (i, 0))
    vec = pl.BlockSpec((1, d), lambda i: (0, 0))
    out_shape = [jax.ShapeDtypeStruct((t, d), F32)]
    if emit_next:
        out_shape.append(jax.ShapeDtypeStruct((t, d), BF16))
    pipelined = (_nbytes((tm, d), y.dtype) + 2 * _nbytes((tm, d), F32)
                 + (_nbytes((tm, d), BF16) if emit_next else 0))
    outs = pl.pallas_call(
        functools.partial(_resnorm_kernel, scale=scale),
        grid=(t // tm,),
        in_specs=[row, row, vec, vec],
        out_specs=[row] * len(out_shape),
        out_shape=out_shape,
        compiler_params=pltpu.CompilerParams(
            dimension_semantics=("parallel",),
            vmem_limit_bytes=_vmem_limit(pipelined, 3 * _nbytes((tm, d), F32))),
        name="resnorm",
    )(y, x, g_post.reshape(1, d), g_next.reshape(1, d))
    return outs if emit_next else (outs[0], None)


def _proj_kernel(*refs, n_seg, n_aux, epilogue):
    h_ref = refs[0]
    w_refs = refs[1:1 + n_seg]
    aux_refs = refs[1 + n_seg:1 + n_seg + n_aux]
    side_ref = refs[1 + n_seg + n_aux]
    out_refs = refs[2 + n_seg + n_aux:-1]
    side_o_ref = refs[-1]
    h = h_ref[...]
    ys = [_dot(h, w_ref[...]) for w_ref in w_refs]
    outs = epilogue(ys, [r[...] for r in aux_refs])
    for o_ref, o in zip(out_refs, outs):
        o_ref[...] = o.astype(o_ref.dtype)
    _side_cast(side_ref, side_o_ref)


def proj(h, w_tiles, col_starts, width, aux, epilogue, out_dtypes, side, *, tm=1024):
    t, d = h.shape
    tn = w_tiles.shape[2]
    n_seg = len(col_starts)
    grid = (t // tm, width // tn)
    side_in_spec, side_out_spec, side_shape, side_bytes = _side_cast_specs(side, *grid)

    def w_spec(start):
        off = start // tn
        return pl.BlockSpec((None, d, tn), lambda i, j: (off + j, 0, 0))

    pipelined = (_nbytes((tm, d), BF16) + n_seg * _nbytes((d, tn), BF16)
                 + sum(_nbytes((tm, tn), dt) for dt in out_dtypes) + side_bytes)
    return pl.pallas_call(
        functools.partial(_proj_kernel, n_seg=n_seg, n_aux=len(aux), epilogue=epilogue),
        grid=grid,
        in_specs=([pl.BlockSpec((tm, d), lambda i, j: (i, 0))]
                  + [w_spec(s) for s in col_starts]
                  + [pl.BlockSpec((1, tn), lambda i, j: (0, j)) for _ in aux]
                  + [side_in_spec]),
        out_specs=[pl.BlockSpec((tm, tn), lambda i, j: (i, j)) for _ in out_dtypes] + [side_out_spec],
        out_shape=[jax.ShapeDtypeStruct((t, width), dt) for dt in out_dtypes] + [side_shape],
        compiler_params=pltpu.CompilerParams(
            dimension_semantics=("parallel", "arbitrary"),
            vmem_limit_bytes=_vmem_limit(pipelined, (n_seg + 4) * _nbytes((tm, tn), F32))),
        name="proj",
    )(h, *([w_tiles] * n_seg), *[a.reshape(1, width) for a in aux], side)


def _hgrn_epilogue(ys, aux):
    q, fr, i, g = ys
    lb, = aux
    f = lb + (1.0 - lb) * jax.nn.sigmoid(fr)
    return _silu(q), jnp.log(f) * LOG2_E, 1.0 - f, i, _silu(g)


def _attn_epilogue(ys, aux):
    q, k, v = ys
    return q * (HEAD_DIM ** -0.5), k, v


def _gate_epilogue(ys, aux):
    ga, gb = ys
    ba, bb = aux
    return jax.nn.sigmoid(ga + ba), jax.nn.sigmoid(gb + bb)


def _lower_half_total(p_hb, level, row8):
    c, w = p_hb.shape
    hb = 1 << level
    if 2 * hb >= F32_SUBLANES:
        blk = min(2 * hb, c)
        return jnp.concatenate(
            [jnp.broadcast_to(p_hb[b * blk + hb - 1:b * blk + hb, :], (blk, w)) for b in range(c // blk)], axis=0)
    x = p_hb.reshape(c // F32_SUBLANES, F32_SUBLANES, w)
    odd = (row8 & 1) == 1
    if level == 0:
        g = jnp.where(odd, pltpu.roll(x, 1, 1), x)
    else:
        z = jnp.where(odd, x, pltpu.roll(x, 7, 1))
        g = jnp.where((row8 & 2) == 0, z, pltpu.roll(z, 2, 1))
    return g.reshape(c, w)


def _add_to_upper_halves(p_hb, g, level, upper_mask):
    c = p_hb.shape[0]
    hb = 1 << level
    if hb < F32_SUBLANES:
        return p_hb + jnp.where(upper_mask, g, 0.0)
    pieces = []
    for lo in range(0, c, 2 * hb):
        pieces += [p_hb[lo:lo + hb], p_hb[lo + hb:lo + 2 * hb] + g[lo + hb:lo + 2 * hb]]
    return jnp.concatenate(pieces, axis=0)


def _hgrn_kernel(qs_ref, lf_ref, kk_ref, v_ref, gs_ref, ng_ref, o_ref, st_ref, *, heads_per_block,
                 chunks_per_iter):
    c = CHUNK
    dk = HEAD_DIM
    w = qs_ref.shape[1]
    n_levels = CHUNK.bit_length() - 1

    @pl.when(pl.program_id(2) == 0)
    def _():
        st_ref[...] = jnp.zeros_like(st_ref)

    row = lax.broadcasted_iota(jnp.int32, (c, w), 0)
    row8 = lax.broadcasted_iota(jnp.int32, (c // F32_SUBLANES, F32_SUBLANES, w), 1)
    upper = [((row >> p) & 1) == 1 for p in range(n_levels)]
    ti = lax.broadcasted_iota(jnp.int32, (c, c), 0)
    si = lax.broadcasted_iota(jnp.int32, (c, c), 1)
    diag_mask = ti == si
    level_masks = [((ti >> (p + 1)) == (si >> (p + 1))) & (((ti >> p) & 1) == 1) & (((si >> p) & 1) == 0)
                   for p in range(n_levels)]

    def iter_body(it, carry):
        pre = []
        for cc in range(chunks_per_iter):
            rows = pl.ds(pl.multiple_of((it * chunks_per_iter + cc) * c, c), c)
            p_hb = lf_ref[rows, :]
            e_q, e_k = [], []
            for p in range(n_levels + 1):
                g = _lower_half_total(p_hb, p, row8)
                e_q.append(jnp.exp2(p_hb))
                e_k.append(None if p == 0 else jnp.exp2(g - p_hb))
                if p < n_levels:
                    p_hb = _add_to_upper_halves(p_hb, g, p, upper[p])
            pre.append((rows, e_q, e_k))
        indep = {}
        for cc, (rows, e_q, e_k) in enumerate(pre):
            for hh in range(heads_per_block):
                cols = slice(hh * dk, (hh + 1) * dk)
                qb = qs_ref[rows, cols]
                kb = kk_ref[rows, cols]
                v = v_ref[rows, cols]
                q = qb.astype(F32)
                k = kb.astype(F32)
                parts = [_dot_nt(qb, kb)]
                for p in range(n_levels):
                    qt = (q * e_q[p][:, cols]).astype(BF16)
                    kt = kb if p == 0 else (k * e_k[p][:, cols]).astype(BF16)
                    parts.append(_dot_nt(qt, kt))
                e_in = e_q[n_levels][:, cols]
                e_out = e_k[n_levels][:, cols]
                kv = _dot_tn(v, (k * e_out).astype(BF16))
                indep[cc, hh] = (parts, kv, (q * e_in).astype(BF16), v, e_in[c - 1:c, :])
        for hh in range(heads_per_block):
            cols = slice(hh * dk, (hh + 1) * dk)
            st = st_ref[hh]
            for cc, (rows, _, _) in enumerate(pre):
                parts, kv, q_in, v, decay_all = indep[cc, hh]
                scores = jnp.where(diag_mask, parts[0], 0.0)
                for p in range(n_levels):
                    scores = jnp.where(level_masks[p], parts[p + 1], scores)
                o = _dot(scores.astype(BF16), v) + _dot_nt(q_in, st.astype(BF16))
                st = st * decay_all + kv
                inv = lax.rsqrt(jnp.mean(o * o, axis=-1, keepdims=True) + NORM_EPS)
                o = o * inv * ng_ref[:, cols] * gs_ref[rows, cols].astype(F32)
                o_ref[rows, cols] = o.astype(o_ref.dtype)
            st_ref[hh] = st
        return carry

    lax.fori_loop(0, qs_ref.shape[0] // (c * chunks_per_iter), iter_body, 0)


def hgrn(qs, lf, kk, v, gs, norm_g, *, batch, seq, block_len=512, heads_per_block=4, chunks_per_iter=2):
    t, width = qs.shape
    bw = heads_per_block * HEAD_DIM
    n_l = seq // block_len
    tok = pl.BlockSpec((block_len, bw), lambda b, h, l: (b * n_l + l, h))
    pipelined = 4 * _nbytes((block_len, bw), BF16) + _nbytes((block_len, bw), F32)
    return pl.pallas_call(
        functools.partial(_hgrn_kernel, heads_per_block=heads_per_block, chunks_per_iter=chunks_per_iter),
        grid=(batch, width // bw, n_l),
        in_specs=[tok, tok, tok, tok, tok,
                  pl.BlockSpec((1, bw), lambda b, h, l: (0, h))],
        out_specs=tok,
        out_shape=jax.ShapeDtypeStruct((t, width), BF16),
        scratch_shapes=[pltpu.VMEM((heads_per_block, HEAD_DIM, HEAD_DIM), F32)],
        compiler_params=pltpu.CompilerParams(
            dimension_semantics=("parallel", "parallel", "arbitrary"),
            vmem_limit_bytes=_vmem_limit(pipelined, 0)),
        name="hgrn",
    )(qs, lf, kk, v, gs, norm_g.reshape(1, width))


ATTN_Q_BLOCK = 2 * CHUNK
ATTN_WINDOW = 640
ATTN_LEAD = ATTN_WINDOW - ATTN_Q_BLOCK
ATTN_BIAS_LANES = pl.cdiv(ATTN_WINDOW + ATTN_Q_BLOCK - 1, LANES) * LANES


def _attn_bias_row(rel_bias):
    m = np.arange(ATTN_BIAS_LANES)
    m = np.where(m >= ATTN_WINDOW, m - ATTN_BIAS_LANES, m)
    idx = np.clip(ATTN_LEAD - m, -MAX_REL, MAX_REL) + MAX_REL
    runs, start = [], 0
    for pos in range(1, len(idx) + 1):
        if pos == len(idx) or idx[pos] != idx[pos - 1] - 1:
            runs.append((start, pos))
            start = pos
    pieces = [jnp.flip(rel_bias[:, int(idx[hi - 1]):int(idx[lo]) + 1], axis=1) for lo, hi in runs]
    return jnp.concatenate(pieces, axis=1)


def _attn_kernel(q_ref, k_ref, v_ref, brow_ref, o_ref, kpad_ref, vpad_ref, *, group):
    seq = q_ref.shape[0]
    kpad_ref[0:ATTN_LEAD, :] = jnp.zeros((ATTN_LEAD, HEAD_DIM), kpad_ref.dtype)
    vpad_ref[0:ATTN_LEAD, :] = jnp.zeros((ATTN_LEAD, HEAD_DIM), vpad_ref.dtype)
    kpad_ref[ATTN_LEAD:, :] = k_ref[...]
    vpad_ref[ATTN_LEAD:, :] = v_ref[...]
    brow = jnp.broadcast_to(brow_ref[0], (ATTN_Q_BLOCK, ATTN_BIAS_LANES))
    toep = pltpu.roll(brow, 0, 1, stride=1, stride_axis=0)[:, :ATTN_WINDOW]
    qi = lax.broadcasted_iota(jnp.int32, (ATTN_Q_BLOCK, ATTN_WINDOW), 0)
    col = lax.broadcasted_iota(jnp.int32, (ATTN_Q_BLOCK, ATTN_WINDOW), 1)
    qc = qi // CHUNK
    kc = col // CHUNK
    lead = ATTN_LEAD // CHUNK
    bias = jnp.where((kc >= qc + lead - LEFT_CHUNKS) & (kc <= qc + lead), toep, MASK_VALUE)

    def do_group(base, masked):
        starts = [base + u * ATTN_Q_BLOCK for u in range(group)]
        scores = []
        for start in starts:
            q = q_ref[pl.ds(start, ATTN_Q_BLOCK), :]
            kw = kpad_ref[pl.ds(start, ATTN_WINDOW), :]
            s = _dot_nt(q, kw) + bias
            if masked:
                s = jnp.where(col >= ATTN_LEAD - start, s, MASK_VALUE)
            scores.append(s)
        probs = []
        for s in scores:
            m = jnp.max(s, axis=-1, keepdims=True)
            p = jnp.exp(s - m)
            probs.append((p.astype(BF16), jnp.sum(p, axis=-1, keepdims=True)))
        for start, (p, denom) in zip(starts, probs):
            vw = vpad_ref[pl.ds(start, ATTN_WINDOW), :]
            o = _dot(p, vw) / denom
            o_ref[pl.ds(start, ATTN_Q_BLOCK), :] = o.astype(o_ref.dtype)

    span = group * ATTN_Q_BLOCK
    n_masked = pl.cdiv(ATTN_LEAD, span)
    for g in range(n_masked):
        do_group(g * span, True)

    def body(g, carry):
        do_group(pl.multiple_of(g * span, span), False)
        return carry

    lax.fori_loop(n_masked, seq // span, body, 0)


def band_attn(q, k, v, bias_row, *, batch, seq, group=8):
    t, width = q.shape
    tok = pl.BlockSpec((seq, HEAD_DIM), lambda b, h: (b, h))
    return pl.pallas_call(
        functools.partial(_attn_kernel, group=group),
        grid=(batch, width // HEAD_DIM),
        in_specs=[tok, tok, tok,
                  pl.BlockSpec((1, 1, ATTN_BIAS_LANES), lambda b, h: (h, 0, 0))],
        out_specs=tok,
        out_shape=jax.ShapeDtypeStruct((t, width), BF16),
        scratch_shapes=[pltpu.VMEM((seq + ATTN_LEAD, HEAD_DIM), BF16),
                        pltpu.VMEM((seq + ATTN_LEAD, HEAD_DIM), BF16)],
        compiler_params=pltpu.CompilerParams(
            dimension_semantics=("parallel", "parallel"),
            vmem_limit_bytes=_vmem_limit(
                4 * _nbytes((seq, HEAD_DIM), BF16),
                2 * _nbytes((seq + ATTN_LEAD, HEAD_DIM), BF16))),
        name="band_attn",
    )(q, k, v, bias_row.reshape(bias_row.shape[0], 1, ATTN_BIAS_LANES))


def _mix_up_kernel(oa_ref, ob_ref, wa_ref, wb_ref, ga_ref, gb_ref, o_ref):
    ya = _dot(oa_ref[...], wa_ref[...])
    yb = _dot(ob_ref[...], wb_ref[...])
    o_ref[...] = (ga_ref[...].astype(F32) * ya + gb_ref[...].astype(F32) * yb).astype(o_ref.dtype)


def mix_up(oa, ob, wa, wb, ga, gb, *, tm=1024, tn=512):
    t, kdim = oa.shape
    d = wa.shape[1]
    pipelined = (2 * _nbytes((tm, kdim), BF16) + 2 * _nbytes((kdim, tn), BF16) + 3 * _nbytes((tm, tn), BF16))
    return pl.pallas_call(
        _mix_up_kernel,
        grid=(t // tm, d // tn),
        in_specs=[pl.BlockSpec((tm, kdim), lambda i, j: (i, 0)),
                  pl.BlockSpec((tm, kdim), lambda i, j: (i, 0)),
                  pl.BlockSpec((kdim, tn), lambda i, j: (0, j)),
                  pl.BlockSpec((kdim, tn), lambda i, j: (0, j)),
                  pl.BlockSpec((tm, tn), lambda i, j: (i, j)),
                  pl.BlockSpec((tm, tn), lambda i, j: (i, j))],
        out_specs=pl.BlockSpec((tm, tn), lambda i, j: (i, j)),
        out_shape=jax.ShapeDtypeStruct((t, d), BF16),
        compiler_params=pltpu.CompilerParams(
            dimension_semantics=("parallel", "arbitrary"),
            vmem_limit_bytes=_vmem_limit(pipelined, 4 * _nbytes((tm, tn), F32))),
        name="mix_up",
    )(oa, ob, wa, wb, ga, gb)


def _ffn(x, h, w1, w3, w2, post_g, next_g, emit_next, side=None, side_tile_cols=None):
    g, w2_b = glu_up(h, w1, w3, w2)
    if side is None:
        y, side_b = matmul_kres(g, w2_b), None
    else:
        y, side_b = matmul_kres(g, w2_b, side, side_tile_cols=side_tile_cols)
    x_new, h_next = resnorm(y, x, post_g, next_g, scale=MACARON_WEIGHT, emit_next=emit_next)
    return x_new, h_next, side_b


def kernel(x, ffn1_pre_g, ffn1_post_g, ffn1_w1, ffn1_w3, ffn1_w2, mix_pre_g, mix_post_g, w_in, b_gate,
           hgrn_lb_logits, hgrn_norm_g, rel_bias, w_up_a, w_up_b, w_out,
           ffn2_pre_g, ffn2_post_g, ffn2_w1, ffn2_w3, ffn2_w2):
    batch, seq, d = x.shape
    depth = ffn1_w1.shape[0]
    d_half = d // 2
    lower_bounds = jnp.cumsum(jax.nn.softmax(hgrn_lb_logits.astype(F32), axis=0), axis=0)
    xt = x.reshape(batch * seq, d)
    h = norm_cast(xt, ffn1_pre_g[0])
    for layer in range(depth):
        xt, h, w_in_b = _ffn(xt, h, ffn1_w1[layer], ffn1_w3[layer], ffn1_w2[layer],
                             ffn1_post_g[layer], mix_pre_g[layer], True, side=w_in[layer],
                             side_tile_cols=PROJ_TILE_COLS)

        qs, lf, kk, vv, gs, w_up_a_b = proj(h, w_in_b, [0, d_half, 2 * d_half, 3 * d_half], d_half,
                                            [lower_bounds[layer]], _hgrn_epilogue,
                                            [BF16, F32, BF16, BF16, BF16], w_up_a[layer])
        qb, kb, vb, w_up_b_b = proj(h, w_in_b, [4 * d_half, 5 * d_half, 6 * d_half], d_half,
                                    [], _attn_epilogue, [BF16, BF16, BF16], w_up_b[layer])
        ga, gb, w_out_b = proj(h, w_in_b, [7 * d_half, 7 * d_half + d], d,
                               [b_gate[layer, 0], b_gate[layer, 1]], _gate_epilogue, [BF16, BF16],
                               w_out[layer])

        oa = hgrn(qs, lf, kk, vv, gs, hgrn_norm_g[layer], batch=batch, seq=seq)
        ob = band_attn(qb, kb, vb, _attn_bias_row(rel_bias[layer].astype(F32)), batch=batch, seq=seq)

        m = mix_up(oa, ob, w_up_a_b, w_up_b_b, ga, gb)
        last = layer == depth - 1
        y = matmul_kres(m, w_out_b, tm=1024)
        xt, h = resnorm(y, xt, mix_post_g[layer], ffn2_pre_g[layer], scale=1.0, emit_next=True)
        next_pre = ffn1_pre_g[layer + 1] if not last else ffn2_pre_g[layer]
        xt, h, _ = _ffn(xt, h, ffn2_w1[layer], ffn2_w3[layer], ffn2_w2[layer],
                        ffn2_post_g[layer], next_pre, not last)
    return xt.reshape(batch, seq, d)
```

```python
import functools

import numpy as np
import jax
import jax.numpy as jnp
from jax import lax
from jax.experimental import pallas as pl
from jax.experimental.pallas import tpu as pltpu

F32 = jnp.float32
BF16 = jnp.bfloat16

NORM_EPS = 1e-6
LOG2_E = 1.4426950408889634
MACARON_WEIGHT = 0.5
CHUNK = 64
HEAD_DIM = 128
LANES = 128
F32_SUBLANES = 8
BF16_SUBLANES = 16
LEFT_CHUNKS = 8
MAX_REL = 256
MASK_VALUE = -1e30

MIB = 1024 * 1024
V7X_VMEM_BYTES = 64 * MIB
VMEM_LIMIT_CAP = V7X_VMEM_BYTES - 6 * MIB
VMEM_LIMIT_FLOOR = 32 * MIB
VMEM_SPILL_BYTES = 8 * MIB


def _vmem_limit(pipelined_bytes, resident_bytes=0):
    est = 2 * pipelined_bytes + resident_bytes + VMEM_SPILL_BYTES
    return int(min(max(est, VMEM_LIMIT_FLOOR), VMEM_LIMIT_CAP))


def _nbytes(shape, dtype):
    return int(np.prod(shape)) * jnp.dtype(dtype).itemsize


def _dot(a, b):
    return jnp.dot(a, b, preferred_element_type=F32)


def _dot_nt(a, b):
    return lax.dot_general(a, b, (((1,), (1,)), ((), ())), preferred_element_type=F32)


def _dot_tn(a, b):
    return lax.dot_general(a, b, (((0,), (0,)), ((), ())), preferred_element_type=F32)


def _silu(x):
    return x * jax.nn.sigmoid(x)


def _side_cast_specs(side, n_i, n_j):
    rows, cols = side.shape
    slab, rem = divmod(rows, n_i * n_j)
    assert rem == 0 and slab % BF16_SUBLANES == 0, (rows, n_i, n_j)
    spec = pl.BlockSpec((slab, cols), lambda i, j: (i * n_j + j, 0))
    return spec, jax.ShapeDtypeStruct((rows, cols), BF16), _nbytes((slab, cols), F32) + _nbytes((slab, cols), BF16)


def _norm_cast_kernel(x_ref, g_ref, o_ref):
    x = x_ref[...]
    inv = lax.rsqrt(jnp.mean(x * x, axis=-1, keepdims=True) + NORM_EPS)
    o_ref[...] = (x * inv * g_ref[...]).astype(o_ref.dtype)


def norm_cast(x, g, *, tm=256):
    t, d = x.shape
    return pl.pallas_call(
        _norm_cast_kernel,
        grid=(t // tm,),
        in_specs=[pl.BlockSpec((tm, d), lambda i: (i, 0)),
                  pl.BlockSpec((1, d), lambda i: (0, 0))],
        out_specs=pl.BlockSpec((tm, d), lambda i: (i, 0)),
        out_shape=jax.ShapeDtypeStruct((t, d), BF16),
        compiler_params=pltpu.CompilerParams(
            dimension_semantics=("parallel",),
            vmem_limit_bytes=_vmem_limit(_nbytes((tm, d), F32) + _nbytes((tm, d), BF16),
                                         3 * _nbytes((tm, d), F32))),
        name="norm_cast",
    )(x, g.reshape(1, d))


def _glu_up_kernel(h_ref, w1_ref, w3_ref, side_ref, o_ref, side_o_ref):
    h = h_ref[...]
    a = _dot(h, w1_ref[...].astype(BF16))
    b = _dot(h, w3_ref[...].astype(BF16))
    o_ref[...] = (_silu(a) * b).astype(o_ref.dtype)
    side_o_ref[...] = side_ref[...].astype(side_o_ref.dtype)


def glu_up(h, w1, w3, side, *, tm=2048, tn=256):
    t, d = h.shape
    f = w1.shape[1]
    grid = (t // tm, f // tn)
    side_spec, side_shape, side_bytes = _side_cast_specs(side, *grid)
    return pl.pallas_call(
        _glu_up_kernel,
        grid=grid,
        in_specs=[pl.BlockSpec((tm, d), lambda i, j: (i, 0), pipeline_mode=pl.Buffered(1)),
                  pl.BlockSpec((d, tn), lambda i, j: (0, j)),
                  pl.BlockSpec((d, tn), lambda i, j: (0, j)),
                  side_spec],
        out_specs=[pl.BlockSpec((tm, tn), lambda i, j: (i, j)), side_spec],
        out_shape=[jax.ShapeDtypeStruct((t, f), BF16), side_shape],
        compiler_params=pltpu.CompilerParams(
            dimension_semantics=("parallel", "arbitrary"),
            vmem_limit_bytes=_vmem_limit(
                2 * _nbytes((d, tn), w1.dtype) + _nbytes((tm, tn), BF16) + side_bytes,
                _nbytes((tm, d), BF16) + 4 * _nbytes((tm, tn), F32) + 2 * _nbytes((d, tn), BF16))),
        name="glu_up",
    )(h, w1, w3, side)


def _matmul_kernel(a_ref, w_ref, *rest):
    maybe_side_ref, o_ref, maybe_side_o_ref = rest if len(rest) == 3 else (None, rest[0], None)
    o_ref[...] = _dot(a_ref[...], w_ref[...]).astype(o_ref.dtype)
    if maybe_side_ref is not None:
        maybe_side_o_ref[...] = maybe_side_ref[...].astype(maybe_side_o_ref.dtype)


def matmul_kres(a, w, side=None, *, tm=512, tn=1024):
    t, kdim = a.shape
    n = w.shape[1]
    grid = (n // tn, t // tm)
    in_specs = [pl.BlockSpec((tm, kdim), lambda j, i: (i, 0)),
                pl.BlockSpec((kdim, tn), lambda j, i: (0, j), pipeline_mode=pl.Buffered(1))]
    out_specs = [pl.BlockSpec((tm, tn), lambda j, i: (i, j))]
    out_shape = [jax.ShapeDtypeStruct((t, n), BF16)]
    operands = [a, w]
    side_bytes = 0
    if side is not None:
        side_spec, side_shape, side_bytes = _side_cast_specs(side, *grid)
        in_specs.append(side_spec)
        out_specs.append(side_spec)
        out_shape.append(side_shape)
        operands.append(side)
    outs = pl.pallas_call(
        _matmul_kernel,
        grid=grid,
        in_specs=in_specs,
        out_specs=out_specs,
        out_shape=out_shape,
        compiler_params=pltpu.CompilerParams(
            dimension_semantics=("parallel", "arbitrary"),
            vmem_limit_bytes=_vmem_limit(
                _nbytes((tm, kdim), BF16) + _nbytes((tm, tn), BF16) + side_bytes,
                _nbytes((kdim, tn), BF16) + 2 * _nbytes((tm, tn), F32))),
        name="matmul_kres",
    )(*operands)
    return outs if side is not None else outs[0]


def _resnorm_kernel(y_ref, x_ref, gpost_ref, gnext_ref, xo_ref, *maybe_h_ref, scale):
    y = y_ref[...].astype(F32)
    inv = lax.rsqrt(jnp.mean(y * y, axis=-1, keepdims=True) + NORM_EPS)
    xn = x_ref[...] + scale * (y * inv * gpost_ref[...])
    xo_ref[...] = xn
    if maybe_h_ref:
        inv2 = lax.rsqrt(jnp.mean(xn * xn, axis=-1, keepdims=True) + NORM_EPS)
        maybe_h_ref[0][...] = (xn * inv2 * gnext_ref[...]).astype(BF16)


def resnorm(y, x, g_post, g_next, *, scale, emit_next, tm=256):
    t, d = x.shape
    row = pl.BlockSpec((tm, d), lambda i: (i, 0))
    vec = pl.BlockSpec((1, d), lambda i: (0, 0))
    out_shape = [jax.ShapeDtypeStruct((t, d), F32)]
    if emit_next:
        out_shape.append(jax.ShapeDtypeStruct((t, d), BF16))
    pipelined = (_nbytes((tm, d), y.dtype) + 2 * _nbytes((tm, d), F32)
                 + (_nbytes((tm, d), BF16) if emit_next else 0))
    outs = pl.pallas_call(
        functools.partial(_resnorm_kernel, scale=scale),
        grid=(t // tm,),
        in_specs=[row, row, vec, vec],
        out_specs=[row] * len(out_shape),
        out_shape=out_shape,
        compiler_params=pltpu.CompilerParams(
            dimension_semantics=("parallel",),
            vmem_limit_bytes=_vmem_limit(pipelined, 3 * _nbytes((tm, d), F32))),
        name="resnorm",
    )(y, x, g_post.reshape(1, d), g_next.reshape(1, d))
    return outs if emit_next else (outs[0], None)


def _proj_kernel(*refs, n_seg, n_aux, epilogue):
    h_ref = refs[0]
    w_refs = refs[1:1 + n_seg]
    aux_refs = refs[1 + n_seg:1 + n_seg + n_aux]
    side_ref = refs[1 + n_seg + n_aux]
    out_refs = refs[2 + n_seg + n_aux:-1]
    side_o_ref = refs[-1]
    h = h_ref[...]
    ys = [_dot(h, w_ref[...]) for w_ref in w_refs]
    outs = epilogue(ys, [r[...] for r in aux_refs])
    for o_ref, o in zip(out_refs, outs):
        o_ref[...] = o.astype(o_ref.dtype)
    side_o_ref[...] = side_ref[...].astype(side_o_ref.dtype)


def proj(h, w_in, col_starts, width, aux, epilogue, out_dtypes, side, *, tm=1024, tn=256):
    t, d = h.shape
    n_seg = len(col_starts)
    grid = (t // tm, width // tn)
    side_spec, side_shape, side_bytes = _side_cast_specs(side, *grid)

    def w_spec(start):
        off = start // tn
        return pl.BlockSpec((d, tn), lambda i, j: (0, off + j))

    pipelined = (_nbytes((tm, d), BF16) + n_seg * _nbytes((d, tn), BF16)
                 + sum(_nbytes((tm, tn), dt) for dt in out_dtypes) + side_bytes)
    return pl.pallas_call(
        functools.partial(_proj_kernel, n_seg=n_seg, n_aux=len(aux), epilogue=epilogue),
        grid=grid,
        in_specs=([pl.BlockSpec((tm, d), lambda i, j: (i, 0))]
                  + [w_spec(s) for s in col_starts]
                  + [pl.BlockSpec((1, tn), lambda i, j: (0, j)) for _ in aux]
                  + [side_spec]),
        out_specs=[pl.BlockSpec((tm, tn), lambda i, j: (i, j)) for _ in out_dtypes] + [side_spec],
        out_shape=[jax.ShapeDtypeStruct((t, width), dt) for dt in out_dtypes] + [side_shape],
        compiler_params=pltpu.CompilerParams(
            dimension_semantics=("parallel", "arbitrary"),
            vmem_limit_bytes=_vmem_limit(pipelined, (n_seg + 4) * _nbytes((tm, tn), F32))),
        name="proj",
    )(h, *([w_in] * n_seg), *[a.reshape(1, width) for a in aux], side)


def _hgrn_epilogue(ys, aux):
    q, fr, i, g = ys
    lb, = aux
    f = lb + (1.0 - lb) * jax.nn.sigmoid(fr)
    return _silu(q), jnp.log(f) * LOG2_E, 1.0 - f, i, _silu(g)


def _attn_epilogue(ys, aux):
    q, k, v = ys
    return q * (HEAD_DIM ** -0.5), k, v


def _gate_epilogue(ys, aux):
    ga, gb = ys
    ba, bb = aux
    return jax.nn.sigmoid(ga + ba), jax.nn.sigmoid(gb + bb)


def _lower_half_total(p_hb, level, row8):
    c, w = p_hb.shape
    hb = 1 << level
    if 2 * hb >= F32_SUBLANES:
        blk = min(2 * hb, c)
        return jnp.concatenate(
            [jnp.broadcast_to(p_hb[b * blk + hb - 1:b * blk + hb, :], (blk, w)) for b in range(c // blk)], axis=0)
    x = p_hb.reshape(c // F32_SUBLANES, F32_SUBLANES, w)
    odd = (row8 & 1) == 1
    if level == 0:
        g = jnp.where(odd, pltpu.roll(x, 1, 1), x)
    else:
        z = jnp.where(odd, x, pltpu.roll(x, 7, 1))
        g = jnp.where((row8 & 2) == 0, z, pltpu.roll(z, 2, 1))
    return g.reshape(c, w)


def _add_to_upper_halves(p_hb, g, level, upper_mask):
    c = p_hb.shape[0]
    hb = 1 << level
    if hb < F32_SUBLANES:
        return p_hb + jnp.where(upper_mask, g, 0.0)
    pieces = []
    for lo in range(0, c, 2 * hb):
        pieces += [p_hb[lo:lo + hb], p_hb[lo + hb:lo + 2 * hb] + g[lo + hb:lo + 2 * hb]]
    return jnp.concatenate(pieces, axis=0)


def _hgrn_kernel(qs_ref, lf_ref, kk_ref, v_ref, gs_ref, ng_ref, o_ref, st_ref, *, heads_per_block,
                 chunks_per_iter):
    c = CHUNK
    dk = HEAD_DIM
    w = qs_ref.shape[1]
    n_levels = CHUNK.bit_length() - 1

    @pl.when(pl.program_id(2) == 0)
    def _():
        st_ref[...] = jnp.zeros_like(st_ref)

    row = lax.broadcasted_iota(jnp.int32, (c, w), 0)
    row8 = lax.broadcasted_iota(jnp.int32, (c // F32_SUBLANES, F32_SUBLANES, w), 1)
    upper = [((row >> p) & 1) == 1 for p in range(n_levels)]
    ti = lax.broadcasted_iota(jnp.int32, (c, c), 0)
    si = lax.broadcasted_iota(jnp.int32, (c, c), 1)
    diag_mask = ti == si
    level_masks = [((ti >> (p + 1)) == (si >> (p + 1))) & (((ti >> p) & 1) == 1) & (((si >> p) & 1) == 0)
                   for p in range(n_levels)]

    def iter_body(it, carry):
        pre = []
        for cc in range(chunks_per_iter):
            rows = pl.ds(pl.multiple_of((it * chunks_per_iter + cc) * c, c), c)
            p_hb = lf_ref[rows, :]
            e_q, e_k = [], []
            for p in range(n_levels + 1):
                g = _lower_half_total(p_hb, p, row8)
                e_q.append(jnp.exp2(p_hb))
                e_k.append(None if p == 0 else jnp.exp2(g - p_hb))
                if p < n_levels:
                    p_hb = _add_to_upper_halves(p_hb, g, p, upper[p])
            pre.append((rows, e_q, e_k))
        indep = {}
        for cc, (rows, e_q, e_k) in enumerate(pre):
            for hh in range(heads_per_block):
                cols = slice(hh * dk, (hh + 1) * dk)
                qb = qs_ref[rows, cols]
                kb = kk_ref[rows, cols]
                v = v_ref[rows, cols]
                q = qb.astype(F32)
                k = kb.astype(F32)
                parts = [_dot_nt(qb, kb)]
                for p in range(n_levels):
                    qt = (q * e_q[p][:, cols]).astype(BF16)
                    kt = kb if p == 0 else (k * e_k[p][:, cols]).astype(BF16)
                    parts.append(_dot_nt(qt, kt))
                e_in = e_q[n_levels][:, cols]
                e_out = e_k[n_levels][:, cols]
                kv = _dot_tn(v, (k * e_out).astype(BF16))
                indep[cc, hh] = (parts, kv, (q * e_in).astype(BF16), v, e_in[c - 1:c, :])
        for hh in range(heads_per_block):
            cols = slice(hh * dk, (hh + 1) * dk)
            st = st_ref[hh]
            for cc, (rows, _, _) in enumerate(pre):
                parts, kv, q_in, v, decay_all = indep[cc, hh]
                scores = jnp.where(diag_mask, parts[0], 0.0)
                for p in range(n_levels):
                    scores = jnp.where(level_masks[p], parts[p + 1], scores)
                o = _dot(scores.astype(BF16), v) + _dot_nt(q_in, st.astype(BF16))
                st = st * decay_all + kv
                inv = lax.rsqrt(jnp.mean(o * o, axis=-1, keepdims=True) + NORM_EPS)
                o = o * inv * ng_ref[:, cols] * gs_ref[rows, cols].astype(F32)
                o_ref[rows, cols] = o.astype(o_ref.dtype)
            st_ref[hh] = st
        return carry

    lax.fori_loop(0, qs_ref.shape[0] // (c * chunks_per_iter), iter_body, 0)


def hgrn(qs, lf, kk, v, gs, norm_g, *, batch, seq, block_len=512, heads_per_block=4, chunks_per_iter=8):
    t, width = qs.shape
    bw = heads_per_block * HEAD_DIM
    n_l = seq // block_len
    tok = pl.BlockSpec((block_len, bw), lambda b, h, l: (b * n_l + l, h))
    pipelined = 4 * _nbytes((block_len, bw), BF16) + _nbytes((block_len, bw), F32)
    return pl.pallas_call(
        functools.partial(_hgrn_kernel, heads_per_block=heads_per_block, chunks_per_iter=chunks_per_iter),
        grid=(batch, width // bw, n_l),
        in_specs=[tok, tok, tok, tok, tok,
                  pl.BlockSpec((1, bw), lambda b, h, l: (0, h))],
        out_specs=tok,
        out_shape=jax.ShapeDtypeStruct((t, width), BF16),
        scratch_shapes=[pltpu.VMEM((heads_per_block, HEAD_DIM, HEAD_DIM), F32)],
        compiler_params=pltpu.CompilerParams(
            dimension_semantics=("parallel", "parallel", "arbitrary"),
            vmem_limit_bytes=_vmem_limit(pipelined, 0)),
        name="hgrn",
    )(qs, lf, kk, v, gs, norm_g.reshape(1, width))


ATTN_Q_BLOCK = 2 * CHUNK
ATTN_WINDOW = 640
ATTN_LEAD = ATTN_WINDOW - ATTN_Q_BLOCK
ATTN_BIAS_LANES = pl.cdiv(ATTN_WINDOW + ATTN_Q_BLOCK - 1, LANES) * LANES


def _attn_bias_row(rel_bias):
    m = np.arange(ATTN_BIAS_LANES)
    m = np.where(m >= ATTN_WINDOW, m - ATTN_BIAS_LANES, m)
    idx = np.clip(ATTN_LEAD - m, -MAX_REL, MAX_REL) + MAX_REL
    runs, start = [], 0
    for pos in range(1, len(idx) + 1):
        if pos == len(idx) or idx[pos] != idx[pos - 1] - 1:
            runs.append((start, pos))
            start = pos
    pieces = [jnp.flip(rel_bias[:, int(idx[hi - 1]):int(idx[lo]) + 1], axis=1) for lo, hi in runs]
    return jnp.concatenate(pieces, axis=1)


def _attn_kernel(q_ref, k_ref, v_ref, brow_ref, o_ref, kpad_ref, vpad_ref, *, group):
    seq = q_ref.shape[0]
    kpad_ref[0:ATTN_LEAD, :] = jnp.zeros((ATTN_LEAD, HEAD_DIM), kpad_ref.dtype)
    vpad_ref[0:ATTN_LEAD, :] = jnp.zeros((ATTN_LEAD, HEAD_DIM), vpad_ref.dtype)
    kpad_ref[ATTN_LEAD:, :] = k_ref[...]
    vpad_ref[ATTN_LEAD:, :] = v_ref[...]
    brow = jnp.broadcast_to(brow_ref[0], (ATTN_Q_BLOCK, ATTN_BIAS_LANES))
    toep = pltpu.roll(brow, 0, 1, stride=1, stride_axis=0)[:, :ATTN_WINDOW]
    qi = lax.broadcasted_iota(jnp.int32, (ATTN_Q_BLOCK, ATTN_WINDOW), 0)
    col = lax.broadcasted_iota(jnp.int32, (ATTN_Q_BLOCK, ATTN_WINDOW), 1)
    qc = qi // CHUNK
    kc = col // CHUNK
    lead = ATTN_LEAD // CHUNK
    bias = jnp.where((kc >= qc + lead - LEFT_CHUNKS) & (kc <= qc + lead), toep, MASK_VALUE)

    def do_group(base, masked):
        starts = [base + u * ATTN_Q_BLOCK for u in range(group)]
        scores = []
        for start in starts:
            q = q_ref[pl.ds(start, ATTN_Q_BLOCK), :]
            kw = kpad_ref[pl.ds(start, ATTN_WINDOW), :]
            s = _dot_nt(q, kw) + bias
            if masked:
                s = jnp.where(col >= ATTN_LEAD - start, s, MASK_VALUE)
            scores.append(s)
        probs = []
        for s in scores:
            m = jnp.max(s, axis=-1, keepdims=True)
            p = jnp.exp(s - m)
            probs.append((p.astype(BF16), jnp.sum(p, axis=-1, keepdims=True)))
        for start, (p, denom) in zip(starts, probs):
            vw = vpad_ref[pl.ds(start, ATTN_WINDOW), :]
            o = _dot(p, vw) / denom
            o_ref[pl.ds(start, ATTN_Q_BLOCK), :] = o.astype(o_ref.dtype)

    span = group * ATTN_Q_BLOCK
    n_masked = pl.cdiv(ATTN_LEAD, span)
    for g in range(n_masked):
        do_group(g * span, True)

    def body(g, carry):
        do_group(pl.multiple_of(g * span, span), False)
        return carry

    lax.fori_loop(n_masked, seq // span, body, 0)


def band_attn(q, k, v, bias_row, *, batch, seq, group=8):
    t, width = q.shape
    tok = pl.BlockSpec((seq, HEAD_DIM), lambda b, h: (b, h))
    return pl.pallas_call(
        functools.partial(_attn_kernel, group=group),
        grid=(batch, width // HEAD_DIM),
        in_specs=[tok, tok, tok,
                  pl.BlockSpec((1, 1, ATTN_BIAS_LANES), lambda b, h: (h, 0, 0))],
        out_specs=tok,
        out_shape=jax.ShapeDtypeStruct((t, width), BF16),
        scratch_shapes=[pltpu.VMEM((seq + ATTN_LEAD, HEAD_DIM), BF16),
                        pltpu.VMEM((seq + ATTN_LEAD, HEAD_DIM), BF16)],
        compiler_params=pltpu.CompilerParams(
            dimension_semantics=("parallel", "parallel"),
            vmem_limit_bytes=_vmem_limit(
                4 * _nbytes((seq, HEAD_DIM), BF16),
                2 * _nbytes((seq + ATTN_LEAD, HEAD_DIM), BF16))),
        name="band_attn",
    )(q, k, v, bias_row.reshape(bias_row.shape[0], 1, ATTN_BIAS_LANES))


def _mix_up_kernel(oa_ref, ob_ref, wa_ref, wb_ref, ga_ref, gb_ref, o_ref):
    ya = _dot(oa_ref[...], wa_ref[...])
    yb = _dot(ob_ref[...], wb_ref[...])
    o_ref[...] = (ga_ref[...].astype(F32) * ya + gb_ref[...].astype(F32) * yb).astype(o_ref.dtype)


def mix_up(oa, ob, wa, wb, ga, gb, *, tm=1024, tn=512):
    t, kdim = oa.shape
    d = wa.shape[1]
    pipelined = (2 * _nbytes((tm, kdim), BF16) + 2 * _nbytes((kdim, tn), BF16) + 3 * _nbytes((tm, tn), BF16))
    return pl.pallas_call(
        _mix_up_kernel,
        grid=(t // tm, d // tn),
        in_specs=[pl.BlockSpec((tm, kdim), lambda i, j: (i, 0)),
                  pl.BlockSpec((tm, kdim), lambda i, j: (i, 0)),
                  pl.BlockSpec((kdim, tn), lambda i, j: (0, j)),
                  pl.BlockSpec((kdim, tn), lambda i, j: (0, j)),
                  pl.BlockSpec((tm, tn), lambda i, j: (i, j)),
                  pl.BlockSpec((tm, tn), lambda i, j: (i, j))],
        out_specs=pl.BlockSpec((tm, tn), lambda i, j: (i, j)),
        out_shape=jax.ShapeDtypeStruct((t, d), BF16),
        compiler_params=pltpu.CompilerParams(
            dimension_semantics=("parallel", "arbitrary"),
            vmem_limit_bytes=_vmem_limit(pipelined, 4 * _nbytes((tm, tn), F32))),
        name="mix_up",
    )(oa, ob, wa, wb, ga, gb)


def _ffn(x, h, w1, w3, w2, post_g, next_g, emit_next, side=None):
    g, w2_b = glu_up(h, w1, w3, w2)
    if side is None:
        y, side_b = matmul_kres(g, w2_b), None
    else:
        y, side_b = matmul_kres(g, w2_b, side)
    x_new, h_next = resnorm(y, x, post_g, next_g, scale=MACARON_WEIGHT, emit_next=emit_next)
    return x_new, h_next, side_b


def kernel(x, ffn1_pre_g, ffn1_post_g, ffn1_w1, ffn1_w3, ffn1_w2, mix_pre_g, mix_post_g, w_in, b_gate,
           hgrn_lb_logits, hgrn_norm_g, rel_bias, w_up_a, w_up_b, w_out,
           ffn2_pre_g, ffn2_post_g, ffn2_w1, ffn2_w3, ffn2_w2):
    batch, seq, d = x.shape
    depth = ffn1_w1.shape[0]
    d_half = d // 2
    lower_bounds = jnp.cumsum(jax.nn.softmax(hgrn_lb_logits.astype(F32), axis=0), axis=0)
    xt = x.reshape(batch * seq, d)
    h = norm_cast(xt, ffn1_pre_g[0])
    for layer in range(depth):
        xt, h, w_in_b = _ffn(xt, h, ffn1_w1[layer], ffn1_w3[layer], ffn1_w2[layer],
                             ffn1_post_g[layer], mix_pre_g[layer], True, side=w_in[layer])

        qs, lf, kk, vv, gs, w_up_a_b = proj(h, w_in_b, [0, d_half, 2 * d_half, 3 * d_half], d_half,
                                            [lower_bounds[layer]], _hgrn_epilogue,
                                            [BF16, F32, BF16, BF16, BF16], w_up_a[layer])
        qb, kb, vb, w_up_b_b = proj(h, w_in_b, [4 * d_half, 5 * d_half, 6 * d_half], d_half,
                                    [], _attn_epilogue, [BF16, BF16, BF16], w_up_b[layer], tn=512)
        ga, gb, w_out_b = proj(h, w_in_b, [7 * d_half, 7 * d_half + d], d,
                               [b_gate[layer, 0], b_gate[layer, 1]], _gate_epilogue, [BF16, BF16],
                               w_out[layer], tn=512)

        oa = hgrn(qs, lf, kk, vv, gs, hgrn_norm_g[layer], batch=batch, seq=seq)
        ob = band_attn(qb, kb, vb, _attn_bias_row(rel_bias[layer].astype(F32)), batch=batch, seq=seq)

        m = mix_up(oa, ob, w_up_a_b, w_up_b_b, ga, gb)
        last = layer == depth - 1
        y = matmul_kres(m, w_out_b, tm=1024)
        xt, h = resnorm(y, xt, mix_post_g[layer], ffn2_pre_g[layer], scale=1.0, emit_next=True)
        next_pre = ffn1_pre_g[layer + 1] if not last else ffn2_pre_g[layer]
        xt, h, _ = _ffn(xt, h, ffn2_w1[layer], ffn2_w3[layer], ffn2_w2[layer],
                        ffn2_post_g[layer], next_pre, not last)
    return xt.reshape(batch, seq, d)
```

```python
import functools

import numpy as np
import jax
import jax.numpy as jnp
from jax import lax
from jax.experimental import pallas as pl
from jax.experimental.pallas import tpu as pltpu

F32 = jnp.float32
BF16 = jnp.bfloat16

NORM_EPS = 1e-6
LOG2_E = 1.4426950408889634
MACARON_WEIGHT = 0.5
CHUNK = 64
HEAD_DIM = 128
LANES = 128
F32_SUBLANES = 8
BF16_SUBLANES = 16
LEFT_CHUNKS = 8
MAX_REL = 256
MASK_VALUE = -1e30

MIB = 1024 * 1024
V7X_VMEM_BYTES = 64 * MIB
VMEM_LIMIT_CAP = V7X_VMEM_BYTES - 6 * MIB
VMEM_LIMIT_FLOOR = 32 * MIB
VMEM_SPILL_BYTES = 8 * MIB


def _vmem_limit(pipelined_bytes, resident_bytes=0):
    est = 2 * pipelined_bytes + resident_bytes + VMEM_SPILL_BYTES
    return int(min(max(est, VMEM_LIMIT_FLOOR), VMEM_LIMIT_CAP))


def _nbytes(shape, dtype):
    return int(np.prod(shape)) * jnp.dtype(dtype).itemsize


def _dot(a, b):
    return jnp.dot(a, b, preferred_element_type=F32)


def _dot_nt(a, b):
    return lax.dot_general(a, b, (((1,), (1,)), ((), ())), preferred_element_type=F32)


def _dot_tn(a, b):
    return lax.dot_general(a, b, (((0,), (0,)), ((), ())), preferred_element_type=F32)


def _silu(x):
    return x * jax.nn.sigmoid(x)


def _side_cast_specs(side, n_i, n_j):
    rows, cols = side.shape
    slab, rem = divmod(rows, n_i * n_j)
    assert rem == 0 and slab % BF16_SUBLANES == 0, (rows, n_i, n_j)
    spec = pl.BlockSpec((slab, cols), lambda i, j: (i * n_j + j, 0))
    return spec, jax.ShapeDtypeStruct((rows, cols), BF16), _nbytes((slab, cols), F32) + _nbytes((slab, cols), BF16)


def _norm_cast_kernel(x_ref, g_ref, o_ref):
    x = x_ref[...]
    inv = lax.rsqrt(jnp.mean(x * x, axis=-1, keepdims=True) + NORM_EPS)
    o_ref[...] = (x * inv * g_ref[...]).astype(o_ref.dtype)


def norm_cast(x, g, *, tm=256):
    t, d = x.shape
    return pl.pallas_call(
        _norm_cast_kernel,
        grid=(t // tm,),
        in_specs=[pl.BlockSpec((tm, d), lambda i: (i, 0)),
                  pl.BlockSpec((1, d), lambda i: (0, 0))],
        out_specs=pl.BlockSpec((tm, d), lambda i: (i, 0)),
        out_shape=jax.ShapeDtypeStruct((t, d), BF16),
        compiler_params=pltpu.CompilerParams(
            dimension_semantics=("parallel",),
            vmem_limit_bytes=_vmem_limit(_nbytes((tm, d), F32) + _nbytes((tm, d), BF16),
                                         3 * _nbytes((tm, d), F32))),
        name="norm_cast",
    )(x, g.reshape(1, d))


def _glu_up_kernel(h_ref, w1_ref, w3_ref, side_ref, o_ref, side_o_ref):
    h = h_ref[...]
    a = _dot(h, w1_ref[...].astype(BF16))
    b = _dot(h, w3_ref[...].astype(BF16))
    o_ref[...] = (_silu(a) * b).astype(o_ref.dtype)
    side_o_ref[...] = side_ref[...].astype(side_o_ref.dtype)


def glu_up(h, w1, w3, side, *, tm=2048, tn=256):
    t, d = h.shape
    f = w1.shape[1]
    grid = (t // tm, f // tn)
    side_spec, side_shape, side_bytes = _side_cast_specs(side, *grid)
    return pl.pallas_call(
        _glu_up_kernel,
        grid=grid,
        in_specs=[pl.BlockSpec((tm, d), lambda i, j: (i, 0), pipeline_mode=pl.Buffered(1)),
                  pl.BlockSpec((d, tn), lambda i, j: (0, j)),
                  pl.BlockSpec((d, tn), lambda i, j: (0, j)),
                  side_spec],
        out_specs=[pl.BlockSpec((tm, tn), lambda i, j: (i, j)), side_spec],
        out_shape=[jax.ShapeDtypeStruct((t, f), BF16), side_shape],
        compiler_params=pltpu.CompilerParams(
            dimension_semantics=("parallel", "arbitrary"),
            vmem_limit_bytes=_vmem_limit(
                2 * _nbytes((d, tn), w1.dtype) + _nbytes((tm, tn), BF16) + side_bytes,
                _nbytes((tm, d), BF16) + 4 * _nbytes((tm, tn), F32) + 2 * _nbytes((d, tn), BF16))),
        name="glu_up",
    )(h, w1, w3, side)


def _matmul_kernel(a_ref, w_ref, *rest):
    maybe_side_ref, o_ref, maybe_side_o_ref = rest if len(rest) == 3 else (None, rest[0], None)
    o_ref[...] = _dot(a_ref[...], w_ref[...]).astype(o_ref.dtype)
    if maybe_side_ref is not None:
        maybe_side_o_ref[...] = maybe_side_ref[...].astype(maybe_side_o_ref.dtype)


def matmul_kres(a, w, side=None, *, tm=512, tn=1024):
    t, kdim = a.shape
    n = w.shape[1]
    grid = (n // tn, t // tm)
    in_specs = [pl.BlockSpec((tm, kdim), lambda j, i: (i, 0)),
                pl.BlockSpec((kdim, tn), lambda j, i: (0, j), pipeline_mode=pl.Buffered(1))]
    out_specs = [pl.BlockSpec((tm, tn), lambda j, i: (i, j))]
    out_shape = [jax.ShapeDtypeStruct((t, n), BF16)]
    operands = [a, w]
    side_bytes = 0
    if side is not None:
        side_spec, side_shape, side_bytes = _side_cast_specs(side, *grid)
        in_specs.append(side_spec)
        out_specs.append(side_spec)
        out_shape.append(side_shape)
        operands.append(side)
    outs = pl.pallas_call(
        _matmul_kernel,
        grid=grid,
        in_specs=in_specs,
        out_specs=out_specs,
        out_shape=out_shape,
        compiler_params=pltpu.CompilerParams(
            dimension_semantics=("parallel", "arbitrary"),
            vmem_limit_bytes=_vmem_limit(
                _nbytes((tm, kdim), BF16) + _nbytes((tm, tn), BF16) + side_bytes,
                _nbytes((kdim, tn), BF16) + 2 * _nbytes((tm, tn), F32))),
        name="matmul_kres",
    )(*operands)
    return outs if side is not None else outs[0]


def _resnorm_kernel(y_ref, x_ref, gpost_ref, gnext_ref, xo_ref, *maybe_h_ref, scale):
    y = y_ref[...].astype(F32)
    inv = lax.rsqrt(jnp.mean(y * y, axis=-1, keepdims=True) + NORM_EPS)
    xn = x_ref[...] + scale * (y * inv * gpost_ref[...])
    xo_ref[...] = xn
    if maybe_h_ref:
        inv2 = lax.rsqrt(jnp.mean(xn * xn, axis=-1, keepdims=True) + NORM_EPS)
        maybe_h_ref[0][...] = (xn * inv2 * gnext_ref[...]).astype(BF16)


def resnorm(y, x, g_post, g_next, *, scale, emit_next, tm=256):
    t, d = x.shape
    row = pl.BlockSpec((tm, d), lambda i: (i, 0))
    vec = pl.BlockSpec((1, d), lambda i: (0, 0))
    out_shape = [jax.ShapeDtypeStruct((t, d), F32)]
    if emit_next:
        out_shape.append(jax.ShapeDtypeStruct((t, d), BF16))
    pipelined = (_nbytes((tm, d), y.dtype) + 2 * _nbytes((tm, d), F32)
                 + (_nbytes((tm, d), BF16) if emit_next else 0))
    outs = pl.pallas_call(
        functools.partial(_resnorm_kernel, scale=scale),
        grid=(t // tm,),
        in_specs=[row, row, vec, vec],
        out_specs=[row] * len(out_shape),
        out_shape=out_shape,
        compiler_params=pltpu.CompilerParams(
            dimension_semantics=("parallel",),
            vmem_limit_bytes=_vmem_limit(pipelined, 3 * _nbytes((tm, d), F32))),
        name="resnorm",
    )(y, x, g_post.reshape(1, d), g_next.reshape(1, d))
    return outs if emit_next else (outs[0], None)


def _proj_kernel(*refs, n_seg, n_aux, epilogue):
    h_ref = refs[0]
    w_refs = refs[1:1 + n_seg]
    aux_refs = refs[1 + n_seg:1 + n_seg + n_aux]
    side_ref = refs[1 + n_seg + n_aux]
    out_refs = refs[2 + n_seg + n_aux:-1]
    side_o_ref = refs[-1]
    h = h_ref[...]
    ys = [_dot(h, w_ref[...]) for w_ref in w_refs]
    outs = epilogue(ys, [r[...] for r in aux_refs])
    for o_ref, o in zip(out_refs, outs):
        o_ref[...] = o.astype(o_ref.dtype)
    side_o_ref[...] = side_ref[...].astype(side_o_ref.dtype)


def proj(h, w_in, col_starts, width, aux, epilogue, out_dtypes, side, *, tm=1024, tn=256):
    t, d = h.shape
    n_seg = len(col_starts)
    grid = (t // tm, width // tn)
    side_spec, side_shape, side_bytes = _side_cast_specs(side, *grid)

    def w_spec(start):
        off = start // tn
        return pl.BlockSpec((d, tn), lambda i, j: (0, off + j))

    pipelined = (_nbytes((tm, d), BF16) + n_seg * _nbytes((d, tn), BF16)
                 + sum(_nbytes((tm, tn), dt) for dt in out_dtypes) + side_bytes)
    return pl.pallas_call(
        functools.partial(_proj_kernel, n_seg=n_seg, n_aux=len(aux), epilogue=epilogue),
        grid=grid,
        in_specs=([pl.BlockSpec((tm, d), lambda i, j: (i, 0))]
                  + [w_spec(s) for s in col_starts]
                  + [pl.BlockSpec((1, tn), lambda i, j: (0, j)) for _ in aux]
                  + [side_spec]),
        out_specs=[pl.BlockSpec((tm, tn), lambda i, j: (i, j)) for _ in out_dtypes] + [side_spec],
        out_shape=[jax.ShapeDtypeStruct((t, width), dt) for dt in out_dtypes] + [side_shape],
        compiler_params=pltpu.CompilerParams(
            dimension_semantics=("parallel", "arbitrary"),
            vmem_limit_bytes=_vmem_limit(pipelined, (n_seg + 4) * _nbytes((tm, tn), F32))),
        name="proj",
    )(h, *([w_in] * n_seg), *[a.reshape(1, width) for a in aux], side)


def _hgrn_epilogue(ys, aux):
    q, fr, i, g = ys
    lb, = aux
    f = lb + (1.0 - lb) * jax.nn.sigmoid(fr)
    return _silu(q), jnp.log(f) * LOG2_E, 1.0 - f, i, _silu(g)


def _attn_epilogue(ys, aux):
    q, k, v = ys
    return q * (HEAD_DIM ** -0.5), k, v


def _gate_epilogue(ys, aux):
    ga, gb = ys
    ba, bb = aux
    return jax.nn.sigmoid(ga + ba), jax.nn.sigmoid(gb + bb)


def _lower_half_total(p_hb, level, row8):
    c, w = p_hb.shape
    hb = 1 << level
    if 2 * hb >= F32_SUBLANES:
        blk = min(2 * hb, c)
        return jnp.concatenate(
            [jnp.broadcast_to(p_hb[b * blk + hb - 1:b * blk + hb, :], (blk, w)) for b in range(c // blk)], axis=0)
    x = p_hb.reshape(c // F32_SUBLANES, F32_SUBLANES, w)
    odd = (row8 & 1) == 1
    if level == 0:
        g = jnp.where(odd, pltpu.roll(x, 1, 1), x)
    else:
        z = jnp.where(odd, x, pltpu.roll(x, 7, 1))
        g = jnp.where((row8 & 2) == 0, z, pltpu.roll(z, 2, 1))
    return g.reshape(c, w)


def _add_to_upper_halves(p_hb, g, level, upper_mask):
    c = p_hb.shape[0]
    hb = 1 << level
    if hb < F32_SUBLANES:
        return p_hb + jnp.where(upper_mask, g, 0.0)
    pieces = []
    for lo in range(0, c, 2 * hb):
        pieces += [p_hb[lo:lo + hb], p_hb[lo + hb:lo + 2 * hb] + g[lo + hb:lo + 2 * hb]]
    return jnp.concatenate(pieces, axis=0)


def _hgrn_kernel(qs_ref, lf_ref, kk_ref, v_ref, gs_ref, ng_ref, o_ref, st_ref, *, heads_per_block,
                 chunks_per_iter):
    c = CHUNK
    dk = HEAD_DIM
    w = qs_ref.shape[1]
    n_levels = CHUNK.bit_length() - 1

    @pl.when(pl.program_id(2) == 0)
    def _():
        st_ref[...] = jnp.zeros_like(st_ref)

    row = lax.broadcasted_iota(jnp.int32, (c, w), 0)
    row8 = lax.broadcasted_iota(jnp.int32, (c // F32_SUBLANES, F32_SUBLANES, w), 1)
    upper = [((row >> p) & 1) == 1 for p in range(n_levels)]
    ti = lax.broadcasted_iota(jnp.int32, (c, c), 0)
    si = lax.broadcasted_iota(jnp.int32, (c, c), 1)
    diag_mask = ti == si
    level_masks = [((ti >> (p + 1)) == (si >> (p + 1))) & (((ti >> p) & 1) == 1) & (((si >> p) & 1) == 0)
                   for p in range(n_levels)]

    def iter_body(it, carry):
        pre = []
        for cc in range(chunks_per_iter):
            rows = pl.ds(pl.multiple_of((it * chunks_per_iter + cc) * c, c), c)
            p_hb = lf_ref[rows, :]
            e_q, e_k = [], []
            for p in range(n_levels + 1):
                g = _lower_half_total(p_hb, p, row8)
                e_q.append(jnp.exp2(p_hb))
                e_k.append(None if p == 0 else jnp.exp2(g - p_hb))
                if p < n_levels:
                    p_hb = _add_to_upper_halves(p_hb, g, p, upper[p])
            pre.append((rows, e_q, e_k))
        indep = {}
        for cc, (rows, e_q, e_k) in enumerate(pre):
            for hh in range(heads_per_block):
                cols = slice(hh * dk, (hh + 1) * dk)
                qb = qs_ref[rows, cols]
                kb = kk_ref[rows, cols]
                v = v_ref[rows, cols]
                q = qb.astype(F32)
                k = kb.astype(F32)
                parts = [_dot_nt(qb, kb)]
                for p in range(n_levels):
                    qt = (q * e_q[p][:, cols]).astype(BF16)
                    kt = kb if p == 0 else (k * e_k[p][:, cols]).astype(BF16)
                    parts.append(_dot_nt(qt, kt))
                e_in = e_q[n_levels][:, cols]
                e_out = e_k[n_levels][:, cols]
                kv = _dot_tn(v, (k * e_out).astype(BF16))
                indep[cc, hh] = (parts, kv, (q * e_in).astype(BF16), v, e_in[c - 1:c, :])
        for hh in range(heads_per_block):
            cols = slice(hh * dk, (hh + 1) * dk)
            st = st_ref[hh]
            for cc, (rows, _, _) in enumerate(pre):
                parts, kv, q_in, v, decay_all = indep[cc, hh]
                scores = jnp.where(diag_mask, parts[0], 0.0)
                for p in range(n_levels):
                    scores = jnp.where(level_masks[p], parts[p + 1], scores)
                o = _dot(scores.astype(BF16), v) + _dot_nt(q_in, st.astype(BF16))
                st = st * decay_all + kv
                inv = lax.rsqrt(jnp.mean(o * o, axis=-1, keepdims=True) + NORM_EPS)
                o = o * inv * ng_ref[:, cols] * gs_ref[rows, cols].astype(F32)
                o_ref[rows, cols] = o.astype(o_ref.dtype)
            st_ref[hh] = st
        return carry

    lax.fori_loop(0, qs_ref.shape[0] // (c * chunks_per_iter), iter_body, 0)


def hgrn(qs, lf, kk, v, gs, norm_g, *, batch, seq, block_len=1024, heads_per_block=4, chunks_per_iter=8):
    t, width = qs.shape
    bw = heads_per_block * HEAD_DIM
    n_l = seq // block_len
    tok = pl.BlockSpec((block_len, bw), lambda b, h, l: (b * n_l + l, h))
    pipelined = 4 * _nbytes((block_len, bw), BF16) + _nbytes((block_len, bw), F32)
    return pl.pallas_call(
        functools.partial(_hgrn_kernel, heads_per_block=heads_per_block, chunks_per_iter=chunks_per_iter),
        grid=(batch, width // bw, n_l),
        in_specs=[tok, tok, tok, tok, tok,
                  pl.BlockSpec((1, bw), lambda b, h, l: (0, h))],
        out_specs=tok,
        out_shape=jax.ShapeDtypeStruct((t, width), BF16),
        scratch_shapes=[pltpu.VMEM((heads_per_block, HEAD_DIM, HEAD_DIM), F32)],
        compiler_params=pltpu.CompilerParams(
            dimension_semantics=("parallel", "parallel", "arbitrary"),
            vmem_limit_bytes=_vmem_limit(pipelined, 0)),
        name="hgrn",
    )(qs, lf, kk, v, gs, norm_g.reshape(1, width))


ATTN_Q_BLOCK = 2 * CHUNK
ATTN_WINDOW = 640
ATTN_LEAD = ATTN_WINDOW - ATTN_Q_BLOCK
ATTN_BIAS_LANES = pl.cdiv(ATTN_WINDOW + ATTN_Q_BLOCK - 1, LANES) * LANES


def _attn_bias_row(rel_bias):
    m = np.arange(ATTN_BIAS_LANES)
    m = np.where(m >= ATTN_WINDOW, m - ATTN_BIAS_LANES, m)
    idx = np.clip(ATTN_LEAD - m, -MAX_REL, MAX_REL) + MAX_REL
    runs, start = [], 0
    for pos in range(1, len(idx) + 1):
        if pos == len(idx) or idx[pos] != idx[pos - 1] - 1:
            runs.append((start, pos))
            start = pos
    pieces = [jnp.flip(rel_bias[:, int(idx[hi - 1]):int(idx[lo]) + 1], axis=1) for lo, hi in runs]
    return jnp.concatenate(pieces, axis=1)


def _attn_kernel(q_ref, k_ref, v_ref, brow_ref, o_ref, kpad_ref, vpad_ref, *, group):
    seq = q_ref.shape[0]
    kpad_ref[0:ATTN_LEAD, :] = jnp.zeros((ATTN_LEAD, HEAD_DIM), kpad_ref.dtype)
    vpad_ref[0:ATTN_LEAD, :] = jnp.zeros((ATTN_LEAD, HEAD_DIM), vpad_ref.dtype)
    kpad_ref[ATTN_LEAD:, :] = k_ref[...]
    vpad_ref[ATTN_LEAD:, :] = v_ref[...]
    brow = jnp.broadcast_to(brow_ref[0], (ATTN_Q_BLOCK, ATTN_BIAS_LANES))
    toep = pltpu.roll(brow, 0, 1, stride=1, stride_axis=0)[:, :ATTN_WINDOW]
    qi = lax.broadcasted_iota(jnp.int32, (ATTN_Q_BLOCK, ATTN_WINDOW), 0)
    col = lax.broadcasted_iota(jnp.int32, (ATTN_Q_BLOCK, ATTN_WINDOW), 1)
    qc = qi // CHUNK
    kc = col // CHUNK
    lead = ATTN_LEAD // CHUNK
    bias = jnp.where((kc >= qc + lead - LEFT_CHUNKS) & (kc <= qc + lead), toep, MASK_VALUE)

    def do_group(base, masked):
        starts = [base + u * ATTN_Q_BLOCK for u in range(group)]
        scores = []
        for start in starts:
            q = q_ref[pl.ds(start, ATTN_Q_BLOCK), :]
            kw = kpad_ref[pl.ds(start, ATTN_WINDOW), :]
            s = _dot_nt(q, kw) + bias
            if masked:
                s = jnp.where(col >= ATTN_LEAD - start, s, MASK_VALUE)
            scores.append(s)
        probs = []
        for s in scores:
            m = jnp.max(s, axis=-1, keepdims=True)
            p = jnp.exp(s - m)
            probs.append((p.astype(BF16), jnp.sum(p, axis=-1, keepdims=True)))
        for start, (p, denom) in zip(starts, probs):
            vw = vpad_ref[pl.ds(start, ATTN_WINDOW), :]
            o = _dot(p, vw) / denom
            o_ref[pl.ds(start, ATTN_Q_BLOCK), :] = o.astype(o_ref.dtype)

    span = group * ATTN_Q_BLOCK
    n_masked = pl.cdiv(ATTN_LEAD, span)
    for g in range(n_masked):
        do_group(g * span, True)

    def body(g, carry):
        do_group(pl.multiple_of(g * span, span), False)
        return carry

    lax.fori_loop(n_masked, seq // span, body, 0)


def band_attn(q, k, v, bias_row, *, batch, seq, group=16):
    t, width = q.shape
    tok = pl.BlockSpec((seq, HEAD_DIM), lambda b, h: (b, h))
    return pl.pallas_call(
        functools.partial(_attn_kernel, group=group),
        grid=(batch, width // HEAD_DIM),
        in_specs=[tok, tok, tok,
                  pl.BlockSpec((1, 1, ATTN_BIAS_LANES), lambda b, h: (h, 0, 0))],
        out_specs=tok,
        out_shape=jax.ShapeDtypeStruct((t, width), BF16),
        scratch_shapes=[pltpu.VMEM((seq + ATTN_LEAD, HEAD_DIM), BF16),
                        pltpu.VMEM((seq + ATTN_LEAD, HEAD_DIM), BF16)],
        compiler_params=pltpu.CompilerParams(
            dimension_semantics=("parallel", "parallel"),
            vmem_limit_bytes=_vmem_limit(
                4 * _nbytes((seq, HEAD_DIM), BF16),
                2 * _nbytes((seq + ATTN_LEAD, HEAD_DIM), BF16))),
        name="band_attn",
    )(q, k, v, bias_row.reshape(bias_row.shape[0], 1, ATTN_BIAS_LANES))


def _mix_up_kernel(oa_ref, ob_ref, wa_ref, wb_ref, ga_ref, gb_ref, o_ref):
    ya = _dot(oa_ref[...], wa_ref[...])
    yb = _dot(ob_ref[...], wb_ref[...])
    o_ref[...] = (ga_ref[...].astype(F32) * ya + gb_ref[...].astype(F32) * yb).astype(o_ref.dtype)


def mix_up(oa, ob, wa, wb, ga, gb, *, tm=1024, tn=1024):
    t, kdim = oa.shape
    d = wa.shape[1]
    pipelined = (2 * _nbytes((tm, kdim), BF16) + 2 * _nbytes((kdim, tn), BF16) + 3 * _nbytes((tm, tn), BF16))
    return pl.pallas_call(
        _mix_up_kernel,
        grid=(t // tm, d // tn),
        in_specs=[pl.BlockSpec((tm, kdim), lambda i, j: (i, 0)),
                  pl.BlockSpec((tm, kdim), lambda i, j: (i, 0)),
                  pl.BlockSpec((kdim, tn), lambda i, j: (0, j)),
                  pl.BlockSpec((kdim, tn), lambda i, j: (0, j)),
                  pl.BlockSpec((tm, tn), lambda i, j: (i, j)),
                  pl.BlockSpec((tm, tn), lambda i, j: (i, j))],
        out_specs=pl.BlockSpec((tm, tn), lambda i, j: (i, j)),
        out_shape=jax.ShapeDtypeStruct((t, d), BF16),
        compiler_params=pltpu.CompilerParams(
            dimension_semantics=("parallel", "arbitrary"),
            vmem_limit_bytes=_vmem_limit(pipelined, 4 * _nbytes((tm, tn), F32))),
        name="mix_up",
    )(oa, ob, wa, wb, ga, gb)


def _ffn(x, h, w1, w3, w2, post_g, next_g, emit_next, side=None):
    g, w2_b = glu_up(h, w1, w3, w2)
    if side is None:
        y, side_b = matmul_kres(g, w2_b), None
    else:
        y, side_b = matmul_kres(g, w2_b, side)
    x_new, h_next = resnorm(y, x, post_g, next_g, scale=MACARON_WEIGHT, emit_next=emit_next)
    return x_new, h_next, side_b


def kernel(x, ffn1_pre_g, ffn1_post_g, ffn1_w1, ffn1_w3, ffn1_w2, mix_pre_g, mix_post_g, w_in, b_gate,
           hgrn_lb_logits, hgrn_norm_g, rel_bias, w_up_a, w_up_b, w_out,
           ffn2_pre_g, ffn2_post_g, ffn2_w1, ffn2_w3, ffn2_w2):
    batch, seq, d = x.shape
    depth = ffn1_w1.shape[0]
    d_half = d // 2
    lower_bounds = jnp.cumsum(jax.nn.softmax(hgrn_lb_logits.astype(F32), axis=0), axis=0)
    xt = x.reshape(batch * seq, d)
    h = norm_cast(xt, ffn1_pre_g[0])
    for layer in range(depth):
        xt, h, w_in_b = _ffn(xt, h, ffn1_w1[layer], ffn1_w3[layer], ffn1_w2[layer],
                             ffn1_post_g[layer], mix_pre_g[layer], True, side=w_in[layer])

        qs, lf, kk, vv, gs, w_up_a_b = proj(h, w_in_b, [0, d_half, 2 * d_half, 3 * d_half], d_half,
                                            [lower_bounds[layer]], _hgrn_epilogue,
                                            [BF16, F32, BF16, BF16, BF16], w_up_a[layer])
        qb, kb, vb, w_up_b_b = proj(h, w_in_b, [4 * d_half, 5 * d_half, 6 * d_half], d_half,
                                    [], _attn_epilogue, [BF16, BF16, BF16], w_up_b[layer], tn=512)
        ga, gb, w_out_b = proj(h, w_in_b, [7 * d_half, 7 * d_half + d], d,
                               [b_gate[layer, 0], b_gate[layer, 1]], _gate_epilogue, [BF16, BF16],
                               w_out[layer], tn=512)

        oa = hgrn(qs, lf, kk, vv, gs, hgrn_norm_g[layer], batch=batch, seq=seq)
        ob = band_attn(qb, kb, vb, _attn_bias_row(rel_bias[layer].astype(F32)), batch=batch, seq=seq)

        m = mix_up(oa, ob, w_up_a_b, w_up_b_b, ga, gb)
        last = layer == depth - 1
        y = matmul_kres(m, w_out_b, tm=1024)
        xt, h = resnorm(y, xt, mix_post_g[layer], ffn2_pre_g[layer], scale=1.0, emit_next=True)
        next_pre = ffn1_pre_g[layer + 1] if not last else ffn2_pre_g[layer]
        xt, h, _ = _ffn(xt, h, ffn2_w1[layer], ffn2_w3[layer], ffn2_w2[layer],
                        ffn2_post_g[layer], next_pre, not last)
    return xt.reshape(batch, seq, d)
```

```python
import functools

import numpy as np
import jax
import jax.numpy as jnp
from jax import lax
from jax.experimental import pallas as pl
from jax.experimental.pallas import tpu as pltpu

F32 = jnp.float32
BF16 = jnp.bfloat16

NORM_EPS = 1e-6
LOG2_E = 1.4426950408889634
MACARON_WEIGHT = 0.5
CHUNK = 64
HEAD_DIM = 128
LANES = 128
F32_SUBLANES = 8
BF16_SUBLANES = 16
LEFT_CHUNKS = 8
MAX_REL = 256
MASK_VALUE = -1e30

MIB = 1024 * 1024
V7X_VMEM_BYTES = 64 * MIB
VMEM_LIMIT_CAP = V7X_VMEM_BYTES - 6 * MIB
VMEM_LIMIT_FLOOR = 32 * MIB
VMEM_SPILL_BYTES = 8 * MIB


def _vmem_limit(pipelined_bytes, resident_bytes=0):
    est = 2 * pipelined_bytes + resident_bytes + VMEM_SPILL_BYTES
    return int(min(max(est, VMEM_LIMIT_FLOOR), VMEM_LIMIT_CAP))


def _nbytes(shape, dtype):
    return int(np.prod(shape)) * jnp.dtype(dtype).itemsize


def _dot(a, b):
    return jnp.dot(a, b, preferred_element_type=F32)


def _dot_nt(a, b):
    return lax.dot_general(a, b, (((1,), (1,)), ((), ())), preferred_element_type=F32)


def _dot_tn(a, b):
    return lax.dot_general(a, b, (((0,), (0,)), ((), ())), preferred_element_type=F32)


def _silu(x):
    return x * jax.nn.sigmoid(x)


def _side_cast_specs(side, n_i, n_j):
    rows, cols = side.shape
    slab, rem = divmod(rows, n_i * n_j)
    assert rem == 0 and slab % BF16_SUBLANES == 0, (rows, n_i, n_j)
    spec = pl.BlockSpec((slab, cols), lambda i, j: (i * n_j + j, 0))
    return spec, jax.ShapeDtypeStruct((rows, cols), BF16), _nbytes((slab, cols), F32) + _nbytes((slab, cols), BF16)


def _norm_cast_kernel(x_ref, g_ref, o_ref):
    x = x_ref[...]
    inv = lax.rsqrt(jnp.mean(x * x, axis=-1, keepdims=True) + NORM_EPS)
    o_ref[...] = (x * inv * g_ref[...]).astype(o_ref.dtype)


def norm_cast(x, g, *, tm=256):
    t, d = x.shape
    return pl.pallas_call(
        _norm_cast_kernel,
        grid=(t // tm,),
        in_specs=[pl.BlockSpec((tm, d), lambda i: (i, 0)),
                  pl.BlockSpec((1, d), lambda i: (0, 0))],
        out_specs=pl.BlockSpec((tm, d), lambda i: (i, 0)),
        out_shape=jax.ShapeDtypeStruct((t, d), BF16),
        compiler_params=pltpu.CompilerParams(
            dimension_semantics=("parallel",),
            vmem_limit_bytes=_vmem_limit(_nbytes((tm, d), F32) + _nbytes((tm, d), BF16),
                                         3 * _nbytes((tm, d), F32))),
        name="norm_cast",
    )(x, g.reshape(1, d))


def _glu_up_kernel(h_ref, w1_ref, w3_ref, side_ref, o_ref, side_o_ref):
    h = h_ref[...]
    a = _dot(h, w1_ref[...].astype(BF16))
    b = _dot(h, w3_ref[...].astype(BF16))
    o_ref[...] = (_silu(a) * b).astype(o_ref.dtype)
    side_o_ref[...] = side_ref[...].astype(side_o_ref.dtype)


def glu_up(h, w1, w3, side, *, tm=2048, tn=256):
    t, d = h.shape
    f = w1.shape[1]
    grid = (t // tm, f // tn)
    side_spec, side_shape, side_bytes = _side_cast_specs(side, *grid)
    return pl.pallas_call(
        _glu_up_kernel,
        grid=grid,
        in_specs=[pl.BlockSpec((tm, d), lambda i, j: (i, 0), pipeline_mode=pl.Buffered(1)),
                  pl.BlockSpec((d, tn), lambda i, j: (0, j)),
                  pl.BlockSpec((d, tn), lambda i, j: (0, j)),
                  side_spec],
        out_specs=[pl.BlockSpec((tm, tn), lambda i, j: (i, j)), side_spec],
        out_shape=[jax.ShapeDtypeStruct((t, f), BF16), side_shape],
        compiler_params=pltpu.CompilerParams(
            dimension_semantics=("parallel", "arbitrary"),
            vmem_limit_bytes=_vmem_limit(
                2 * _nbytes((d, tn), w1.dtype) + _nbytes((tm, tn), BF16) + side_bytes,
                _nbytes((tm, d), BF16) + 4 * _nbytes((tm, tn), F32) + 2 * _nbytes((d, tn), BF16))),
        name="glu_up",
    )(h, w1, w3, side)


def _matmul_kernel(a_ref, w_ref, *rest):
    maybe_side_ref, o_ref, maybe_side_o_ref = rest if len(rest) == 3 else (None, rest[0], None)
    o_ref[...] = _dot(a_ref[...], w_ref[...]).astype(o_ref.dtype)
    if maybe_side_ref is not None:
        maybe_side_o_ref[...] = maybe_side_ref[...].astype(maybe_side_o_ref.dtype)


def matmul_kres(a, w, side=None, *, tm=512, tn=1024):
    t, kdim = a.shape
    n = w.shape[1]
    grid = (n // tn, t // tm)
    in_specs = [pl.BlockSpec((tm, kdim), lambda j, i: (i, 0)),
                pl.BlockSpec((kdim, tn), lambda j, i: (0, j), pipeline_mode=pl.Buffered(1))]
    out_specs = [pl.BlockSpec((tm, tn), lambda j, i: (i, j))]
    out_shape = [jax.ShapeDtypeStruct((t, n), BF16)]
    operands = [a, w]
    side_bytes = 0
    if side is not None:
        side_spec, side_shape, side_bytes = _side_cast_specs(side, *grid)
        in_specs.append(side_spec)
        out_specs.append(side_spec)
        out_shape.append(side_shape)
        operands.append(side)
    outs = pl.pallas_call(
        _matmul_kernel,
        grid=grid,
        in_specs=in_specs,
        out_specs=out_specs,
        out_shape=out_shape,
        compiler_params=pltpu.CompilerParams(
            dimension_semantics=("parallel", "arbitrary"),
            vmem_limit_bytes=_vmem_limit(
                _nbytes((tm, kdim), BF16) + _nbytes((tm, tn), BF16) + side_bytes,
                _nbytes((kdim, tn), BF16) + 2 * _nbytes((tm, tn), F32))),
        name="matmul_kres",
    )(*operands)
    return outs if side is not None else outs[0]


def _resnorm_kernel(y_ref, x_ref, gpost_ref, gnext_ref, xo_ref, *maybe_h_ref, scale):
    y = y_ref[...].astype(F32)
    inv = lax.rsqrt(jnp.mean(y * y, axis=-1, keepdims=True) + NORM_EPS)
    xn = x_ref[...] + scale * (y * inv * gpost_ref[...])
    xo_ref[...] = xn
    if maybe_h_ref:
        inv2 = lax.rsqrt(jnp.mean(xn * xn, axis=-1, keepdims=True) + NORM_EPS)
        maybe_h_ref[0][...] = (xn * inv2 * gnext_ref[...]).astype(BF16)


def resnorm(y, x, g_post, g_next, *, scale, emit_next, tm=256):
    t, d = x.shape
    row = pl.BlockSpec((tm, d), lambda i: (i, 0))
    vec = pl.BlockSpec((1, d), lambda i: (0, 0))
    out_shape = [jax.ShapeDtypeStruct((t, d), F32)]
    if emit_next:
        out_shape.append(jax.ShapeDtypeStruct((t, d), BF16))
    pipelined = (_nbytes((tm, d), y.dtype) + 2 * _nbytes((tm, d), F32)
                 + (_nbytes((tm, d), BF16) if emit_next else 0))
    outs = pl.pallas_call(
        functools.partial(_resnorm_kernel, scale=scale),
        grid=(t // tm,),
        in_specs=[row, row, vec, vec],
        out_specs=[row] * len(out_shape),
        out_shape=out_shape,
        compiler_params=pltpu.CompilerParams(
            dimension_semantics=("parallel",),
            vmem_limit_bytes=_vmem_limit(pipelined, 3 * _nbytes((tm, d), F32))),
        name="resnorm",
    )(y, x, g_post.reshape(1, d), g_next.reshape(1, d))
    return outs if emit_next else (outs[0], None)


def _proj_kernel(*refs, n_seg, n_aux, epilogue):
    h_ref = refs[0]
    w_refs = refs[1:1 + n_seg]
    aux_refs = refs[1 + n_seg:1 + n_seg + n_aux]
    side_ref = refs[1 + n_seg + n_aux]
    out_refs = refs[2 + n_seg + n_aux:-1]
    side_o_ref = refs[-1]
    h = h_ref[...]
    ys = [_dot(h, w_ref[...]) for w_ref in w_refs]
    outs = epilogue(ys, [r[...] for r in aux_refs])
    for o_ref, o in zip(out_refs, outs):
        o_ref[...] = o.astype(o_ref.dtype)
    side_o_ref[...] = side_ref[...].astype(side_o_ref.dtype)


def proj(h, w_in, col_starts, width, aux, epilogue, out_dtypes, side, *, tm=1024, tn=512):
    t, d = h.shape
    n_seg = len(col_starts)
    grid = (width // tn, t // tm)
    side_spec, side_shape, side_bytes = _side_cast_specs(side, *grid)

    def w_spec(start):
        off = start // tn
        return pl.BlockSpec((d, tn), lambda j, i: (0, off + j), pipeline_mode=pl.Buffered(1))

    pipelined = _nbytes((tm, d), BF16) + sum(_nbytes((tm, tn), dt) for dt in out_dtypes) + side_bytes
    return pl.pallas_call(
        functools.partial(_proj_kernel, n_seg=n_seg, n_aux=len(aux), epilogue=epilogue),
        grid=grid,
        in_specs=([pl.BlockSpec((tm, d), lambda j, i: (i, 0))]
                  + [w_spec(s) for s in col_starts]
                  + [pl.BlockSpec((1, tn), lambda j, i: (0, j)) for _ in aux]
                  + [side_spec]),
        out_specs=[pl.BlockSpec((tm, tn), lambda j, i: (i, j)) for _ in out_dtypes] + [side_spec],
        out_shape=[jax.ShapeDtypeStruct((t, width), dt) for dt in out_dtypes] + [side_shape],
        compiler_params=pltpu.CompilerParams(
            dimension_semantics=("parallel", "arbitrary"),
            vmem_limit_bytes=_vmem_limit(
                pipelined, n_seg * _nbytes((d, tn), BF16) + (n_seg + 4) * _nbytes((tm, tn), F32))),
        name="proj",
    )(h, *([w_in] * n_seg), *[a.reshape(1, width) for a in aux], side)


def _hgrn_epilogue(ys, aux):
    q, fr, i, g = ys
    lb, = aux
    f = lb + (1.0 - lb) * jax.nn.sigmoid(fr)
    return _silu(q), jnp.log(f) * LOG2_E, 1.0 - f, i, _silu(g)


def _attn_epilogue(ys, aux):
    q, k, v = ys
    return q * (HEAD_DIM ** -0.5), k, v


def _gate_epilogue(ys, aux):
    ga, gb = ys
    ba, bb = aux
    return jax.nn.sigmoid(ga + ba), jax.nn.sigmoid(gb + bb)


def _lower_half_total(p_hb, level, row8):
    c, w = p_hb.shape
    hb = 1 << level
    if 2 * hb >= F32_SUBLANES:
        blk = min(2 * hb, c)
        return jnp.concatenate(
            [jnp.broadcast_to(p_hb[b * blk + hb - 1:b * blk + hb, :], (blk, w)) for b in range(c // blk)], axis=0)
    x = p_hb.reshape(c // F32_SUBLANES, F32_SUBLANES, w)
    odd = (row8 & 1) == 1
    if level == 0:
        g = jnp.where(odd, pltpu.roll(x, 1, 1), x)
    else:
        z = jnp.where(odd, x, pltpu.roll(x, 7, 1))
        g = jnp.where((row8 & 2) == 0, z, pltpu.roll(z, 2, 1))
    return g.reshape(c, w)


def _add_to_upper_halves(p_hb, g, level, upper_mask):
    c = p_hb.shape[0]
    hb = 1 << level
    if hb < F32_SUBLANES:
        return p_hb + jnp.where(upper_mask, g, 0.0)
    pieces = []
    for lo in range(0, c, 2 * hb):
        pieces += [p_hb[lo:lo + hb], p_hb[lo + hb:lo + 2 * hb] + g[lo + hb:lo + 2 * hb]]
    return jnp.concatenate(pieces, axis=0)


def _hgrn_kernel(qs_ref, lf_ref, kk_ref, v_ref, gs_ref, ng_ref, o_ref, st_ref, *, heads_per_block,
                 chunks_per_iter):
    c = CHUNK
    dk = HEAD_DIM
    w = qs_ref.shape[1]
    n_levels = CHUNK.bit_length() - 1

    @pl.when(pl.program_id(2) == 0)
    def _():
        st_ref[...] = jnp.zeros_like(st_ref)

    row = lax.broadcasted_iota(jnp.int32, (c, w), 0)
    row8 = lax.broadcasted_iota(jnp.int32, (c // F32_SUBLANES, F32_SUBLANES, w), 1)
    upper = [((row >> p) & 1) == 1 for p in range(n_levels)]
    ti = lax.broadcasted_iota(jnp.int32, (c, c), 0)
    si = lax.broadcasted_iota(jnp.int32, (c, c), 1)
    diag_mask = ti == si
    level_masks = [((ti >> (p + 1)) == (si >> (p + 1))) & (((ti >> p) & 1) == 1) & (((si >> p) & 1) == 0)
                   for p in range(n_levels)]

    def iter_body(it, carry):
        pre = []
        for cc in range(chunks_per_iter):
            rows = pl.ds(pl.multiple_of((it * chunks_per_iter + cc) * c, c), c)
            p_hb = lf_ref[rows, :]
            e_q, e_k = [], []
            for p in range(n_levels + 1):
                g = _lower_half_total(p_hb, p, row8)
                e_q.append(jnp.exp2(p_hb))
                e_k.append(None if p == 0 else jnp.exp2(g - p_hb))
                if p < n_levels:
                    p_hb = _add_to_upper_halves(p_hb, g, p, upper[p])
            pre.append((rows, e_q, e_k))
        indep = {}
        for cc, (rows, e_q, e_k) in enumerate(pre):
            for hh in range(heads_per_block):
                cols = slice(hh * dk, (hh + 1) * dk)
                qb = qs_ref[rows, cols]
                kb = kk_ref[rows, cols]
                v = v_ref[rows, cols]
                q = qb.astype(F32)
                k = kb.astype(F32)
                parts = [_dot_nt(qb, kb)]
                for p in range(n_levels):
                    qt = (q * e_q[p][:, cols]).astype(BF16)
                    kt = kb if p == 0 else (k * e_k[p][:, cols]).astype(BF16)
                    parts.append(_dot_nt(qt, kt))
                e_in = e_q[n_levels][:, cols]
                e_out = e_k[n_levels][:, cols]
                kv = _dot_tn(v, (k * e_out).astype(BF16))
                indep[cc, hh] = (parts, kv, (q * e_in).astype(BF16), v, e_in[c - 1:c, :])
        for hh in range(heads_per_block):
            cols = slice(hh * dk, (hh + 1) * dk)
            st = st_ref[hh]
            for cc, (rows, _, _) in enumerate(pre):
                parts, kv, q_in, v, decay_all = indep[cc, hh]
                scores = jnp.where(diag_mask, parts[0], 0.0)
                for p in range(n_levels):
                    scores = jnp.where(level_masks[p], parts[p + 1], scores)
                o = _dot(scores.astype(BF16), v) + _dot_nt(q_in, st.astype(BF16))
                st = st * decay_all + kv
                inv = lax.rsqrt(jnp.mean(o * o, axis=-1, keepdims=True) + NORM_EPS)
                o = o * inv * ng_ref[:, cols] * gs_ref[rows, cols].astype(F32)
                o_ref[rows, cols] = o.astype(o_ref.dtype)
            st_ref[hh] = st
        return carry

    lax.fori_loop(0, qs_ref.shape[0] // (c * chunks_per_iter), iter_body, 0)


def hgrn(qs, lf, kk, v, gs, norm_g, *, batch, seq, block_len=1024, heads_per_block=4, chunks_per_iter=8):
    t, width = qs.shape
    bw = heads_per_block * HEAD_DIM
    n_l = seq // block_len
    tok = pl.BlockSpec((block_len, bw), lambda b, h, l: (b * n_l + l, h))
    pipelined = 4 * _nbytes((block_len, bw), BF16) + _nbytes((block_len, bw), F32)
    return pl.pallas_call(
        functools.partial(_hgrn_kernel, heads_per_block=heads_per_block, chunks_per_iter=chunks_per_iter),
        grid=(batch, width // bw, n_l),
        in_specs=[tok, tok, tok, tok, tok,
                  pl.BlockSpec((1, bw), lambda b, h, l: (0, h))],
        out_specs=tok,
        out_shape=jax.ShapeDtypeStruct((t, width), BF16),
        scratch_shapes=[pltpu.VMEM((heads_per_block, HEAD_DIM, HEAD_DIM), F32)],
        compiler_params=pltpu.CompilerParams(
            dimension_semantics=("parallel", "parallel", "arbitrary"),
            vmem_limit_bytes=_vmem_limit(pipelined, 0)),
        name="hgrn",
    )(qs, lf, kk, v, gs, norm_g.reshape(1, width))


ATTN_Q_BLOCK = 2 * CHUNK
ATTN_WINDOW = 640
ATTN_LEAD = ATTN_WINDOW - ATTN_Q_BLOCK
ATTN_BIAS_LANES = pl.cdiv(ATTN_WINDOW + ATTN_Q_BLOCK - 1, LANES) * LANES


def _attn_bias_row(rel_bias):
    m = np.arange(ATTN_BIAS_LANES)
    m = np.where(m >= ATTN_WINDOW, m - ATTN_BIAS_LANES, m)
    idx = np.clip(ATTN_LEAD - m, -MAX_REL, MAX_REL) + MAX_REL
    runs, start = [], 0
    for pos in range(1, len(idx) + 1):
        if pos == len(idx) or idx[pos] != idx[pos - 1] - 1:
            runs.append((start, pos))
            start = pos
    pieces = [jnp.flip(rel_bias[:, int(idx[hi - 1]):int(idx[lo]) + 1], axis=1) for lo, hi in runs]
    return jnp.concatenate(pieces, axis=1)


def _attn_kernel(q_ref, k_ref, v_ref, brow_ref, o_ref, kpad_ref, vpad_ref, *, group):
    seq = q_ref.shape[0]
    kpad_ref[0:ATTN_LEAD, :] = jnp.zeros((ATTN_LEAD, HEAD_DIM), kpad_ref.dtype)
    vpad_ref[0:ATTN_LEAD, :] = jnp.zeros((ATTN_LEAD, HEAD_DIM), vpad_ref.dtype)
    kpad_ref[ATTN_LEAD:, :] = k_ref[...]
    vpad_ref[ATTN_LEAD:, :] = v_ref[...]
    brow = jnp.broadcast_to(brow_ref[0], (ATTN_Q_BLOCK, ATTN_BIAS_LANES))
    toep = pltpu.roll(brow, 0, 1, stride=1, stride_axis=0)[:, :ATTN_WINDOW]
    qi = lax.broadcasted_iota(jnp.int32, (ATTN_Q_BLOCK, ATTN_WINDOW), 0)
    col = lax.broadcasted_iota(jnp.int32, (ATTN_Q_BLOCK, ATTN_WINDOW), 1)
    qc = qi // CHUNK
    kc = col // CHUNK
    lead = ATTN_LEAD // CHUNK
    bias = jnp.where((kc >= qc + lead - LEFT_CHUNKS) & (kc <= qc + lead), toep, MASK_VALUE)

    def do_group(base, masked):
        starts = [base + u * ATTN_Q_BLOCK for u in range(group)]
        scores = []
        for start in starts:
            q = q_ref[pl.ds(start, ATTN_Q_BLOCK), :]
            kw = kpad_ref[pl.ds(start, ATTN_WINDOW), :]
            s = _dot_nt(q, kw) + bias
            if masked:
                s = jnp.where(col >= ATTN_LEAD - start, s, MASK_VALUE)
            scores.append(s)
        probs = []
        for s in scores:
            m = jnp.max(s, axis=-1, keepdims=True)
            p = jnp.exp(s - m)
            probs.append((p.astype(BF16), jnp.sum(p, axis=-1, keepdims=True)))
        for start, (p, denom) in zip(starts, probs):
            vw = vpad_ref[pl.ds(start, ATTN_WINDOW), :]
            o = _dot(p, vw) / denom
            o_ref[pl.ds(start, ATTN_Q_BLOCK), :] = o.astype(o_ref.dtype)

    span = group * ATTN_Q_BLOCK
    n_masked = pl.cdiv(ATTN_LEAD, span)
    for g in range(n_masked):
        do_group(g * span, True)

    def body(g, carry):
        do_group(pl.multiple_of(g * span, span), False)
        return carry

    lax.fori_loop(n_masked, seq // span, body, 0)


def band_attn(q, k, v, bias_row, *, batch, seq, group=16):
    t, width = q.shape
    tok = pl.BlockSpec((seq, HEAD_DIM), lambda b, h: (b, h))
    return pl.pallas_call(
        functools.partial(_attn_kernel, group=group),
        grid=(batch, width // HEAD_DIM),
        in_specs=[tok, tok, tok,
                  pl.BlockSpec((1, 1, ATTN_BIAS_LANES), lambda b, h: (h, 0, 0))],
        out_specs=tok,
        out_shape=jax.ShapeDtypeStruct((t, width), BF16),
        scratch_shapes=[pltpu.VMEM((seq + ATTN_LEAD, HEAD_DIM), BF16),
                        pltpu.VMEM((seq + ATTN_LEAD, HEAD_DIM), BF16)],
        compiler_params=pltpu.CompilerParams(
            dimension_semantics=("parallel", "parallel"),
            vmem_limit_bytes=_vmem_limit(
                4 * _nbytes((seq, HEAD_DIM), BF16),
                2 * _nbytes((seq + ATTN_LEAD, HEAD_DIM), BF16))),
        name="band_attn",
    )(q, k, v, bias_row.reshape(bias_row.shape[0], 1, ATTN_BIAS_LANES))


def _mix_up_kernel(oa_ref, ob_ref, wa_ref, wb_ref, ga_ref, gb_ref, o_ref):
    ya = _dot(oa_ref[...], wa_ref[...])
    yb = _dot(ob_ref[...], wb_ref[...])
    o_ref[...] = (ga_ref[...].astype(F32) * ya + gb_ref[...].astype(F32) * yb).astype(o_ref.dtype)


def mix_up(oa, ob, wa, wb, ga, gb, *, tm=1024, tn=1024):
    t, kdim = oa.shape
    d = wa.shape[1]
    pipelined = (2 * _nbytes((tm, kdim), BF16) + 2 * _nbytes((kdim, tn), BF16) + 3 * _nbytes((tm, tn), BF16))
    return pl.pallas_call(
        _mix_up_kernel,
        grid=(t // tm, d // tn),
        in_specs=[pl.BlockSpec((tm, kdim), lambda i, j: (i, 0)),
                  pl.BlockSpec((tm, kdim), lambda i, j: (i, 0)),
                  pl.BlockSpec((kdim, tn), lambda i, j: (0, j)),
                  pl.BlockSpec((kdim, tn), lambda i, j: (0, j)),
                  pl.BlockSpec((tm, tn), lambda i, j: (i, j)),
                  pl.BlockSpec((tm, tn), lambda i, j: (i, j))],
        out_specs=pl.BlockSpec((tm, tn), lambda i, j: (i, j)),
        out_shape=jax.ShapeDtypeStruct((t, d), BF16),
        compiler_params=pltpu.CompilerParams(
            dimension_semantics=("parallel", "arbitrary"),
            vmem_limit_bytes=_vmem_limit(pipelined, 4 * _nbytes((tm, tn), F32))),
        name="mix_up",
    )(oa, ob, wa, wb, ga, gb)


def _ffn(x, h, w1, w3, w2, post_g, next_g, emit_next, side=None):
    g, w2_b = glu_up(h, w1, w3, w2)
    if side is None:
        y, side_b = matmul_kres(g, w2_b), None
    else:
        y, side_b = matmul_kres(g, w2_b, side)
    x_new, h_next = resnorm(y, x, post_g, next_g, scale=MACARON_WEIGHT, emit_next=emit_next)
    return x_new, h_next, side_b


def kernel(x, ffn1_pre_g, ffn1_post_g, ffn1_w1, ffn1_w3, ffn1_w2, mix_pre_g, mix_post_g, w_in, b_gate,
           hgrn_lb_logits, hgrn_norm_g, rel_bias, w_up_a, w_up_b, w_out,
           ffn2_pre_g, ffn2_post_g, ffn2_w1, ffn2_w3, ffn2_w2):
    batch, seq, d = x.shape
    depth = ffn1_w1.shape[0]
    d_half = d // 2
    lower_bounds = jnp.cumsum(jax.nn.softmax(hgrn_lb_logits.astype(F32), axis=0), axis=0)
    xt = x.reshape(batch * seq, d)
    h = norm_cast(xt, ffn1_pre_g[0])
    for layer in range(depth):
        xt, h, w_in_b = _ffn(xt, h, ffn1_w1[layer], ffn1_w3[layer], ffn1_w2[layer],
                             ffn1_post_g[layer], mix_pre_g[layer], True, side=w_in[layer])

        qs, lf, kk, vv, gs, w_up_a_b = proj(h, w_in_b, [0, d_half, 2 * d_half, 3 * d_half], d_half,
                                            [lower_bounds[layer]], _hgrn_epilogue,
                                            [BF16, F32, BF16, BF16, BF16], w_up_a[layer])
        qb, kb, vb, w_up_b_b = proj(h, w_in_b, [4 * d_half, 5 * d_half, 6 * d_half], d_half,
                                    [], _attn_epilogue, [BF16, BF16, BF16], w_up_b[layer])
        ga, gb, w_out_b = proj(h, w_in_b, [7 * d_half, 7 * d_half + d], d,
                               [b_gate[layer, 0], b_gate[layer, 1]], _gate_epilogue, [BF16, BF16],
                               w_out[layer], tn=1024)

        oa = hgrn(qs, lf, kk, vv, gs, hgrn_norm_g[layer], batch=batch, seq=seq)
        ob = band_attn(qb, kb, vb, _attn_bias_row(rel_bias[layer].astype(F32)), batch=batch, seq=seq)

        m = mix_up(oa, ob, w_up_a_b, w_up_b_b, ga, gb)
        last = layer == depth - 1
        y = matmul_kres(m, w_out_b, tm=1024)
        xt, h = resnorm(y, xt, mix_post_g[layer], ffn2_pre_g[layer], scale=1.0, emit_next=True)
        next_pre = ffn1_pre_g[layer + 1] if not last else ffn2_pre_g[layer]
        xt, h, _ = _ffn(xt, h, ffn2_w1[layer], ffn2_w3[layer], ffn2_w2[layer],
                        ffn2_post_g[layer], next_pre, not last)
    return xt.reshape(batch, seq, d)
```

```python
import functools

import numpy as np
import jax
import jax.numpy as jnp
from jax import lax
from jax.experimental import pallas as pl
from jax.experimental.pallas import tpu as pltpu

F32 = jnp.float32
BF16 = jnp.bfloat16

NORM_EPS = 1e-6
LOG2_E = 1.4426950408889634
MACARON_WEIGHT = 0.5
CHUNK = 64
HEAD_DIM = 128
LANES = 128
F32_SUBLANES = 8
BF16_SUBLANES = 16
LEFT_CHUNKS = 8
MAX_REL = 256
MASK_VALUE = -1e30

MIB = 1024 * 1024
V7X_VMEM_BYTES = 64 * MIB
VMEM_LIMIT_CAP = V7X_VMEM_BYTES - 6 * MIB
VMEM_LIMIT_FLOOR = 32 * MIB
VMEM_SPILL_BYTES = 8 * MIB


def _vmem_limit(pipelined_bytes, resident_bytes=0):
    est = 2 * pipelined_bytes + resident_bytes + VMEM_SPILL_BYTES
    return int(min(max(est, VMEM_LIMIT_FLOOR), VMEM_LIMIT_CAP))


def _nbytes(shape, dtype):
    return int(np.prod(shape)) * jnp.dtype(dtype).itemsize


def _dot(a, b):
    return jnp.dot(a, b, preferred_element_type=F32)


def _dot_nt(a, b):
    return lax.dot_general(a, b, (((1,), (1,)), ((), ())), preferred_element_type=F32)


def _dot_tn(a, b):
    return lax.dot_general(a, b, (((0,), (0,)), ((), ())), preferred_element_type=F32)


def _silu(x):
    return x * jax.nn.sigmoid(x)


def _side_cast_specs(side, n_i, n_j):
    rows, cols = side.shape
    slab, rem = divmod(rows, n_i * n_j)
    assert rem == 0 and slab % BF16_SUBLANES == 0, (rows, n_i, n_j)
    spec = pl.BlockSpec((slab, cols), lambda i, j: (i * n_j + j, 0))
    return spec, jax.ShapeDtypeStruct((rows, cols), BF16), _nbytes((slab, cols), F32) + _nbytes((slab, cols), BF16)


def _norm_cast_kernel(x_ref, g_ref, o_ref):
    x = x_ref[...]
    inv = lax.rsqrt(jnp.mean(x * x, axis=-1, keepdims=True) + NORM_EPS)
    o_ref[...] = (x * inv * g_ref[...]).astype(o_ref.dtype)


def norm_cast(x, g, *, tm=256):
    t, d = x.shape
    return pl.pallas_call(
        _norm_cast_kernel,
        grid=(t // tm,),
        in_specs=[pl.BlockSpec((tm, d), lambda i: (i, 0)),
                  pl.BlockSpec((1, d), lambda i: (0, 0))],
        out_specs=pl.BlockSpec((tm, d), lambda i: (i, 0)),
        out_shape=jax.ShapeDtypeStruct((t, d), BF16),
        compiler_params=pltpu.CompilerParams(
            dimension_semantics=("parallel",),
            vmem_limit_bytes=_vmem_limit(_nbytes((tm, d), F32) + _nbytes((tm, d), BF16),
                                         3 * _nbytes((tm, d), F32))),
        name="norm_cast",
    )(x, g.reshape(1, d))


def _glu_up_kernel(h_ref, w1_ref, w3_ref, side_ref, o_ref, side_o_ref):
    h = h_ref[...]
    a = _dot(h, w1_ref[...].astype(BF16))
    b = _dot(h, w3_ref[...].astype(BF16))
    o_ref[...] = (_silu(a) * b).astype(o_ref.dtype)
    side_o_ref[...] = side_ref[...].astype(side_o_ref.dtype)


def glu_up(h, w1, w3, side, *, tm=2048, tn=256):
    t, d = h.shape
    f = w1.shape[1]
    grid = (t // tm, f // tn)
    side_spec, side_shape, side_bytes = _side_cast_specs(side, *grid)
    return pl.pallas_call(
        _glu_up_kernel,
        grid=grid,
        in_specs=[pl.BlockSpec((tm, d), lambda i, j: (i, 0), pipeline_mode=pl.Buffered(1)),
                  pl.BlockSpec((d, tn), lambda i, j: (0, j)),
                  pl.BlockSpec((d, tn), lambda i, j: (0, j)),
                  side_spec],
        out_specs=[pl.BlockSpec((tm, tn), lambda i, j: (i, j)), side_spec],
        out_shape=[jax.ShapeDtypeStruct((t, f), BF16), side_shape],
        compiler_params=pltpu.CompilerParams(
            dimension_semantics=("parallel", "arbitrary"),
            vmem_limit_bytes=_vmem_limit(
                2 * _nbytes((d, tn), w1.dtype) + _nbytes((tm, tn), BF16) + side_bytes,
                _nbytes((tm, d), BF16) + 4 * _nbytes((tm, tn), F32) + 2 * _nbytes((d, tn), BF16))),
        name="glu_up",
    )(h, w1, w3, side)


def _matmul_kernel(a_ref, w_ref, *rest):
    maybe_side_ref, o_ref, maybe_side_o_ref = rest if len(rest) == 3 else (None, rest[0], None)
    o_ref[...] = _dot(a_ref[...], w_ref[...]).astype(o_ref.dtype)
    if maybe_side_ref is not None:
        maybe_side_o_ref[...] = maybe_side_ref[...].astype(maybe_side_o_ref.dtype)


def matmul_kres(a, w, side=None, *, tm=512, tn=1024):
    t, kdim = a.shape
    n = w.shape[1]
    grid = (n // tn, t // tm)
    in_specs = [pl.BlockSpec((tm, kdim), lambda j, i: (i, 0)),
                pl.BlockSpec((kdim, tn), lambda j, i: (0, j), pipeline_mode=pl.Buffered(1))]
    out_specs = [pl.BlockSpec((tm, tn), lambda j, i: (i, j))]
    out_shape = [jax.ShapeDtypeStruct((t, n), BF16)]
    operands = [a, w]
    side_bytes = 0
    if side is not None:
        side_spec, side_shape, side_bytes = _side_cast_specs(side, *grid)
        in_specs.append(side_spec)
        out_specs.append(side_spec)
        out_shape.append(side_shape)
        operands.append(side)
    outs = pl.pallas_call(
        _matmul_kernel,
        grid=grid,
        in_specs=in_specs,
        out_specs=out_specs,
        out_shape=out_shape,
        compiler_params=pltpu.CompilerParams(
            dimension_semantics=("parallel", "arbitrary"),
            vmem_limit_bytes=_vmem_limit(
                _nbytes((tm, kdim), BF16) + _nbytes((tm, tn), BF16) + side_bytes,
                _nbytes((kdim, tn), BF16) + 2 * _nbytes((tm, tn), F32))),
        name="matmul_kres",
    )(*operands)
    return outs if side is not None else outs[0]


def _resnorm_kernel(y_ref, x_ref, gpost_ref, gnext_ref, xo_ref, *maybe_h_ref, scale):
    y = y_ref[...].astype(F32)
    inv = lax.rsqrt(jnp.mean(y * y, axis=-1, keepdims=True) + NORM_EPS)
    xn = x_ref[...] + scale * (y * inv * gpost_ref[...])
    xo_ref[...] = xn
    if maybe_h_ref:
        inv2 = lax.rsqrt(jnp.mean(xn * xn, axis=-1, keepdims=True) + NORM_EPS)
        maybe_h_ref[0][...] = (xn * inv2 * gnext_ref[...]).astype(BF16)


def resnorm(y, x, g_post, g_next, *, scale, emit_next, tm=256):
    t, d = x.shape
    row = pl.BlockSpec((tm, d), lambda i: (i, 0))
    vec = pl.BlockSpec((1, d), lambda i: (0, 0))
    out_shape = [jax.ShapeDtypeStruct((t, d), F32)]
    if emit_next:
        out_shape.append(jax.ShapeDtypeStruct((t, d), BF16))
    pipelined = (_nbytes((tm, d), y.dtype) + 2 * _nbytes((tm, d), F32)
                 + (_nbytes((tm, d), BF16) if emit_next else 0))
    outs = pl.pallas_call(
        functools.partial(_resnorm_kernel, scale=scale),
        grid=(t // tm,),
        in_specs=[row, row, vec, vec],
        out_specs=[row] * len(out_shape),
        out_shape=out_shape,
        compiler_params=pltpu.CompilerParams(
            dimension_semantics=("parallel",),
            vmem_limit_bytes=_vmem_limit(pipelined, 3 * _nbytes((tm, d), F32))),
        name="resnorm",
    )(y, x, g_post.reshape(1, d), g_next.reshape(1, d))
    return outs if emit_next else (outs[0], None)


def _proj_kernel(*refs, n_seg, n_aux, epilogue):
    h_ref = refs[0]
    w_refs = refs[1:1 + n_seg]
    aux_refs = refs[1 + n_seg:1 + n_seg + n_aux]
    side_ref = refs[1 + n_seg + n_aux]
    out_refs = refs[2 + n_seg + n_aux:-1]
    side_o_ref = refs[-1]
    h = h_ref[...]
    ys = [_dot(h, w_ref[...]) for w_ref in w_refs]
    outs = epilogue(ys, [r[...] for r in aux_refs])
    for o_ref, o in zip(out_refs, outs):
        o_ref[...] = o.astype(o_ref.dtype)
    side_o_ref[...] = side_ref[...].astype(side_o_ref.dtype)


def proj(h, w_in, col_starts, width, aux, epilogue, out_dtypes, side, *, tm=1024, tn=256):
    t, d = h.shape
    n_seg = len(col_starts)
    grid = (t // tm, width // tn)
    side_spec, side_shape, side_bytes = _side_cast_specs(side, *grid)

    def w_spec(start):
        off = start // tn
        return pl.BlockSpec((d, tn), lambda i, j: (0, off + j))

    pipelined = (_nbytes((tm, d), BF16) + n_seg * _nbytes((d, tn), BF16)
                 + sum(_nbytes((tm, tn), dt) for dt in out_dtypes) + side_bytes)
    return pl.pallas_call(
        functools.partial(_proj_kernel, n_seg=n_seg, n_aux=len(aux), epilogue=epilogue),
        grid=grid,
        in_specs=([pl.BlockSpec((tm, d), lambda i, j: (i, 0))]
                  + [w_spec(s) for s in col_starts]
                  + [pl.BlockSpec((1, tn), lambda i, j: (0, j)) for _ in aux]
                  + [side_spec]),
        out_specs=[pl.BlockSpec((tm, tn), lambda i, j: (i, j)) for _ in out_dtypes] + [side_spec],
        out_shape=[jax.ShapeDtypeStruct((t, width), dt) for dt in out_dtypes] + [side_shape],
        compiler_params=pltpu.CompilerParams(
            dimension_semantics=("parallel", "arbitrary"),
            vmem_limit_bytes=_vmem_limit(pipelined, (n_seg + 4) * _nbytes((tm, tn), F32))),
        name="proj",
    )(h, *([w_in] * n_seg), *[a.reshape(1, width) for a in aux], side)


def _hgrn_epilogue(ys, aux):
    q, fr, i, g = ys
    lb, = aux
    f = lb + (1.0 - lb) * jax.nn.sigmoid(fr)
    return _silu(q), jnp.log(f) * LOG2_E, 1.0 - f, i, _silu(g)


def _attn_epilogue(ys, aux):
    q, k, v = ys
    return q * (HEAD_DIM ** -0.5), k, v


def _gate_epilogue(ys, aux):
    ga, gb = ys
    ba, bb = aux
    return jax.nn.sigmoid(ga + ba), jax.nn.sigmoid(gb + bb)


def _lower_half_total(p_hb, level, row8):
    c, w = p_hb.shape
    hb = 1 << level
    if 2 * hb >= F32_SUBLANES:
        blk = min(2 * hb, c)
        return jnp.concatenate(
            [jnp.broadcast_to(p_hb[b * blk + hb - 1:b * blk + hb, :], (blk, w)) for b in range(c // blk)], axis=0)
    x = p_hb.reshape(c // F32_SUBLANES, F32_SUBLANES, w)
    odd = (row8 & 1) == 1
    if level == 0:
        g = jnp.where(odd, pltpu.roll(x, 1, 1), x)
    else:
        z = jnp.where(odd, x, pltpu.roll(x, 7, 1))
        g = jnp.where((row8 & 2) == 0, z, pltpu.roll(z, 2, 1))
    return g.reshape(c, w)


def _add_to_upper_halves(p_hb, g, level, upper_mask):
    c = p_hb.shape[0]
    hb = 1 << level
    if hb < F32_SUBLANES:
        return p_hb + jnp.where(upper_mask, g, 0.0)
    pieces = []
    for lo in range(0, c, 2 * hb):
        pieces += [p_hb[lo:lo + hb], p_hb[lo + hb:lo + 2 * hb] + g[lo + hb:lo + 2 * hb]]
    return jnp.concatenate(pieces, axis=0)


def _hgrn_kernel(qs_ref, lf_ref, kk_ref, v_ref, gs_ref, ng_ref, side_ref, o_ref, side_o_ref, st_ref, *,
                 heads_per_block, chunks_per_iter):
    c = CHUNK
    dk = HEAD_DIM
    w = qs_ref.shape[1]
    n_levels = CHUNK.bit_length() - 1

    @pl.when(pl.program_id(2) == 0)
    def _():
        st_ref[...] = jnp.zeros_like(st_ref)

    row = lax.broadcasted_iota(jnp.int32, (c, w), 0)
    row8 = lax.broadcasted_iota(jnp.int32, (c // F32_SUBLANES, F32_SUBLANES, w), 1)
    upper = [((row >> p) & 1) == 1 for p in range(n_levels)]
    ti = lax.broadcasted_iota(jnp.int32, (c, c), 0)
    si = lax.broadcasted_iota(jnp.int32, (c, c), 1)
    diag_mask = ti == si
    level_masks = [((ti >> (p + 1)) == (si >> (p + 1))) & (((ti >> p) & 1) == 1) & (((si >> p) & 1) == 0)
                   for p in range(n_levels)]

    def iter_body(it, carry):
        pre = []
        for cc in range(chunks_per_iter):
            rows = pl.ds(pl.multiple_of((it * chunks_per_iter + cc) * c, c), c)
            p_hb = lf_ref[rows, :]
            e_q, e_k = [], []
            for p in range(n_levels + 1):
                g = _lower_half_total(p_hb, p, row8)
                e_q.append(jnp.exp2(p_hb))
                e_k.append(None if p == 0 else jnp.exp2(g - p_hb))
                if p < n_levels:
                    p_hb = _add_to_upper_halves(p_hb, g, p, upper[p])
            pre.append((rows, e_q, e_k))
        indep = {}
        for cc, (rows, e_q, e_k) in enumerate(pre):
            for hh in range(heads_per_block):
                cols = slice(hh * dk, (hh + 1) * dk)
                qb = qs_ref[rows, cols]
                kb = kk_ref[rows, cols]
                v = v_ref[rows, cols]
                q = qb.astype(F32)
                k = kb.astype(F32)
                parts = [_dot_nt(qb, kb)]
                for p in range(n_levels):
                    qt = (q * e_q[p][:, cols]).astype(BF16)
                    kt = kb if p == 0 else (k * e_k[p][:, cols]).astype(BF16)
                    parts.append(_dot_nt(qt, kt))
                e_in = e_q[n_levels][:, cols]
                e_out = e_k[n_levels][:, cols]
                kv = _dot_tn(v, (k * e_out).astype(BF16))
                indep[cc, hh] = (parts, kv, (q * e_in).astype(BF16), v, e_in[c - 1:c, :])
        for hh in range(heads_per_block):
            cols = slice(hh * dk, (hh + 1) * dk)
            st = st_ref[hh]
            for cc, (rows, _, _) in enumerate(pre):
                parts, kv, q_in, v, decay_all = indep[cc, hh]
                scores = jnp.where(diag_mask, parts[0], 0.0)
                for p in range(n_levels):
                    scores = jnp.where(level_masks[p], parts[p + 1], scores)
                o = _dot(scores.astype(BF16), v) + _dot_nt(q_in, st.astype(BF16))
                st = st * decay_all + kv
                inv = lax.rsqrt(jnp.mean(o * o, axis=-1, keepdims=True) + NORM_EPS)
                o = o * inv * ng_ref[:, cols] * gs_ref[rows, cols].astype(F32)
                o_ref[rows, cols] = o.astype(o_ref.dtype)
            st_ref[hh] = st
        return carry

    lax.fori_loop(0, qs_ref.shape[0] // (c * chunks_per_iter), iter_body, 0)
    side_o_ref[...] = side_ref[...].astype(side_o_ref.dtype)


def hgrn(qs, lf, kk, v, gs, norm_g, side, *, batch, seq, block_len=1024, heads_per_block=4, chunks_per_iter=8):
    t, width = qs.shape
    bw = heads_per_block * HEAD_DIM
    n_h, n_l = width // bw, seq // block_len
    tok = pl.BlockSpec((block_len, bw), lambda b, h, l: (b * n_l + l, h))
    side_rows, side_cols = side.shape
    slab, rem = divmod(side_rows, batch * n_h * n_l)
    assert rem == 0 and slab % BF16_SUBLANES == 0, (side_rows, batch, n_h, n_l)
    side_spec = pl.BlockSpec((slab, side_cols), lambda b, h, l: ((b * n_h + h) * n_l + l, 0))
    pipelined = (4 * _nbytes((block_len, bw), BF16) + _nbytes((block_len, bw), F32)
                 + _nbytes((slab, side_cols), F32) + _nbytes((slab, side_cols), BF16))
    return pl.pallas_call(
        functools.partial(_hgrn_kernel, heads_per_block=heads_per_block, chunks_per_iter=chunks_per_iter),
        grid=(batch, n_h, n_l),
        in_specs=[tok, tok, tok, tok, tok,
                  pl.BlockSpec((1, bw), lambda b, h, l: (0, h)),
                  side_spec],
        out_specs=[tok, side_spec],
        out_shape=[jax.ShapeDtypeStruct((t, width), BF16), jax.ShapeDtypeStruct(side.shape, BF16)],
        scratch_shapes=[pltpu.VMEM((heads_per_block, HEAD_DIM, HEAD_DIM), F32)],
        compiler_params=pltpu.CompilerParams(
            dimension_semantics=("parallel", "parallel", "arbitrary"),
            vmem_limit_bytes=_vmem_limit(pipelined, 0)),
        name="hgrn",
    )(qs, lf, kk, v, gs, norm_g.reshape(1, width), side)


ATTN_Q_BLOCK = 2 * CHUNK
ATTN_WINDOW = 640
ATTN_LEAD = ATTN_WINDOW - ATTN_Q_BLOCK
ATTN_BIAS_LANES = pl.cdiv(ATTN_WINDOW + ATTN_Q_BLOCK - 1, LANES) * LANES


def _attn_bias_row(rel_bias):
    m = np.arange(ATTN_BIAS_LANES)
    m = np.where(m >= ATTN_WINDOW, m - ATTN_BIAS_LANES, m)
    idx = np.clip(ATTN_LEAD - m, -MAX_REL, MAX_REL) + MAX_REL
    runs, start = [], 0
    for pos in range(1, len(idx) + 1):
        if pos == len(idx) or idx[pos] != idx[pos - 1] - 1:
            runs.append((start, pos))
            start = pos
    pieces = [jnp.flip(rel_bias[:, int(idx[hi - 1]):int(idx[lo]) + 1], axis=1) for lo, hi in runs]
    return jnp.concatenate(pieces, axis=1)


def _attn_kernel(q_ref, k_ref, v_ref, brow_ref, o_ref, kpad_ref, vpad_ref, *, group):
    seq = q_ref.shape[0]
    kpad_ref[0:ATTN_LEAD, :] = jnp.zeros((ATTN_LEAD, HEAD_DIM), kpad_ref.dtype)
    vpad_ref[0:ATTN_LEAD, :] = jnp.zeros((ATTN_LEAD, HEAD_DIM), vpad_ref.dtype)
    kpad_ref[ATTN_LEAD:, :] = k_ref[...]
    vpad_ref[ATTN_LEAD:, :] = v_ref[...]
    brow = jnp.broadcast_to(brow_ref[0], (ATTN_Q_BLOCK, ATTN_BIAS_LANES))
    toep = pltpu.roll(brow, 0, 1, stride=1, stride_axis=0)[:, :ATTN_WINDOW]
    qi = lax.broadcasted_iota(jnp.int32, (ATTN_Q_BLOCK, ATTN_WINDOW), 0)
    col = lax.broadcasted_iota(jnp.int32, (ATTN_Q_BLOCK, ATTN_WINDOW), 1)
    qc = qi // CHUNK
    kc = col // CHUNK
    lead = ATTN_LEAD // CHUNK
    bias = jnp.where((kc >= qc + lead - LEFT_CHUNKS) & (kc <= qc + lead), toep, MASK_VALUE)

    def do_group(base, masked):
        starts = [base + u * ATTN_Q_BLOCK for u in range(group)]
        scores = []
        for start in starts:
            q = q_ref[pl.ds(start, ATTN_Q_BLOCK), :]
            kw = kpad_ref[pl.ds(start, ATTN_WINDOW), :]
            s = _dot_nt(q, kw) + bias
            if masked:
                s = jnp.where(col >= ATTN_LEAD - start, s, MASK_VALUE)
            scores.append(s)
        probs = []
        for s in scores:
            m = jnp.max(s, axis=-1, keepdims=True)
            p = jnp.exp(s - m)
            probs.append((p.astype(BF16), jnp.sum(p, axis=-1, keepdims=True)))
        for start, (p, denom) in zip(starts, probs):
            vw = vpad_ref[pl.ds(start, ATTN_WINDOW), :]
            o = _dot(p, vw) / denom
            o_ref[pl.ds(start, ATTN_Q_BLOCK), :] = o.astype(o_ref.dtype)

    span = group * ATTN_Q_BLOCK
    n_masked = pl.cdiv(ATTN_LEAD, span)
    for g in range(n_masked):
        do_group(g * span, True)

    def body(g, carry):
        do_group(pl.multiple_of(g * span, span), False)
        return carry

    lax.fori_loop(n_masked, seq // span, body, 0)


def band_attn(q, k, v, bias_row, *, batch, seq, group=16):
    t, width = q.shape
    tok = pl.BlockSpec((seq, HEAD_DIM), lambda b, h: (b, h))
    return pl.pallas_call(
        functools.partial(_attn_kernel, group=group),
        grid=(batch, width // HEAD_DIM),
        in_specs=[tok, tok, tok,
                  pl.BlockSpec((1, 1, ATTN_BIAS_LANES), lambda b, h: (h, 0, 0))],
        out_specs=tok,
        out_shape=jax.ShapeDtypeStruct((t, width), BF16),
        scratch_shapes=[pltpu.VMEM((seq + ATTN_LEAD, HEAD_DIM), BF16),
                        pltpu.VMEM((seq + ATTN_LEAD, HEAD_DIM), BF16)],
        compiler_params=pltpu.CompilerParams(
            dimension_semantics=("parallel", "parallel"),
            vmem_limit_bytes=_vmem_limit(
                4 * _nbytes((seq, HEAD_DIM), BF16),
                2 * _nbytes((seq + ATTN_LEAD, HEAD_DIM), BF16))),
        name="band_attn",
    )(q, k, v, bias_row.reshape(bias_row.shape[0], 1, ATTN_BIAS_LANES))


def _mix_up_kernel(oa_ref, ob_ref, wa_ref, wb_ref, ga_ref, gb_ref, o_ref):
    ya = _dot(oa_ref[...], wa_ref[...])
    yb = _dot(ob_ref[...], wb_ref[...])
    o_ref[...] = (ga_ref[...].astype(F32) * ya + gb_ref[...].astype(F32) * yb).astype(o_ref.dtype)


def mix_up(oa, ob, wa, wb, ga, gb, *, tm=1024, tn=1024):
    t, kdim = oa.shape
    d = wa.shape[1]
    pipelined = (2 * _nbytes((tm, kdim), BF16) + 2 * _nbytes((kdim, tn), BF16) + 3 * _nbytes((tm, tn), BF16))
    return pl.pallas_call(
        _mix_up_kernel,
        grid=(t // tm, d // tn),
        in_specs=[pl.BlockSpec((tm, kdim), lambda i, j: (i, 0)),
                  pl.BlockSpec((tm, kdim), lambda i, j: (i, 0)),
                  pl.BlockSpec((kdim, tn), lambda i, j: (0, j)),
                  pl.BlockSpec((kdim, tn), lambda i, j: (0, j)),
                  pl.BlockSpec((tm, tn), lambda i, j: (i, j)),
                  pl.BlockSpec((tm, tn), lambda i, j: (i, j))],
        out_specs=pl.BlockSpec((tm, tn), lambda i, j: (i, j)),
        out_shape=jax.ShapeDtypeStruct((t, d), BF16),
        compiler_params=pltpu.CompilerParams(
            dimension_semantics=("parallel", "arbitrary"),
            vmem_limit_bytes=_vmem_limit(pipelined, 4 * _nbytes((tm, tn), F32))),
        name="mix_up",
    )(oa, ob, wa, wb, ga, gb)


def _ffn(x, h, w1, w3, w2, post_g, next_g, emit_next, side=None, glu_tm=2048):
    g, w2_b = glu_up(h, w1, w3, w2, tm=glu_tm)
    if side is None:
        y, side_b = matmul_kres(g, w2_b), None
    else:
        y, side_b = matmul_kres(g, w2_b, side)
    x_new, h_next = resnorm(y, x, post_g, next_g, scale=MACARON_WEIGHT, emit_next=emit_next)
    return x_new, h_next, side_b


def kernel(x, ffn1_pre_g, ffn1_post_g, ffn1_w1, ffn1_w3, ffn1_w2, mix_pre_g, mix_post_g, w_in, b_gate,
           hgrn_lb_logits, hgrn_norm_g, rel_bias, w_up_a, w_up_b, w_out,
           ffn2_pre_g, ffn2_post_g, ffn2_w1, ffn2_w3, ffn2_w2):
    batch, seq, d = x.shape
    depth = ffn1_w1.shape[0]
    d_half = d // 2
    lower_bounds = jnp.cumsum(jax.nn.softmax(hgrn_lb_logits.astype(F32), axis=0), axis=0)
    xt = x.reshape(batch * seq, d)
    h = norm_cast(xt, ffn1_pre_g[0])
    for layer in range(depth):
        xt, h, w_in_b = _ffn(xt, h, ffn1_w1[layer], ffn1_w3[layer], ffn1_w2[layer],
                             ffn1_post_g[layer], mix_pre_g[layer], True, side=w_in[layer])

        qs, lf, kk, vv, gs, w_up_a_b = proj(h, w_in_b, [0, d_half, 2 * d_half, 3 * d_half], d_half,
                                            [lower_bounds[layer]], _hgrn_epilogue,
                                            [BF16, F32, BF16, BF16, BF16], w_up_a[layer])
        qb, kb, vb, w_up_b_b = proj(h, w_in_b, [4 * d_half, 5 * d_half, 6 * d_half], d_half,
                                    [], _attn_epilogue, [BF16, BF16, BF16], w_up_b[layer], tn=512)
        ga, gb, w_out_b = proj(h, w_in_b, [7 * d_half, 7 * d_half + d], d,
                               [b_gate[layer, 0], b_gate[layer, 1]], _gate_epilogue, [BF16, BF16],
                               w_out[layer], tn=512)

        oa, ffn2_w1_b = hgrn(qs, lf, kk, vv, gs, hgrn_norm_g[layer], ffn2_w1[layer], batch=batch, seq=seq)
        ob = band_attn(qb, kb, vb, _attn_bias_row(rel_bias[layer].astype(F32)), batch=batch, seq=seq)

        m = mix_up(oa, ob, w_up_a_b, w_up_b_b, ga, gb)
        last = layer == depth - 1
        y, ffn2_w3_b = matmul_kres(m, w_out_b, ffn2_w3[layer], tm=1024)
        xt, h = resnorm(y, xt, mix_post_g[layer], ffn2_pre_g[layer], scale=1.0, emit_next=True)
        next_pre = ffn1_pre_g[layer + 1] if not last else ffn2_pre_g[layer]
        xt, h, _ = _ffn(xt, h, ffn2_w1_b, ffn2_w3_b, ffn2_w2[layer],
                        ffn2_post_g[layer], next_pre, not last, glu_tm=4096)
    return xt.reshape(batch, seq, d)
```

```python
import functools

import numpy as np
import jax
import jax.numpy as jnp
from jax import lax
from jax.experimental import pallas as pl
from jax.experimental.pallas import tpu as pltpu

F32 = jnp.float32
BF16 = jnp.bfloat16

NORM_EPS = 1e-6
LOG2_E = 1.4426950408889634
MACARON_WEIGHT = 0.5
CHUNK = 64
HEAD_DIM = 128
LANES = 128
F32_SUBLANES = 8
BF16_SUBLANES = 16
LEFT_CHUNKS = 8
MAX_REL = 256
GLU_DOT_ROWS = 1024
MASK_VALUE = -1e30

MIB = 1024 * 1024
V7X_VMEM_BYTES = 64 * MIB
VMEM_LIMIT_CAP = V7X_VMEM_BYTES - 6 * MIB
VMEM_LIMIT_FLOOR = 32 * MIB
VMEM_SPILL_BYTES = 8 * MIB


def _vmem_limit(pipelined_bytes, resident_bytes=0):
    est = 2 * pipelined_bytes + resident_bytes + VMEM_SPILL_BYTES
    return int(min(max(est, VMEM_LIMIT_FLOOR), VMEM_LIMIT_CAP))


def _nbytes(shape, dtype):
    return int(np.prod(shape)) * jnp.dtype(dtype).itemsize


def _dot(a, b):
    return jnp.dot(a, b, preferred_element_type=F32)


def _dot_nt(a, b):
    return lax.dot_general(a, b, (((1,), (1,)), ((), ())), preferred_element_type=F32)


def _dot_tn(a, b):
    return lax.dot_general(a, b, (((0,), (0,)), ((), ())), preferred_element_type=F32)


def _silu(x):
    return x * jax.nn.sigmoid(x)


def _side_cast_specs(side, n_i, n_j):
    rows, cols = side.shape
    slab, rem = divmod(rows, n_i * n_j)
    assert rem == 0 and slab % BF16_SUBLANES == 0, (rows, n_i, n_j)
    spec = pl.BlockSpec((slab, cols), lambda i, j: (i * n_j + j, 0))
    return spec, jax.ShapeDtypeStruct((rows, cols), BF16), _nbytes((slab, cols), F32) + _nbytes((slab, cols), BF16)


def _norm_cast_kernel(x_ref, g_ref, o_ref):
    x = x_ref[...]
    inv = lax.rsqrt(jnp.mean(x * x, axis=-1, keepdims=True) + NORM_EPS)
    o_ref[...] = (x * inv * g_ref[...]).astype(o_ref.dtype)


def norm_cast(x, g, *, tm=256):
    t, d = x.shape
    return pl.pallas_call(
        _norm_cast_kernel,
        grid=(t // tm,),
        in_specs=[pl.BlockSpec((tm, d), lambda i: (i, 0)),
                  pl.BlockSpec((1, d), lambda i: (0, 0))],
        out_specs=pl.BlockSpec((tm, d), lambda i: (i, 0)),
        out_shape=jax.ShapeDtypeStruct((t, d), BF16),
        compiler_params=pltpu.CompilerParams(
            dimension_semantics=("parallel",),
            vmem_limit_bytes=_vmem_limit(_nbytes((tm, d), F32) + _nbytes((tm, d), BF16),
                                         3 * _nbytes((tm, d), F32))),
        name="norm_cast",
    )(x, g.reshape(1, d))


def _glu_up_kernel(h_ref, w1_ref, w3_ref, side_ref, o_ref, side_o_ref):
    w1 = w1_ref[...].astype(BF16)
    w3 = w3_ref[...].astype(BF16)
    for r in range(0, h_ref.shape[0], GLU_DOT_ROWS):
        h = h_ref[r:r + GLU_DOT_ROWS, :]
        o_ref[r:r + GLU_DOT_ROWS, :] = (_silu(_dot(h, w1)) * _dot(h, w3)).astype(o_ref.dtype)
    side_o_ref[...] = side_ref[...].astype(side_o_ref.dtype)


def glu_up(h, w1, w3, side, *, tm=2048, tn=256):
    t, d = h.shape
    f = w1.shape[1]
    grid = (t // tm, f // tn)
    side_spec, side_shape, side_bytes = _side_cast_specs(side, *grid)
    return pl.pallas_call(
        _glu_up_kernel,
        grid=grid,
        in_specs=[pl.BlockSpec((tm, d), lambda i, j: (i, 0), pipeline_mode=pl.Buffered(1)),
                  pl.BlockSpec((d, tn), lambda i, j: (0, j)),
                  pl.BlockSpec((d, tn), lambda i, j: (0, j)),
                  side_spec],
        out_specs=[pl.BlockSpec((tm, tn), lambda i, j: (i, j)), side_spec],
        out_shape=[jax.ShapeDtypeStruct((t, f), BF16), side_shape],
        compiler_params=pltpu.CompilerParams(
            dimension_semantics=("parallel", "arbitrary"),
            vmem_limit_bytes=_vmem_limit(
                2 * _nbytes((d, tn), w1.dtype) + _nbytes((tm, tn), BF16) + side_bytes,
                _nbytes((tm, d), BF16) + 4 * _nbytes((tm, tn), F32) + 2 * _nbytes((d, tn), BF16))),
        name="glu_up",
    )(h, w1, w3, side)


def _matmul_kernel(a_ref, w_ref, *rest):
    maybe_side_ref, o_ref, maybe_side_o_ref = rest if len(rest) == 3 else (None, rest[0], None)
    o_ref[...] = _dot(a_ref[...], w_ref[...]).astype(o_ref.dtype)
    if maybe_side_ref is not None:
        maybe_side_o_ref[...] = maybe_side_ref[...].astype(maybe_side_o_ref.dtype)


def matmul_kres(a, w, side=None, *, tm=512, tn=1024):
    t, kdim = a.shape
    n = w.shape[1]
    grid = (n // tn, t // tm)
    in_specs = [pl.BlockSpec((tm, kdim), lambda j, i: (i, 0)),
                pl.BlockSpec((kdim, tn), lambda j, i: (0, j), pipeline_mode=pl.Buffered(1))]
    out_specs = [pl.BlockSpec((tm, tn), lambda j, i: (i, j))]
    out_shape = [jax.ShapeDtypeStruct((t, n), BF16)]
    operands = [a, w]
    side_bytes = 0
    if side is not None:
        side_spec, side_shape, side_bytes = _side_cast_specs(side, *grid)
        in_specs.append(side_spec)
        out_specs.append(side_spec)
        out_shape.append(side_shape)
        operands.append(side)
    outs = pl.pallas_call(
        _matmul_kernel,
        grid=grid,
        in_specs=in_specs,
        out_specs=out_specs,
        out_shape=out_shape,
        compiler_params=pltpu.CompilerParams(
            dimension_semantics=("parallel", "arbitrary"),
            vmem_limit_bytes=_vmem_limit(
                _nbytes((tm, kdim), BF16) + _nbytes((tm, tn), BF16) + side_bytes,
                _nbytes((kdim, tn), BF16) + 2 * _nbytes((tm, tn), F32))),
        name="matmul_kres",
    )(*operands)
    return outs if side is not None else outs[0]


def _resnorm_kernel(y_ref, x_ref, gpost_ref, gnext_ref, xo_ref, *maybe_h_ref, scale):
    y = y_ref[...].astype(F32)
    inv = lax.rsqrt(jnp.mean(y * y, axis=-1, keepdims=True) + NORM_EPS)
    xn = x_ref[...] + scale * (y * inv * gpost_ref[...])
    xo_ref[...] = xn
    if maybe_h_ref:
        inv2 = lax.rsqrt(jnp.mean(xn * xn, axis=-1, keepdims=True) + NORM_EPS)
        maybe_h_ref[0][...] = (xn * inv2 * gnext_ref[...]).astype(BF16)


def resnorm(y, x, g_post, g_next, *, scale, emit_next, tm=256):
    t, d = x.shape
    row = pl.BlockSpec((tm, d), lambda i: (i, 0))
    vec = pl.BlockSpec((1, d), lambda i: (0, 0))
    out_shape = [jax.ShapeDtypeStruct((t, d), F32)]
    if emit_next:
        out_shape.append(jax.ShapeDtypeStruct((t, d), BF16))
    pipelined = (_nbytes((tm, d), y.dtype) + 2 * _nbytes((tm, d), F32)
                 + (_nbytes((tm, d), BF16) if emit_next else 0))
    outs = pl.pallas_call(
        functools.partial(_resnorm_kernel, scale=scale),
        grid=(t // tm,),
        in_specs=[row, row, vec, vec],
        out_specs=[row] * len(out_shape),
        out_shape=out_shape,
        compiler_params=pltpu.CompilerParams(
            dimension_semantics=("parallel",),
            vmem_limit_bytes=_vmem_limit(pipelined, 3 * _nbytes((tm, d), F32))),
        name="resnorm",
    )(y, x, g_post.reshape(1, d), g_next.reshape(1, d))
    return outs if emit_next else (outs[0], None)


def _proj_kernel(*refs, n_seg, n_aux, epilogue):
    h_ref = refs[0]
    w_refs = refs[1:1 + n_seg]
    aux_refs = refs[1 + n_seg:1 + n_seg + n_aux]
    side_ref = refs[1 + n_seg + n_aux]
    out_refs = refs[2 + n_seg + n_aux:-1]
    side_o_ref = refs[-1]
    h = h_ref[...]
    ys = [_dot(h, w_ref[...]) for w_ref in w_refs]
    outs = epilogue(ys, [r[...] for r in aux_refs])
    for o_ref, o in zip(out_refs, outs):
        o_ref[...] = o.astype(o_ref.dtype)
    side_o_ref[...] = side_ref[...].astype(side_o_ref.dtype)


def proj(h, w_in, col_starts, width, aux, epilogue, out_dtypes, side, *, tm=1024, tn=256):
    t, d = h.shape
    n_seg = len(col_starts)
    grid = (t // tm, width // tn)
    side_spec, side_shape, side_bytes = _side_cast_specs(side, *grid)

    def w_spec(start):
        off = start // tn
        return pl.BlockSpec((d, tn), lambda i, j: (0, off + j))

    pipelined = (_nbytes((tm, d), BF16) + n_seg * _nbytes((d, tn), BF16)
                 + sum(_nbytes((tm, tn), dt) for dt in out_dtypes) + side_bytes)
    return pl.pallas_call(
        functools.partial(_proj_kernel, n_seg=n_seg, n_aux=len(aux), epilogue=epilogue),
        grid=grid,
        in_specs=([pl.BlockSpec((tm, d), lambda i, j: (i, 0))]
                  + [w_spec(s) for s in col_starts]
                  + [pl.BlockSpec((1, tn), lambda i, j: (0, j)) for _ in aux]
                  + [side_spec]),
        out_specs=[pl.BlockSpec((tm, tn), lambda i, j: (i, j)) for _ in out_dtypes] + [side_spec],
        out_shape=[jax.ShapeDtypeStruct((t, width), dt) for dt in out_dtypes] + [side_shape],
        compiler_params=pltpu.CompilerParams(
            dimension_semantics=("parallel", "arbitrary"),
            vmem_limit_bytes=_vmem_limit(pipelined, (n_seg + 4) * _nbytes((tm, tn), F32))),
        name="proj",
    )(h, *([w_in] * n_seg), *[a.reshape(1, width) for a in aux], side)


def _hgrn_epilogue(ys, aux):
    q, fr, i, g = ys
    lb, = aux
    f = lb + (1.0 - lb) * jax.nn.sigmoid(fr)
    return _silu(q), jnp.log(f) * LOG2_E, 1.0 - f, i, _silu(g)


def _attn_epilogue(ys, aux):
    q, k, v = ys
    return q * (HEAD_DIM ** -0.5), k, v


def _gate_epilogue(ys, aux):
    ga, gb = ys
    ba, bb = aux
    return jax.nn.sigmoid(ga + ba), jax.nn.sigmoid(gb + bb)


def _lower_half_total(p_hb, level, row8):
    c, w = p_hb.shape
    hb = 1 << level
    if 2 * hb >= F32_SUBLANES:
        blk = min(2 * hb, c)
        return jnp.concatenate(
            [jnp.broadcast_to(p_hb[b * blk + hb - 1:b * blk + hb, :], (blk, w)) for b in range(c // blk)], axis=0)
    x = p_hb.reshape(c // F32_SUBLANES, F32_SUBLANES, w)
    odd = (row8 & 1) == 1
    if level == 0:
        g = jnp.where(odd, pltpu.roll(x, 1, 1), x)
    else:
        z = jnp.where(odd, x, pltpu.roll(x, 7, 1))
        g = jnp.where((row8 & 2) == 0, z, pltpu.roll(z, 2, 1))
    return g.reshape(c, w)


def _add_to_upper_halves(p_hb, g, level, upper_mask):
    c = p_hb.shape[0]
    hb = 1 << level
    if hb < F32_SUBLANES:
        return p_hb + jnp.where(upper_mask, g, 0.0)
    pieces = []
    for lo in range(0, c, 2 * hb):
        pieces += [p_hb[lo:lo + hb], p_hb[lo + hb:lo + 2 * hb] + g[lo + hb:lo + 2 * hb]]
    return jnp.concatenate(pieces, axis=0)


def _hgrn_kernel(qs_ref, lf_ref, kk_ref, v_ref, gs_ref, ng_ref, side_ref, o_ref, side_o_ref, st_ref, *,
                 heads_per_block, chunks_per_iter):
    c = CHUNK
    dk = HEAD_DIM
    w = qs_ref.shape[1]
    n_levels = CHUNK.bit_length() - 1

    @pl.when(pl.program_id(2) == 0)
    def _():
        st_ref[...] = jnp.zeros_like(st_ref)

    row = lax.broadcasted_iota(jnp.int32, (c, w), 0)
    row8 = lax.broadcasted_iota(jnp.int32, (c // F32_SUBLANES, F32_SUBLANES, w), 1)
    upper = [((row >> p) & 1) == 1 for p in range(n_levels)]
    ti = lax.broadcasted_iota(jnp.int32, (c, c), 0)
    si = lax.broadcasted_iota(jnp.int32, (c, c), 1)
    diag_mask = ti == si
    level_masks = [((ti >> (p + 1)) == (si >> (p + 1))) & (((ti >> p) & 1) == 1) & (((si >> p) & 1) == 0)
                   for p in range(n_levels)]

    def iter_body(it, carry):
        pre = []
        for cc in range(chunks_per_iter):
            rows = pl.ds(pl.multiple_of((it * chunks_per_iter + cc) * c, c), c)
            p_hb = lf_ref[rows, :]
            e_q, e_k = [], []
            for p in range(n_levels + 1):
                g = _lower_half_total(p_hb, p, row8)
                e_q.append(jnp.exp2(p_hb))
                e_k.append(None if p == 0 else jnp.exp2(g - p_hb))
                if p < n_levels:
                    p_hb = _add_to_upper_halves(p_hb, g, p, upper[p])
            pre.append((rows, e_q, e_k))
        indep = {}
        for cc, (rows, e_q, e_k) in enumerate(pre):
            for hh in range(heads_per_block):
                cols = slice(hh * dk, (hh + 1) * dk)
                qb = qs_ref[rows, cols]
                kb = kk_ref[rows, cols]
                v = v_ref[rows, cols]
                q = qb.astype(F32)
                k = kb.astype(F32)
                parts = [_dot_nt(qb, kb)]
                for p in range(n_levels):
                    qt = (q * e_q[p][:, cols]).astype(BF16)
                    kt = kb if p == 0 else (k * e_k[p][:, cols]).astype(BF16)
                    parts.append(_dot_nt(qt, kt))
                e_in = e_q[n_levels][:, cols]
                e_out = e_k[n_levels][:, cols]
                kv = _dot_tn(v, (k * e_out).astype(BF16))
                indep[cc, hh] = (parts, kv, (q * e_in).astype(BF16), v, e_in[c - 1:c, :])
        for hh in range(heads_per_block):
            cols = slice(hh * dk, (hh + 1) * dk)
            st = st_ref[hh]
            for cc, (rows, _, _) in enumerate(pre):
                parts, kv, q_in, v, decay_all = indep[cc, hh]
                scores = jnp.where(diag_mask, parts[0], 0.0)
                for p in range(n_levels):
                    scores = jnp.where(level_masks[p], parts[p + 1], scores)
                o = _dot(scores.astype(BF16), v) + _dot_nt(q_in, st.astype(BF16))
                st = st * decay_all + kv
                inv = lax.rsqrt(jnp.mean(o * o, axis=-1, keepdims=True) + NORM_EPS)
                o = o * inv * ng_ref[:, cols] * gs_ref[rows, cols].astype(F32)
                o_ref[rows, cols] = o.astype(o_ref.dtype)
            st_ref[hh] = st
        return carry

    lax.fori_loop(0, qs_ref.shape[0] // (c * chunks_per_iter), iter_body, 0)
    side_o_ref[...] = side_ref[...].astype(side_o_ref.dtype)


def hgrn(qs, lf, kk, v, gs, norm_g, side, *, batch, seq, block_len=1024, heads_per_block=4, chunks_per_iter=8):
    t, width = qs.shape
    bw = heads_per_block * HEAD_DIM
    n_h, n_l = width // bw, seq // block_len
    tok = pl.BlockSpec((block_len, bw), lambda b, h, l: (b * n_l + l, h))
    side_rows, side_cols = side.shape
    slab, rem = divmod(side_rows, batch * n_h * n_l)
    assert rem == 0 and slab % BF16_SUBLANES == 0, (side_rows, batch, n_h, n_l)
    side_spec = pl.BlockSpec((slab, side_cols), lambda b, h, l: ((b * n_h + h) * n_l + l, 0))
    pipelined = (4 * _nbytes((block_len, bw), BF16) + _nbytes((block_len, bw), F32)
                 + _nbytes((slab, side_cols), F32) + _nbytes((slab, side_cols), BF16))
    return pl.pallas_call(
        functools.partial(_hgrn_kernel, heads_per_block=heads_per_block, chunks_per_iter=chunks_per_iter),
        grid=(batch, n_h, n_l),
        in_specs=[tok, tok, tok, tok, tok,
                  pl.BlockSpec((1, bw), lambda b, h, l: (0, h)),
                  side_spec],
        out_specs=[tok, side_spec],
        out_shape=[jax.ShapeDtypeStruct((t, width), BF16), jax.ShapeDtypeStruct(side.shape, BF16)],
        scratch_shapes=[pltpu.VMEM((heads_per_block, HEAD_DIM, HEAD_DIM), F32)],
        compiler_params=pltpu.CompilerParams(
            dimension_semantics=("parallel", "parallel", "arbitrary"),
            vmem_limit_bytes=_vmem_limit(pipelined, 0)),
        name="hgrn",
    )(qs, lf, kk, v, gs, norm_g.reshape(1, width), side)


ATTN_Q_BLOCK = 2 * CHUNK
ATTN_WINDOW = 640
ATTN_LEAD = ATTN_WINDOW - ATTN_Q_BLOCK
ATTN_BIAS_LANES = pl.cdiv(ATTN_WINDOW + ATTN_Q_BLOCK - 1, LANES) * LANES


def _attn_bias_row(rel_bias):
    m = np.arange(ATTN_BIAS_LANES)
    m = np.where(m >= ATTN_WINDOW, m - ATTN_BIAS_LANES, m)
    idx = np.clip(ATTN_LEAD - m, -MAX_REL, MAX_REL) + MAX_REL
    runs, start = [], 0
    for pos in range(1, len(idx) + 1):
        if pos == len(idx) or idx[pos] != idx[pos - 1] - 1:
            runs.append((start, pos))
            start = pos
    pieces = [jnp.flip(rel_bias[:, int(idx[hi - 1]):int(idx[lo]) + 1], axis=1) for lo, hi in runs]
    return jnp.concatenate(pieces, axis=1)


def _attn_kernel(q_ref, k_ref, v_ref, brow_ref, o_ref, kpad_ref, vpad_ref, *, group):
    seq = q_ref.shape[0]
    kpad_ref[0:ATTN_LEAD, :] = jnp.zeros((ATTN_LEAD, HEAD_DIM), kpad_ref.dtype)
    vpad_ref[0:ATTN_LEAD, :] = jnp.zeros((ATTN_LEAD, HEAD_DIM), vpad_ref.dtype)
    kpad_ref[ATTN_LEAD:, :] = k_ref[...]
    vpad_ref[ATTN_LEAD:, :] = v_ref[...]
    brow = jnp.broadcast_to(brow_ref[0], (ATTN_Q_BLOCK, ATTN_BIAS_LANES))
    toep = pltpu.roll(brow, 0, 1, stride=1, stride_axis=0)[:, :ATTN_WINDOW]
    qi = lax.broadcasted_iota(jnp.int32, (ATTN_Q_BLOCK, ATTN_WINDOW), 0)
    col = lax.broadcasted_iota(jnp.int32, (ATTN_Q_BLOCK, ATTN_WINDOW), 1)
    qc = qi // CHUNK
    kc = col // CHUNK
    lead = ATTN_LEAD // CHUNK
    bias = jnp.where((kc >= qc + lead - LEFT_CHUNKS) & (kc <= qc + lead), toep, MASK_VALUE)

    def do_group(base, masked):
        starts = [base + u * ATTN_Q_BLOCK for u in range(group)]
        scores = []
        for start in starts:
            q = q_ref[pl.ds(start, ATTN_Q_BLOCK), :]
            kw = kpad_ref[pl.ds(start, ATTN_WINDOW), :]
            s = _dot_nt(q, kw) + bias
            if masked:
                s = jnp.where(col >= ATTN_LEAD - start, s, MASK_VALUE)
            scores.append(s)
        probs = []
        for s in scores:
            m = jnp.max(s, axis=-1, keepdims=True)
            p = jnp.exp(s - m)
            probs.append((p.astype(BF16), jnp.sum(p, axis=-1, keepdims=True)))
        for start, (p, denom) in zip(starts, probs):
            vw = vpad_ref[pl.ds(start, ATTN_WINDOW), :]
            o = _dot(p, vw) / denom
            o_ref[pl.ds(start, ATTN_Q_BLOCK), :] = o.astype(o_ref.dtype)

    span = group * ATTN_Q_BLOCK
    n_masked = pl.cdiv(ATTN_LEAD, span)
    for g in range(n_masked):
        do_group(g * span, True)

    def body(g, carry):
        do_group(pl.multiple_of(g * span, span), False)
        return carry

    lax.fori_loop(n_masked, seq // span, body, 0)


def band_attn(q, k, v, bias_row, *, batch, seq, group=16):
    t, width = q.shape
    tok = pl.BlockSpec((seq, HEAD_DIM), lambda b, h: (b, h))
    return pl.pallas_call(
        functools.partial(_attn_kernel, group=group),
        grid=(batch, width // HEAD_DIM),
        in_specs=[tok, tok, tok,
                  pl.BlockSpec((1, 1, ATTN_BIAS_LANES), lambda b, h: (h, 0, 0))],
        out_specs=tok,
        out_shape=jax.ShapeDtypeStruct((t, width), BF16),
        scratch_shapes=[pltpu.VMEM((seq + ATTN_LEAD, HEAD_DIM), BF16),
                        pltpu.VMEM((seq + ATTN_LEAD, HEAD_DIM), BF16)],
        compiler_params=pltpu.CompilerParams(
            dimension_semantics=("parallel", "parallel"),
            vmem_limit_bytes=_vmem_limit(
                4 * _nbytes((seq, HEAD_DIM), BF16),
                2 * _nbytes((seq + ATTN_LEAD, HEAD_DIM), BF16))),
        name="band_attn",
    )(q, k, v, bias_row.reshape(bias_row.shape[0], 1, ATTN_BIAS_LANES))


def _mix_up_kernel(oa_ref, ob_ref, wa_ref, wb_ref, ga_ref, gb_ref, o_ref):
    ya = _dot(oa_ref[...], wa_ref[...])
    yb = _dot(ob_ref[...], wb_ref[...])
    o_ref[...] = (ga_ref[...].astype(F32) * ya + gb_ref[...].astype(F32) * yb).astype(o_ref.dtype)


def mix_up(oa, ob, wa, wb, ga, gb, *, tm=1024, tn=1024):
    t, kdim = oa.shape
    d = wa.shape[1]
    pipelined = (2 * _nbytes((tm, kdim), BF16) + 2 * _nbytes((kdim, tn), BF16) + 3 * _nbytes((tm, tn), BF16))
    return pl.pallas_call(
        _mix_up_kernel,
        grid=(t // tm, d // tn),
        in_specs=[pl.BlockSpec((tm, kdim), lambda i, j: (i, 0)),
                  pl.BlockSpec((tm, kdim), lambda i, j: (i, 0)),
                  pl.BlockSpec((kdim, tn), lambda i, j: (0, j)),
                  pl.BlockSpec((kdim, tn), lambda i, j: (0, j)),
                  pl.BlockSpec((tm, tn), lambda i, j: (i, j)),
                  pl.BlockSpec((tm, tn), lambda i, j: (i, j))],
        out_specs=pl.BlockSpec((tm, tn), lambda i, j: (i, j)),
        out_shape=jax.ShapeDtypeStruct((t, d), BF16),
        compiler_params=pltpu.CompilerParams(
            dimension_semantics=("parallel", "arbitrary"),
            vmem_limit_bytes=_vmem_limit(pipelined, 4 * _nbytes((tm, tn), F32))),
        name="mix_up",
    )(oa, ob, wa, wb, ga, gb)


def _ffn(x, h, w1, w3, w2, post_g, next_g, emit_next, side=None, glu_tm=2048):
    g, w2_b = glu_up(h, w1, w3, w2, tm=glu_tm)
    if side is None:
        y, side_b = matmul_kres(g, w2_b), None
    else:
        y, side_b = matmul_kres(g, w2_b, side)
    x_new, h_next = resnorm(y, x, post_g, next_g, scale=MACARON_WEIGHT, emit_next=emit_next)
    return x_new, h_next, side_b


def kernel(x, ffn1_pre_g, ffn1_post_g, ffn1_w1, ffn1_w3, ffn1_w2, mix_pre_g, mix_post_g, w_in, b_gate,
           hgrn_lb_logits, hgrn_norm_g, rel_bias, w_up_a, w_up_b, w_out,
           ffn2_pre_g, ffn2_post_g, ffn2_w1, ffn2_w3, ffn2_w2):
    batch, seq, d = x.shape
    depth = ffn1_w1.shape[0]
    d_half = d // 2
    lower_bounds = jnp.cumsum(jax.nn.softmax(hgrn_lb_logits.astype(F32), axis=0), axis=0)
    xt = x.reshape(batch * seq, d)
    h = norm_cast(xt, ffn1_pre_g[0])
    for layer in range(depth):
        xt, h, w_in_b = _ffn(xt, h, ffn1_w1[layer], ffn1_w3[layer], ffn1_w2[layer],
                             ffn1_post_g[layer], mix_pre_g[layer], True, side=w_in[layer])

        qs, lf, kk, vv, gs, w_up_a_b = proj(h, w_in_b, [0, d_half, 2 * d_half, 3 * d_half], d_half,
                                            [lower_bounds[layer]], _hgrn_epilogue,
                                            [BF16, F32, BF16, BF16, BF16], w_up_a[layer])
        qb, kb, vb, w_up_b_b = proj(h, w_in_b, [4 * d_half, 5 * d_half, 6 * d_half], d_half,
                                    [], _attn_epilogue, [BF16, BF16, BF16], w_up_b[layer], tn=512)
        ga, gb, w_out_b = proj(h, w_in_b, [7 * d_half, 7 * d_half + d], d,
                               [b_gate[layer, 0], b_gate[layer, 1]], _gate_epilogue, [BF16, BF16],
                               w_out[layer], tn=512)

        oa, ffn2_w1_b = hgrn(qs, lf, kk, vv, gs, hgrn_norm_g[layer], ffn2_w1[layer], batch=batch, seq=seq)
        ob = band_attn(qb, kb, vb, _attn_bias_row(rel_bias[layer].astype(F32)), batch=batch, seq=seq)

        m = mix_up(oa, ob, w_up_a_b, w_up_b_b, ga, gb)
        last = layer == depth - 1
        y, ffn2_w3_b = matmul_kres(m, w_out_b, ffn2_w3[layer], tm=1024)
        xt, h = resnorm(y, xt, mix_post_g[layer], ffn2_pre_g[layer], scale=1.0, emit_next=True)
        next_pre = ffn1_pre_g[layer + 1] if not last else ffn2_pre_g[layer]
        xt, h, _ = _ffn(xt, h, ffn2_w1_b, ffn2_w3_b, ffn2_w2[layer],
                        ffn2_post_g[layer], next_pre, not last, glu_tm=4096)
    return xt.reshape(batch, seq, d)
```

```python
import functools

import numpy as np
import jax
import jax.numpy as jnp
from jax import lax
from jax.experimental import pallas as pl
from jax.experimental.pallas import tpu as pltpu

F32 = jnp.float32
BF16 = jnp.bfloat16

NORM_EPS = 1e-6
LOG2_E = 1.4426950408889634
MACARON_WEIGHT = 0.5
CHUNK = 64
HEAD_DIM = 128
LANES = 128
F32_SUBLANES = 8
BF16_SUBLANES = 16
LEFT_CHUNKS = 8
MAX_REL = 256
GLU_DOT_ROWS = 1024
MASK_VALUE = -1e30

MIB = 1024 * 1024
V7X_VMEM_BYTES = 64 * MIB
VMEM_LIMIT_CAP = V7X_VMEM_BYTES - 2 * MIB
VMEM_LIMIT_FLOOR = 32 * MIB
VMEM_SPILL_BYTES = 8 * MIB


def _vmem_limit(pipelined_bytes, resident_bytes=0):
    est = 2 * pipelined_bytes + resident_bytes + VMEM_SPILL_BYTES
    return int(min(max(est, VMEM_LIMIT_FLOOR), VMEM_LIMIT_CAP))


def _nbytes(shape, dtype):
    return int(np.prod(shape)) * jnp.dtype(dtype).itemsize


def _dot(a, b):
    return jnp.dot(a, b, preferred_element_type=F32)


def _dot_nt(a, b):
    return lax.dot_general(a, b, (((1,), (1,)), ((), ())), preferred_element_type=F32)


def _dot_tn(a, b):
    return lax.dot_general(a, b, (((0,), (0,)), ((), ())), preferred_element_type=F32)


def _silu(x):
    return x * jax.nn.sigmoid(x)


def _side_cast_specs(side, n_i, n_j):
    rows, cols = side.shape
    slab, rem = divmod(rows, n_i * n_j)
    assert rem == 0 and slab % BF16_SUBLANES == 0, (rows, n_i, n_j)
    spec = pl.BlockSpec((slab, cols), lambda i, j: (i * n_j + j, 0))
    return spec, jax.ShapeDtypeStruct((rows, cols), BF16), _nbytes((slab, cols), F32) + _nbytes((slab, cols), BF16)


def _norm_cast_kernel(x_ref, g_ref, o_ref):
    x = x_ref[...]
    inv = lax.rsqrt(jnp.mean(x * x, axis=-1, keepdims=True) + NORM_EPS)
    o_ref[...] = (x * inv * g_ref[...]).astype(o_ref.dtype)


def norm_cast(x, g, *, tm=256):
    t, d = x.shape
    return pl.pallas_call(
        _norm_cast_kernel,
        grid=(t // tm,),
        in_specs=[pl.BlockSpec((tm, d), lambda i: (i, 0)),
                  pl.BlockSpec((1, d), lambda i: (0, 0))],
        out_specs=pl.BlockSpec((tm, d), lambda i: (i, 0)),
        out_shape=jax.ShapeDtypeStruct((t, d), BF16),
        compiler_params=pltpu.CompilerParams(
            dimension_semantics=("parallel",),
            vmem_limit_bytes=_vmem_limit(_nbytes((tm, d), F32) + _nbytes((tm, d), BF16),
                                         3 * _nbytes((tm, d), F32))),
        name="norm_cast",
    )(x, g.reshape(1, d))


def _glu_up_kernel(h_ref, w1_ref, w3_ref, side_ref, o_ref, side_o_ref):
    w1 = w1_ref[...].astype(BF16)
    w3 = w3_ref[...].astype(BF16)
    for r in range(0, h_ref.shape[0], GLU_DOT_ROWS):
        h = h_ref[r:r + GLU_DOT_ROWS, :]
        o_ref[r:r + GLU_DOT_ROWS, :] = (_silu(_dot(h, w1)) * _dot(h, w3)).astype(o_ref.dtype)
    side_o_ref[...] = side_ref[...].astype(side_o_ref.dtype)


def glu_up(h, w1, w3, side, *, tm=2048, tn=256, h_buffers=1):
    t, d = h.shape
    f = w1.shape[1]
    grid = (t // tm, f // tn)
    side_spec, side_shape, side_bytes = _side_cast_specs(side, *grid)
    return pl.pallas_call(
        _glu_up_kernel,
        grid=grid,
        in_specs=[pl.BlockSpec((tm, d), lambda i, j: (i, 0), pipeline_mode=pl.Buffered(h_buffers)),
                  pl.BlockSpec((d, tn), lambda i, j: (0, j)),
                  pl.BlockSpec((d, tn), lambda i, j: (0, j)),
                  side_spec],
        out_specs=[pl.BlockSpec((tm, tn), lambda i, j: (i, j)), side_spec],
        out_shape=[jax.ShapeDtypeStruct((t, f), BF16), side_shape],
        compiler_params=pltpu.CompilerParams(
            dimension_semantics=("parallel", "arbitrary"),
            vmem_limit_bytes=_vmem_limit(
                2 * _nbytes((d, tn), w1.dtype) + _nbytes((tm, tn), BF16) + side_bytes,
                h_buffers * _nbytes((tm, d), BF16) + 4 * _nbytes((GLU_DOT_ROWS, tn), F32)
                + 2 * _nbytes((d, tn), BF16))),
        name="glu_up",
    )(h, w1, w3, side)


def _matmul_kernel(a_ref, w_ref, *rest):
    maybe_side_ref, o_ref, maybe_side_o_ref = rest if len(rest) == 3 else (None, rest[0], None)
    o_ref[...] = _dot(a_ref[...], w_ref[...]).astype(o_ref.dtype)
    if maybe_side_ref is not None:
        maybe_side_o_ref[...] = maybe_side_ref[...].astype(maybe_side_o_ref.dtype)


def matmul_kres(a, w, side=None, *, tm=512, tn=1024):
    t, kdim = a.shape
    n = w.shape[1]
    grid = (n // tn, t // tm)
    in_specs = [pl.BlockSpec((tm, kdim), lambda j, i: (i, 0)),
                pl.BlockSpec((kdim, tn), lambda j, i: (0, j), pipeline_mode=pl.Buffered(1))]
    out_specs = [pl.BlockSpec((tm, tn), lambda j, i: (i, j))]
    out_shape = [jax.ShapeDtypeStruct((t, n), BF16)]
    operands = [a, w]
    side_bytes = 0
    if side is not None:
        side_spec, side_shape, side_bytes = _side_cast_specs(side, *grid)
        in_specs.append(side_spec)
        out_specs.append(side_spec)
        out_shape.append(side_shape)
        operands.append(side)
    outs = pl.pallas_call(
        _matmul_kernel,
        grid=grid,
        in_specs=in_specs,
        out_specs=out_specs,
        out_shape=out_shape,
        compiler_params=pltpu.CompilerParams(
            dimension_semantics=("parallel", "arbitrary"),
            vmem_limit_bytes=_vmem_limit(
                _nbytes((tm, kdim), BF16) + _nbytes((tm, tn), BF16) + side_bytes,
                _nbytes((kdim, tn), BF16) + 2 * _nbytes((tm, tn), F32))),
        name="matmul_kres",
    )(*operands)
    return outs if side is not None else outs[0]


def _resnorm_kernel(y_ref, x_ref, gpost_ref, gnext_ref, xo_ref, *maybe_h_ref, scale):
    y = y_ref[...].astype(F32)
    inv = lax.rsqrt(jnp.mean(y * y, axis=-1, keepdims=True) + NORM_EPS)
    xn = x_ref[...] + scale * (y * inv * gpost_ref[...])
    xo_ref[...] = xn
    if maybe_h_ref:
        inv2 = lax.rsqrt(jnp.mean(xn * xn, axis=-1, keepdims=True) + NORM_EPS)
        maybe_h_ref[0][...] = (xn * inv2 * gnext_ref[...]).astype(BF16)


def resnorm(y, x, g_post, g_next, *, scale, emit_next, tm=256):
    t, d = x.shape
    row = pl.BlockSpec((tm, d), lambda i: (i, 0))
    vec = pl.BlockSpec((1, d), lambda i: (0, 0))
    out_shape = [jax.ShapeDtypeStruct((t, d), F32)]
    if emit_next:
        out_shape.append(jax.ShapeDtypeStruct((t, d), BF16))
    pipelined = (_nbytes((tm, d), y.dtype) + 2 * _nbytes((tm, d), F32)
                 + (_nbytes((tm, d), BF16) if emit_next else 0))
    outs = pl.pallas_call(
        functools.partial(_resnorm_kernel, scale=scale),
        grid=(t // tm,),
        in_specs=[row, row, vec, vec],
        out_specs=[row] * len(out_shape),
        out_shape=out_shape,
        compiler_params=pltpu.CompilerParams(
            dimension_semantics=("parallel",),
            vmem_limit_bytes=_vmem_limit(pipelined, 3 * _nbytes((tm, d), F32))),
        name="resnorm",
    )(y, x, g_post.reshape(1, d), g_next.reshape(1, d))
    return outs if emit_next else (outs[0], None)


def _proj_kernel(*refs, n_seg, n_aux, epilogue):
    h_ref = refs[0]
    w_refs = refs[1:1 + n_seg]
    aux_refs = refs[1 + n_seg:1 + n_seg + n_aux]
    side_ref = refs[1 + n_seg + n_aux]
    out_refs = refs[2 + n_seg + n_aux:-1]
    side_o_ref = refs[-1]
    h = h_ref[...]
    ys = [_dot(h, w_ref[...]) for w_ref in w_refs]
    outs = epilogue(ys, [r[...] for r in aux_refs])
    for o_ref, o in zip(out_refs, outs):
        o_ref[...] = o.astype(o_ref.dtype)
    side_o_ref[...] = side_ref[...].astype(side_o_ref.dtype)


def proj(h, w_in, col_starts, width, aux, epilogue, out_dtypes, side, *, tm=1024, tn=256):
    t, d = h.shape
    n_seg = len(col_starts)
    grid = (t // tm, width // tn)
    side_spec, side_shape, side_bytes = _side_cast_specs(side, *grid)

    def w_spec(start):
        off = start // tn
        return pl.BlockSpec((d, tn), lambda i, j: (0, off + j))

    pipelined = (_nbytes((tm, d), BF16) + n_seg * _nbytes((d, tn), BF16)
                 + sum(_nbytes((tm, tn), dt) for dt in out_dtypes) + side_bytes)
    return pl.pallas_call(
        functools.partial(_proj_kernel, n_seg=n_seg, n_aux=len(aux), epilogue=epilogue),
        grid=grid,
        in_specs=([pl.BlockSpec((tm, d), lambda i, j: (i, 0))]
                  + [w_spec(s) for s in col_starts]
                  + [pl.BlockSpec((1, tn), lambda i, j: (0, j)) for _ in aux]
                  + [side_spec]),
        out_specs=[pl.BlockSpec((tm, tn), lambda i, j: (i, j)) for _ in out_dtypes] + [side_spec],
        out_shape=[jax.ShapeDtypeStruct((t, width), dt) for dt in out_dtypes] + [side_shape],
        compiler_params=pltpu.CompilerParams(
            dimension_semantics=("parallel", "arbitrary"),
            vmem_limit_bytes=_vmem_limit(pipelined, (n_seg + 4) * _nbytes((tm, tn), F32))),
        name="proj",
    )(h, *([w_in] * n_seg), *[a.reshape(1, width) for a in aux], side)


def _hgrn_epilogue(ys, aux):
    q, fr, i, g = ys
    lb, = aux
    f = lb + (1.0 - lb) * jax.nn.sigmoid(fr)
    return _silu(q), jnp.log(f) * LOG2_E, 1.0 - f, i, _silu(g)


def _attn_epilogue(ys, aux):
    q, k, v = ys
    return q * (HEAD_DIM ** -0.5), k, v


def _gate_epilogue(ys, aux):
    ga, gb = ys
    ba, bb = aux
    return jax.nn.sigmoid(ga + ba), jax.nn.sigmoid(gb + bb)


def _lower_half_total(p_hb, level, row8):
    c, w = p_hb.shape
    hb = 1 << level
    if 2 * hb >= F32_SUBLANES:
        blk = min(2 * hb, c)
        return jnp.concatenate(
            [jnp.broadcast_to(p_hb[b * blk + hb - 1:b * blk + hb, :], (blk, w)) for b in range(c // blk)], axis=0)
    x = p_hb.reshape(c // F32_SUBLANES, F32_SUBLANES, w)
    odd = (row8 & 1) == 1
    if level == 0:
        g = jnp.where(odd, pltpu.roll(x, 1, 1), x)
    else:
        z = jnp.where(odd, x, pltpu.roll(x, 7, 1))
        g = jnp.where((row8 & 2) == 0, z, pltpu.roll(z, 2, 1))
    return g.reshape(c, w)


def _add_to_upper_halves(p_hb, g, level, upper_mask):
    c = p_hb.shape[0]
    hb = 1 << level
    if hb < F32_SUBLANES:
        return p_hb + jnp.where(upper_mask, g, 0.0)
    pieces = []
    for lo in range(0, c, 2 * hb):
        pieces += [p_hb[lo:lo + hb], p_hb[lo + hb:lo + 2 * hb] + g[lo + hb:lo + 2 * hb]]
    return jnp.concatenate(pieces, axis=0)


def _hgrn_kernel(qs_ref, lf_ref, kk_ref, v_ref, gs_ref, ng_ref, side_ref, o_ref, side_o_ref, st_ref, *,
                 heads_per_block, chunks_per_iter):
    c = CHUNK
    dk = HEAD_DIM
    w = qs_ref.shape[1]
    n_levels = CHUNK.bit_length() - 1

    @pl.when(pl.program_id(2) == 0)
    def _():
        st_ref[...] = jnp.zeros_like(st_ref)

    row = lax.broadcasted_iota(jnp.int32, (c, w), 0)
    row8 = lax.broadcasted_iota(jnp.int32, (c // F32_SUBLANES, F32_SUBLANES, w), 1)
    upper = [((row >> p) & 1) == 1 for p in range(n_levels)]
    ti = lax.broadcasted_iota(jnp.int32, (c, c), 0)
    si = lax.broadcasted_iota(jnp.int32, (c, c), 1)
    diag_mask = ti == si
    level_masks = [((ti >> (p + 1)) == (si >> (p + 1))) & (((ti >> p) & 1) == 1) & (((si >> p) & 1) == 0)
                   for p in range(n_levels)]

    def iter_body(it, carry):
        pre = []
        for cc in range(chunks_per_iter):
            rows = pl.ds(pl.multiple_of((it * chunks_per_iter + cc) * c, c), c)
            p_hb = lf_ref[rows, :]
            e_q, e_k = [], []
            for p in range(n_levels + 1):
                g = _lower_half_total(p_hb, p, row8)
                e_q.append(jnp.exp2(p_hb))
                e_k.append(None if p == 0 else jnp.exp2(g - p_hb))
                if p < n_levels:
                    p_hb = _add_to_upper_halves(p_hb, g, p, upper[p])
            pre.append((rows, e_q, e_k))
        indep = {}
        for cc, (rows, e_q, e_k) in enumerate(pre):
            for hh in range(heads_per_block):
                cols = slice(hh * dk, (hh + 1) * dk)
                qb = qs_ref[rows, cols]
                kb = kk_ref[rows, cols]
                v = v_ref[rows, cols]
                q = qb.astype(F32)
                k = kb.astype(F32)
                parts = [_dot_nt(qb, kb)]
                for p in range(n_levels):
                    qt = (q * e_q[p][:, cols]).astype(BF16)
                    kt = kb if p == 0 else (k * e_k[p][:, cols]).astype(BF16)
                    parts.append(_dot_nt(qt, kt))
                e_in = e_q[n_levels][:, cols]
                e_out = e_k[n_levels][:, cols]
                kv = _dot_tn(v, (k * e_out).astype(BF16))
                indep[cc, hh] = (parts, kv, (q * e_in).astype(BF16), v, e_in[c - 1:c, :])
        for hh in range(heads_per_block):
            cols = slice(hh * dk, (hh + 1) * dk)
            st = st_ref[hh]
            for cc, (rows, _, _) in enumerate(pre):
                parts, kv, q_in, v, decay_all = indep[cc, hh]
                scores = jnp.where(diag_mask, parts[0], 0.0)
                for p in range(n_levels):
                    scores = jnp.where(level_masks[p], parts[p + 1], scores)
                o = _dot(scores.astype(BF16), v) + _dot_nt(q_in, st.astype(BF16))
                st = st * decay_all + kv
                inv = lax.rsqrt(jnp.mean(o * o, axis=-1, keepdims=True) + NORM_EPS)
                o = o * inv * ng_ref[:, cols] * gs_ref[rows, cols].astype(F32)
                o_ref[rows, cols] = o.astype(o_ref.dtype)
            st_ref[hh] = st
        return carry

    lax.fori_loop(0, qs_ref.shape[0] // (c * chunks_per_iter), iter_body, 0)
    side_o_ref[...] = side_ref[...].astype(side_o_ref.dtype)


def hgrn(qs, lf, kk, v, gs, norm_g, side, *, batch, seq, block_len=1024, heads_per_block=4, chunks_per_iter=8):
    t, width = qs.shape
    bw = heads_per_block * HEAD_DIM
    n_h, n_l = width // bw, seq // block_len
    tok = pl.BlockSpec((block_len, bw), lambda b, h, l: (b * n_l + l, h))
    side_rows, side_cols = side.shape
    slab, rem = divmod(side_rows, batch * n_h * n_l)
    assert rem == 0 and slab % BF16_SUBLANES == 0, (side_rows, batch, n_h, n_l)
    side_spec = pl.BlockSpec((slab, side_cols), lambda b, h, l: ((b * n_h + h) * n_l + l, 0))
    pipelined = (4 * _nbytes((block_len, bw), BF16) + _nbytes((block_len, bw), F32)
                 + _nbytes((slab, side_cols), F32) + _nbytes((slab, side_cols), BF16))
    return pl.pallas_call(
        functools.partial(_hgrn_kernel, heads_per_block=heads_per_block, chunks_per_iter=chunks_per_iter),
        grid=(batch, n_h, n_l),
        in_specs=[tok, tok, tok, tok, tok,
                  pl.BlockSpec((1, bw), lambda b, h, l: (0, h)),
                  side_spec],
        out_specs=[tok, side_spec],
        out_shape=[jax.ShapeDtypeStruct((t, width), BF16), jax.ShapeDtypeStruct(side.shape, BF16)],
        scratch_shapes=[pltpu.VMEM((heads_per_block, HEAD_DIM, HEAD_DIM), F32)],
        compiler_params=pltpu.CompilerParams(
            dimension_semantics=("parallel", "parallel", "arbitrary"),
            vmem_limit_bytes=_vmem_limit(pipelined, 0)),
        name="hgrn",
    )(qs, lf, kk, v, gs, norm_g.reshape(1, width), side)


ATTN_Q_BLOCK = 2 * CHUNK
ATTN_WINDOW = 640
ATTN_LEAD = ATTN_WINDOW - ATTN_Q_BLOCK
ATTN_BIAS_LANES = pl.cdiv(ATTN_WINDOW + ATTN_Q_BLOCK - 1, LANES) * LANES


def _attn_bias_row(rel_bias):
    m = np.arange(ATTN_BIAS_LANES)
    m = np.where(m >= ATTN_WINDOW, m - ATTN_BIAS_LANES, m)
    idx = np.clip(ATTN_LEAD - m, -MAX_REL, MAX_REL) + MAX_REL
    runs, start = [], 0
    for pos in range(1, len(idx) + 1):
        if pos == len(idx) or idx[pos] != idx[pos - 1] - 1:
            runs.append((start, pos))
            start = pos
    pieces = [jnp.flip(rel_bias[:, int(idx[hi - 1]):int(idx[lo]) + 1], axis=1) for lo, hi in runs]
    return jnp.concatenate(pieces, axis=1)


def _attn_kernel(q_ref, k_ref, v_ref, brow_ref, o_ref, kpad_ref, vpad_ref, *, group):
    seq = q_ref.shape[0]
    kpad_ref[0:ATTN_LEAD, :] = jnp.zeros((ATTN_LEAD, HEAD_DIM), kpad_ref.dtype)
    vpad_ref[0:ATTN_LEAD, :] = jnp.zeros((ATTN_LEAD, HEAD_DIM), vpad_ref.dtype)
    kpad_ref[ATTN_LEAD:, :] = k_ref[...]
    vpad_ref[ATTN_LEAD:, :] = v_ref[...]
    brow = jnp.broadcast_to(brow_ref[0], (ATTN_Q_BLOCK, ATTN_BIAS_LANES))
    toep = pltpu.roll(brow, 0, 1, stride=1, stride_axis=0)[:, :ATTN_WINDOW]
    qi = lax.broadcasted_iota(jnp.int32, (ATTN_Q_BLOCK, ATTN_WINDOW), 0)
    col = lax.broadcasted_iota(jnp.int32, (ATTN_Q_BLOCK, ATTN_WINDOW), 1)
    qc = qi // CHUNK
    kc = col // CHUNK
    lead = ATTN_LEAD // CHUNK
    bias = jnp.where((kc >= qc + lead - LEFT_CHUNKS) & (kc <= qc + lead), toep, MASK_VALUE)

    def do_group(base, masked):
        starts = [base + u * ATTN_Q_BLOCK for u in range(group)]
        scores = []
        for start in starts:
            q = q_ref[pl.ds(start, ATTN_Q_BLOCK), :]
            kw = kpad_ref[pl.ds(start, ATTN_WINDOW), :]
            s = _dot_nt(q, kw) + bias
            if masked:
                s = jnp.where(col >= ATTN_LEAD - start, s, MASK_VALUE)
            scores.append(s)
        probs = []
        for s in scores:
            m = jnp.max(s, axis=-1, keepdims=True)
            p = jnp.exp(s - m)
            probs.append((p.astype(BF16), jnp.sum(p, axis=-1, keepdims=True)))
        for start, (p, denom) in zip(starts, probs):
            vw = vpad_ref[pl.ds(start, ATTN_WINDOW), :]
            o = _dot(p, vw) / denom
            o_ref[pl.ds(start, ATTN_Q_BLOCK), :] = o.astype(o_ref.dtype)

    span = group * ATTN_Q_BLOCK
    n_masked = pl.cdiv(ATTN_LEAD, span)
    for g in range(n_masked):
        do_group(g * span, True)

    def body(g, carry):
        do_group(pl.multiple_of(g * span, span), False)
        return carry

    lax.fori_loop(n_masked, seq // span, body, 0)


def band_attn(q, k, v, bias_row, *, batch, seq, group=16):
    t, width = q.shape
    tok = pl.BlockSpec((seq, HEAD_DIM), lambda b, h: (b, h))
    return pl.pallas_call(
        functools.partial(_attn_kernel, group=group),
        grid=(batch, width // HEAD_DIM),
        in_specs=[tok, tok, tok,
                  pl.BlockSpec((1, 1, ATTN_BIAS_LANES), lambda b, h: (h, 0, 0))],
        out_specs=tok,
        out_shape=jax.ShapeDtypeStruct((t, width), BF16),
        scratch_shapes=[pltpu.VMEM((seq + ATTN_LEAD, HEAD_DIM), BF16),
                        pltpu.VMEM((seq + ATTN_LEAD, HEAD_DIM), BF16)],
        compiler_params=pltpu.CompilerParams(
            dimension_semantics=("parallel", "parallel"),
            vmem_limit_bytes=_vmem_limit(
                4 * _nbytes((seq, HEAD_DIM), BF16),
                2 * _nbytes((seq + ATTN_LEAD, HEAD_DIM), BF16))),
        name="band_attn",
    )(q, k, v, bias_row.reshape(bias_row.shape[0], 1, ATTN_BIAS_LANES))


def _mix_up_kernel(oa_ref, ob_ref, wa_ref, wb_ref, ga_ref, gb_ref, o_ref):
    ya = _dot(oa_ref[...], wa_ref[...])
    yb = _dot(ob_ref[...], wb_ref[...])
    o_ref[...] = (ga_ref[...].astype(F32) * ya + gb_ref[...].astype(F32) * yb).astype(o_ref.dtype)


def mix_up(oa, ob, wa, wb, ga, gb, *, tm=1024, tn=1024):
    t, kdim = oa.shape
    d = wa.shape[1]
    pipelined = (2 * _nbytes((tm, kdim), BF16) + 2 * _nbytes((kdim, tn), BF16) + 3 * _nbytes((tm, tn), BF16))
    return pl.pallas_call(
        _mix_up_kernel,
        grid=(t // tm, d // tn),
        in_specs=[pl.BlockSpec((tm, kdim), lambda i, j: (i, 0)),
                  pl.BlockSpec((tm, kdim), lambda i, j: (i, 0)),
                  pl.BlockSpec((kdim, tn), lambda i, j: (0, j)),
                  pl.BlockSpec((kdim, tn), lambda i, j: (0, j)),
                  pl.BlockSpec((tm, tn), lambda i, j: (i, j)),
                  pl.BlockSpec((tm, tn), lambda i, j: (i, j))],
        out_specs=pl.BlockSpec((tm, tn), lambda i, j: (i, j)),
        out_shape=jax.ShapeDtypeStruct((t, d), BF16),
        compiler_params=pltpu.CompilerParams(
            dimension_semantics=("parallel", "arbitrary"),
            vmem_limit_bytes=_vmem_limit(pipelined, 4 * _nbytes((tm, tn), F32))),
        name="mix_up",
    )(oa, ob, wa, wb, ga, gb)


def _ffn(x, h, w1, w3, w2, post_g, next_g, emit_next, side=None, glu_tm=2048, glu_h_buffers=2):
    g, w2_b = glu_up(h, w1, w3, w2, tm=glu_tm, h_buffers=glu_h_buffers)
    if side is None:
        y, side_b = matmul_kres(g, w2_b), None
    else:
        y, side_b = matmul_kres(g, w2_b, side)
    x_new, h_next = resnorm(y, x, post_g, next_g, scale=MACARON_WEIGHT, emit_next=emit_next)
    return x_new, h_next, side_b


def kernel(x, ffn1_pre_g, ffn1_post_g, ffn1_w1, ffn1_w3, ffn1_w2, mix_pre_g, mix_post_g, w_in, b_gate,
           hgrn_lb_logits, hgrn_norm_g, rel_bias, w_up_a, w_up_b, w_out,
           ffn2_pre_g, ffn2_post_g, ffn2_w1, ffn2_w3, ffn2_w2):
    batch, seq, d = x.shape
    depth = ffn1_w1.shape[0]
    d_half = d // 2
    lower_bounds = jnp.cumsum(jax.nn.softmax(hgrn_lb_logits.astype(F32), axis=0), axis=0)
    xt = x.reshape(batch * seq, d)
    h = norm_cast(xt, ffn1_pre_g[0])
    for layer in range(depth):
        xt, h, w_in_b = _ffn(xt, h, ffn1_w1[layer], ffn1_w3[layer], ffn1_w2[layer],
                             ffn1_post_g[layer], mix_pre_g[layer], True, side=w_in[layer],
                             glu_tm=4096, glu_h_buffers=1)

        qs, lf, kk, vv, gs, w_up_a_b = proj(h, w_in_b, [0, d_half, 2 * d_half, 3 * d_half], d_half,
                                            [lower_bounds[layer]], _hgrn_epilogue,
                                            [BF16, F32, BF16, BF16, BF16], w_up_a[layer])
        qb, kb, vb, w_up_b_b = proj(h, w_in_b, [4 * d_half, 5 * d_half, 6 * d_half], d_half,
                                    [], _attn_epilogue, [BF16, BF16, BF16], w_up_b[layer], tn=512)
        ga, gb, w_out_b = proj(h, w_in_b, [7 * d_half, 7 * d_half + d], d,
                               [b_gate[layer, 0], b_gate[layer, 1]], _gate_epilogue, [BF16, BF16],
                               w_out[layer], tn=512)

        oa, ffn2_w1_b = hgrn(qs, lf, kk, vv, gs, hgrn_norm_g[layer], ffn2_w1[layer], batch=batch, seq=seq)
        ob = band_attn(qb, kb, vb, _attn_bias_row(rel_bias[layer].astype(F32)), batch=batch, seq=seq)

        m = mix_up(oa, ob, w_up_a_b, w_up_b_b, ga, gb)
        last = layer == depth - 1
        y, ffn2_w3_b = matmul_kres(m, w_out_b, ffn2_w3[layer], tm=1024)
        xt, h = resnorm(y, xt, mix_post_g[layer], ffn2_pre_g[layer], scale=1.0, emit_next=True)
        next_pre = ffn1_pre_g[layer + 1] if not last else ffn2_pre_g[layer]
        xt, h, _ = _ffn(xt, h, ffn2_w1_b, ffn2_w3_b, ffn2_w2[layer],
                        ffn2_post_g[layer], next_pre, not last, glu_tm=4096, glu_h_buffers=1)
    return xt.reshape(batch, seq, d)
```

```python
import functools

import numpy as np
import jax
import jax.numpy as jnp
from jax import lax
from jax.experimental import pallas as pl
from jax.experimental.pallas import tpu as pltpu

F32 = jnp.float32
BF16 = jnp.bfloat16

NORM_EPS = 1e-6
LOG2_E = 1.4426950408889634
MACARON_WEIGHT = 0.5
CHUNK = 64
HEAD_DIM = 128
LANES = 128
F32_SUBLANES = 8
BF16_SUBLANES = 16
LEFT_CHUNKS = 8
MAX_REL = 256
GLU_DOT_ROWS = 1024
MASK_VALUE = -1e30

MIB = 1024 * 1024
V7X_VMEM_BYTES = 64 * MIB
VMEM_LIMIT_CAP = V7X_VMEM_BYTES - 2 * MIB
VMEM_LIMIT_FLOOR = 32 * MIB
VMEM_SPILL_BYTES = 8 * MIB


def _vmem_limit(pipelined_bytes, resident_bytes=0):
    est = 2 * pipelined_bytes + resident_bytes + VMEM_SPILL_BYTES
    return int(min(max(est, VMEM_LIMIT_FLOOR), VMEM_LIMIT_CAP))


def _nbytes(shape, dtype):
    return int(np.prod(shape)) * jnp.dtype(dtype).itemsize


def _dot(a, b):
    return jnp.dot(a, b, preferred_element_type=F32)


def _dot_nt(a, b):
    return lax.dot_general(a, b, (((1,), (1,)), ((), ())), preferred_element_type=F32)


def _dot_tn(a, b):
    return lax.dot_general(a, b, (((0,), (0,)), ((), ())), preferred_element_type=F32)


def _silu(x):
    return x * jax.nn.sigmoid(x)


def _side_cast_specs(side, n_i, n_j):
    rows, cols = side.shape
    slab, rem = divmod(rows, n_i * n_j)
    assert rem == 0 and slab % BF16_SUBLANES == 0, (rows, n_i, n_j)
    spec = pl.BlockSpec((slab, cols), lambda i, j: (i * n_j + j, 0))
    return spec, jax.ShapeDtypeStruct((rows, cols), BF16), _nbytes((slab, cols), F32) + _nbytes((slab, cols), BF16)


def _norm_cast_kernel(x_ref, g_ref, o_ref):
    x = x_ref[...]
    inv = lax.rsqrt(jnp.mean(x * x, axis=-1, keepdims=True) + NORM_EPS)
    o_ref[...] = (x * inv * g_ref[...]).astype(o_ref.dtype)


def norm_cast(x, g, *, tm=256):
    t, d = x.shape
    return pl.pallas_call(
        _norm_cast_kernel,
        grid=(t // tm,),
        in_specs=[pl.BlockSpec((tm, d), lambda i: (i, 0)),
                  pl.BlockSpec((1, d), lambda i: (0, 0))],
        out_specs=pl.BlockSpec((tm, d), lambda i: (i, 0)),
        out_shape=jax.ShapeDtypeStruct((t, d), BF16),
        compiler_params=pltpu.CompilerParams(
            dimension_semantics=("parallel",),
            vmem_limit_bytes=_vmem_limit(_nbytes((tm, d), F32) + _nbytes((tm, d), BF16),
                                         3 * _nbytes((tm, d), F32))),
        name="norm_cast",
    )(x, g.reshape(1, d))


def _glu_up_kernel(h_ref, w1_ref, w3_ref, side_ref, o_ref, side_o_ref):
    w1 = w1_ref[...].astype(BF16)
    w3 = w3_ref[...].astype(BF16)
    for r in range(0, h_ref.shape[0], GLU_DOT_ROWS):
        h = h_ref[r:r + GLU_DOT_ROWS, :]
        o_ref[r:r + GLU_DOT_ROWS, :] = (_silu(_dot(h, w1)) * _dot(h, w3)).astype(o_ref.dtype)
    side_o_ref[...] = side_ref[...].astype(side_o_ref.dtype)


def glu_up(h, w1, w3, side, *, tm=4096, tn=256):
    t, d = h.shape
    f = w1.shape[1]
    grid = (t // tm, f // tn)
    side_spec, side_shape, side_bytes = _side_cast_specs(side, *grid)
    return pl.pallas_call(
        _glu_up_kernel,
        grid=grid,
        in_specs=[pl.BlockSpec((tm, d), lambda i, j: (i, 0), pipeline_mode=pl.Buffered(1)),
                  pl.BlockSpec((d, tn), lambda i, j: (0, j)),
                  pl.BlockSpec((d, tn), lambda i, j: (0, j)),
                  side_spec],
        out_specs=[pl.BlockSpec((tm, tn), lambda i, j: (i, j)), side_spec],
        out_shape=[jax.ShapeDtypeStruct((t, f), BF16), side_shape],
        compiler_params=pltpu.CompilerParams(
            dimension_semantics=("parallel", "arbitrary"),
            vmem_limit_bytes=_vmem_limit(
                2 * _nbytes((d, tn), w1.dtype) + _nbytes((tm, tn), BF16) + side_bytes,
                _nbytes((tm, d), BF16) + 4 * _nbytes((GLU_DOT_ROWS, tn), F32) + 2 * _nbytes((d, tn), BF16))),
        name="glu_up",
    )(h, w1, w3, side)


def _matmul_kernel(a_ref, w_ref, *rest):
    maybe_side_ref, o_ref, maybe_side_o_ref = rest if len(rest) == 3 else (None, rest[0], None)
    o_ref[...] = _dot(a_ref[...], w_ref[...]).astype(o_ref.dtype)
    if maybe_side_ref is not None:
        maybe_side_o_ref[...] = maybe_side_ref[...].astype(maybe_side_o_ref.dtype)


def matmul_kres(a, w, side=None, *, tm=512, tn=1024):
    t, kdim = a.shape
    n = w.shape[1]
    grid = (n // tn, t // tm)
    in_specs = [pl.BlockSpec((tm, kdim), lambda j, i: (i, 0)),
                pl.BlockSpec((kdim, tn), lambda j, i: (0, j), pipeline_mode=pl.Buffered(1))]
    out_specs = [pl.BlockSpec((tm, tn), lambda j, i: (i, j))]
    out_shape = [jax.ShapeDtypeStruct((t, n), BF16)]
    operands = [a, w]
    side_bytes = 0
    if side is not None:
        side_spec, side_shape, side_bytes = _side_cast_specs(side, *grid)
        in_specs.append(side_spec)
        out_specs.append(side_spec)
        out_shape.append(side_shape)
        operands.append(side)
    outs = pl.pallas_call(
        _matmul_kernel,
        grid=grid,
        in_specs=in_specs,
        out_specs=out_specs,
        out_shape=out_shape,
        compiler_params=pltpu.CompilerParams(
            dimension_semantics=("parallel", "arbitrary"),
            vmem_limit_bytes=_vmem_limit(
                _nbytes((tm, kdim), BF16) + _nbytes((tm, tn), BF16) + side_bytes,
                _nbytes((kdim, tn), BF16) + 2 * _nbytes((tm, tn), F32))),
        name="matmul_kres",
    )(*operands)
    return outs if side is not None else outs[0]


def _resnorm_kernel(y_ref, x_ref, gpost_ref, gnext_ref, xo_ref, *maybe_h_ref, scale):
    y = y_ref[...].astype(F32)
    inv = lax.rsqrt(jnp.mean(y * y, axis=-1, keepdims=True) + NORM_EPS)
    xn = x_ref[...] + scale * (y * inv * gpost_ref[...])
    xo_ref[...] = xn
    if maybe_h_ref:
        inv2 = lax.rsqrt(jnp.mean(xn * xn, axis=-1, keepdims=True) + NORM_EPS)
        maybe_h_ref[0][...] = (xn * inv2 * gnext_ref[...]).astype(BF16)


def resnorm(y, x, g_post, g_next, *, scale, emit_next, tm=256):
    t, d = x.shape
    row = pl.BlockSpec((tm, d), lambda i: (i, 0))
    vec = pl.BlockSpec((1, d), lambda i: (0, 0))
    out_shape = [jax.ShapeDtypeStruct((t, d), F32)]
    if emit_next:
        out_shape.append(jax.ShapeDtypeStruct((t, d), BF16))
    pipelined = (_nbytes((tm, d), y.dtype) + 2 * _nbytes((tm, d), F32)
                 + (_nbytes((tm, d), BF16) if emit_next else 0))
    outs = pl.pallas_call(
        functools.partial(_resnorm_kernel, scale=scale),
        grid=(t // tm,),
        in_specs=[row, row, vec, vec],
        out_specs=[row] * len(out_shape),
        out_shape=out_shape,
        compiler_params=pltpu.CompilerParams(
            dimension_semantics=("parallel",),
            vmem_limit_bytes=_vmem_limit(pipelined, 3 * _nbytes((tm, d), F32))),
        name="resnorm",
    )(y, x, g_post.reshape(1, d), g_next.reshape(1, d))
    return outs if emit_next else (outs[0], None)


def _proj_kernel(*refs, n_seg, n_aux, epilogue):
    h_ref = refs[0]
    w_refs = refs[1:1 + n_seg]
    aux_refs = refs[1 + n_seg:1 + n_seg + n_aux]
    side_ref = refs[1 + n_seg + n_aux]
    out_refs = refs[2 + n_seg + n_aux:-1]
    side_o_ref = refs[-1]
    h = h_ref[...]
    ys = [_dot(h, w_ref[...]) for w_ref in w_refs]
    outs = epilogue(ys, [r[...] for r in aux_refs])
    for o_ref, o in zip(out_refs, outs):
        o_ref[...] = o.astype(o_ref.dtype)
    side_o_ref[...] = side_ref[...].astype(side_o_ref.dtype)


def proj(h, w_in, col_starts, width, aux, epilogue, out_dtypes, side, *, tm=1024, tn=256):
    t, d = h.shape
    n_seg = len(col_starts)
    grid = (t // tm, width // tn)
    side_spec, side_shape, side_bytes = _side_cast_specs(side, *grid)

    def w_spec(start):
        off = start // tn
        return pl.BlockSpec((d, tn), lambda i, j: (0, off + j))

    pipelined = (_nbytes((tm, d), BF16) + n_seg * _nbytes((d, tn), BF16)
                 + sum(_nbytes((tm, tn), dt) for dt in out_dtypes) + side_bytes)
    return pl.pallas_call(
        functools.partial(_proj_kernel, n_seg=n_seg, n_aux=len(aux), epilogue=epilogue),
        grid=grid,
        in_specs=([pl.BlockSpec((tm, d), lambda i, j: (i, 0))]
                  + [w_spec(s) for s in col_starts]
                  + [pl.BlockSpec((1, tn), lambda i, j: (0, j)) for _ in aux]
                  + [side_spec]),
        out_specs=[pl.BlockSpec((tm, tn), lambda i, j: (i, j)) for _ in out_dtypes] + [side_spec],
        out_shape=[jax.ShapeDtypeStruct((t, width), dt) for dt in out_dtypes] + [side_shape],
        compiler_params=pltpu.CompilerParams(
            dimension_semantics=("parallel", "arbitrary"),
            vmem_limit_bytes=_vmem_limit(pipelined, (n_seg + 4) * _nbytes((tm, tn), F32))),
        name="proj",
    )(h, *([w_in] * n_seg), *[a.reshape(1, width) for a in aux], side)


def _hgrn_epilogue(ys, aux):
    q, fr, i, g = ys
    lb, = aux
    f = lb + (1.0 - lb) * jax.nn.sigmoid(fr)
    return _silu(q), jnp.log(f) * LOG2_E, 1.0 - f, i, _silu(g)


def _attn_epilogue(ys, aux):
    q, k, v = ys
    return q * (HEAD_DIM ** -0.5), k, v


def _gate_epilogue(ys, aux):
    ga, gb = ys
    ba, bb = aux
    return jax.nn.sigmoid(ga + ba), jax.nn.sigmoid(gb + bb)


def _lower_half_total(p_hb, level, row8):
    c, w = p_hb.shape
    hb = 1 << level
    if 2 * hb >= F32_SUBLANES:
        blk = min(2 * hb, c)
        return jnp.concatenate(
            [jnp.broadcast_to(p_hb[b * blk + hb - 1:b * blk + hb, :], (blk, w)) for b in range(c // blk)], axis=0)
    x = p_hb.reshape(c // F32_SUBLANES, F32_SUBLANES, w)
    odd = (row8 & 1) == 1
    if level == 0:
        g = jnp.where(odd, pltpu.roll(x, 1, 1), x)
    else:
        z = jnp.where(odd, x, pltpu.roll(x, 7, 1))
        g = jnp.where((row8 & 2) == 0, z, pltpu.roll(z, 2, 1))
    return g.reshape(c, w)


def _add_to_upper_halves(p_hb, g, level, upper_mask):
    c = p_hb.shape[0]
    hb = 1 << level
    if hb < F32_SUBLANES:
        return p_hb + jnp.where(upper_mask, g, 0.0)
    pieces = []
    for lo in range(0, c, 2 * hb):
        pieces += [p_hb[lo:lo + hb], p_hb[lo + hb:lo + 2 * hb] + g[lo + hb:lo + 2 * hb]]
    return jnp.concatenate(pieces, axis=0)


def _hgrn_kernel(qs_ref, lf_ref, kk_ref, v_ref, gs_ref, ng_ref, o_ref, st_ref, *, heads_per_block,
                 chunks_per_iter):
    c = CHUNK
    dk = HEAD_DIM
    w = qs_ref.shape[1]
    n_levels = CHUNK.bit_length() - 1

    @pl.when(pl.program_id(2) == 0)
    def _():
        st_ref[...] = jnp.zeros_like(st_ref)

    row = lax.broadcasted_iota(jnp.int32, (c, w), 0)
    row8 = lax.broadcasted_iota(jnp.int32, (c // F32_SUBLANES, F32_SUBLANES, w), 1)
    upper = [((row >> p) & 1) == 1 for p in range(n_levels)]
    ti = lax.broadcasted_iota(jnp.int32, (c, c), 0)
    si = lax.broadcasted_iota(jnp.int32, (c, c), 1)
    diag_mask = ti == si
    level_masks = [((ti >> (p + 1)) == (si >> (p + 1))) & (((ti >> p) & 1) == 1) & (((si >> p) & 1) == 0)
                   for p in range(n_levels)]

    def iter_body(it, carry):
        pre = []
        for cc in range(chunks_per_iter):
            rows = pl.ds(pl.multiple_of((it * chunks_per_iter + cc) * c, c), c)
            p_hb = lf_ref[rows, :]
            e_q, e_k = [], []
            for p in range(n_levels + 1):
                g = _lower_half_total(p_hb, p, row8)
                e_q.append(jnp.exp2(p_hb))
                e_k.append(None if p == 0 else jnp.exp2(g - p_hb))
                if p < n_levels:
                    p_hb = _add_to_upper_halves(p_hb, g, p, upper[p])
            pre.append((rows, e_q, e_k))
        indep = {}
        for cc, (rows, e_q, e_k) in enumerate(pre):
            for hh in range(heads_per_block):
                cols = slice(hh * dk, (hh + 1) * dk)
                qb = qs_ref[rows, cols]
                kb = kk_ref[rows, cols]
                v = v_ref[rows, cols]
                q = qb.astype(F32)
                k = kb.astype(F32)
                parts = [_dot_nt(qb, kb)]
                for p in range(n_levels):
                    qt = (q * e_q[p][:, cols]).astype(BF16)
                    kt = kb if p == 0 else (k * e_k[p][:, cols]).astype(BF16)
                    parts.append(_dot_nt(qt, kt))
                e_in = e_q[n_levels][:, cols]
                e_out = e_k[n_levels][:, cols]
                kv = _dot_tn(v, (k * e_out).astype(BF16))
                indep[cc, hh] = (parts, kv, (q * e_in).astype(BF16), v, e_in[c - 1:c, :])
        for hh in range(heads_per_block):
            cols = slice(hh * dk, (hh + 1) * dk)
            st = st_ref[hh]
            for cc, (rows, _, _) in enumerate(pre):
                parts, kv, q_in, v, decay_all = indep[cc, hh]
                scores = jnp.where(diag_mask, parts[0], 0.0)
                for p in range(n_levels):
                    scores = jnp.where(level_masks[p], parts[p + 1], scores)
                o = _dot(scores.astype(BF16), v) + _dot_nt(q_in, st.astype(BF16))
                st = st * decay_all + kv
                inv = lax.rsqrt(jnp.mean(o * o, axis=-1, keepdims=True) + NORM_EPS)
                o = o * inv * ng_ref[:, cols] * gs_ref[rows, cols].astype(F32)
                o_ref[rows, cols] = o.astype(o_ref.dtype)
            st_ref[hh] = st
        return carry

    lax.fori_loop(0, qs_ref.shape[0] // (c * chunks_per_iter), iter_body, 0)


def hgrn(qs, lf, kk, v, gs, norm_g, *, batch, seq, block_len=1024, heads_per_block=4, chunks_per_iter=8):
    t, width = qs.shape
    bw = heads_per_block * HEAD_DIM
    n_l = seq // block_len
    tok = pl.BlockSpec((block_len, bw), lambda b, h, l: (b * n_l + l, h))
    pipelined = 4 * _nbytes((block_len, bw), BF16) + _nbytes((block_len, bw), F32)
    return pl.pallas_call(
        functools.partial(_hgrn_kernel, heads_per_block=heads_per_block, chunks_per_iter=chunks_per_iter),
        grid=(batch, width // bw, n_l),
        in_specs=[tok, tok, tok, tok, tok,
                  pl.BlockSpec((1, bw), lambda b, h, l: (0, h))],
        out_specs=tok,
        out_shape=jax.ShapeDtypeStruct((t, width), BF16),
        scratch_shapes=[pltpu.VMEM((heads_per_block, HEAD_DIM, HEAD_DIM), F32)],
        compiler_params=pltpu.CompilerParams(
            dimension_semantics=("parallel", "parallel", "arbitrary"),
            vmem_limit_bytes=_vmem_limit(pipelined, 0)),
        name="hgrn",
    )(qs, lf, kk, v, gs, norm_g.reshape(1, width))


ATTN_Q_BLOCK = 2 * CHUNK
ATTN_WINDOW = 640
ATTN_LEAD = ATTN_WINDOW - ATTN_Q_BLOCK
ATTN_BIAS_LANES = pl.cdiv(ATTN_WINDOW + ATTN_Q_BLOCK - 1, LANES) * LANES


def _attn_bias_row(rel_bias):
    m = np.arange(ATTN_BIAS_LANES)
    m = np.where(m >= ATTN_WINDOW, m - ATTN_BIAS_LANES, m)
    idx = np.clip(ATTN_LEAD - m, -MAX_REL, MAX_REL) + MAX_REL
    runs, start = [], 0
    for pos in range(1, len(idx) + 1):
        if pos == len(idx) or idx[pos] != idx[pos - 1] - 1:
            runs.append((start, pos))
            start = pos
    pieces = [jnp.flip(rel_bias[:, int(idx[hi - 1]):int(idx[lo]) + 1], axis=1) for lo, hi in runs]
    return jnp.concatenate(pieces, axis=1)


def _attn_kernel(q_ref, k_ref, v_ref, brow_ref, o_ref, kpad_ref, vpad_ref, *, group):
    seq = q_ref.shape[0]
    kpad_ref[0:ATTN_LEAD, :] = jnp.zeros((ATTN_LEAD, HEAD_DIM), kpad_ref.dtype)
    vpad_ref[0:ATTN_LEAD, :] = jnp.zeros((ATTN_LEAD, HEAD_DIM), vpad_ref.dtype)
    kpad_ref[ATTN_LEAD:, :] = k_ref[...]
    vpad_ref[ATTN_LEAD:, :] = v_ref[...]
    brow = jnp.broadcast_to(brow_ref[0], (ATTN_Q_BLOCK, ATTN_BIAS_LANES))
    toep = pltpu.roll(brow, 0, 1, stride=1, stride_axis=0)[:, :ATTN_WINDOW]
    qi = lax.broadcasted_iota(jnp.int32, (ATTN_Q_BLOCK, ATTN_WINDOW), 0)
    col = lax.broadcasted_iota(jnp.int32, (ATTN_Q_BLOCK, ATTN_WINDOW), 1)
    qc = qi // CHUNK
    kc = col // CHUNK
    lead = ATTN_LEAD // CHUNK
    bias = jnp.where((kc >= qc + lead - LEFT_CHUNKS) & (kc <= qc + lead), toep, MASK_VALUE)

    def do_group(base, masked):
        starts = [base + u * ATTN_Q_BLOCK for u in range(group)]
        scores = []
        for start in starts:
            q = q_ref[pl.ds(start, ATTN_Q_BLOCK), :]
            kw = kpad_ref[pl.ds(start, ATTN_WINDOW), :]
            s = _dot_nt(q, kw) + bias
            if masked:
                s = jnp.where(col >= ATTN_LEAD - start, s, MASK_VALUE)
            scores.append(s)
        probs = []
        for s in scores:
            m = jnp.max(s, axis=-1, keepdims=True)
            p = jnp.exp(s - m)
            probs.append((p.astype(BF16), jnp.sum(p, axis=-1, keepdims=True)))
        for start, (p, denom) in zip(starts, probs):
            vw = vpad_ref[pl.ds(start, ATTN_WINDOW), :]
            o = _dot(p, vw) / denom
            o_ref[pl.ds(start, ATTN_Q_BLOCK), :] = o.astype(o_ref.dtype)

    span = group * ATTN_Q_BLOCK
    n_masked = pl.cdiv(ATTN_LEAD, span)
    for g in range(n_masked):
        do_group(g * span, True)

    def body(g, carry):
        do_group(pl.multiple_of(g * span, span), False)
        return carry

    lax.fori_loop(n_masked, seq // span, body, 0)


def band_attn(q, k, v, bias_row, *, batch, seq, group=16):
    t, width = q.shape
    tok = pl.BlockSpec((seq, HEAD_DIM), lambda b, h: (b, h))
    return pl.pallas_call(
        functools.partial(_attn_kernel, group=group),
        grid=(batch, width // HEAD_DIM),
        in_specs=[tok, tok, tok,
                  pl.BlockSpec((1, 1, ATTN_BIAS_LANES), lambda b, h: (h, 0, 0))],
        out_specs=tok,
        out_shape=jax.ShapeDtypeStruct((t, width), BF16),
        scratch_shapes=[pltpu.VMEM((seq + ATTN_LEAD, HEAD_DIM), BF16),
                        pltpu.VMEM((seq + ATTN_LEAD, HEAD_DIM), BF16)],
        compiler_params=pltpu.CompilerParams(
            dimension_semantics=("parallel", "parallel"),
            vmem_limit_bytes=_vmem_limit(
                4 * _nbytes((seq, HEAD_DIM), BF16),
                2 * _nbytes((seq + ATTN_LEAD, HEAD_DIM), BF16))),
        name="band_attn",
    )(q, k, v, bias_row.reshape(bias_row.shape[0], 1, ATTN_BIAS_LANES))


def _mix_up_kernel(oa_ref, ob_ref, wa_ref, wb_ref, ga_ref, gb_ref, o_ref):
    ya = _dot(oa_ref[...], wa_ref[...])
    yb = _dot(ob_ref[...], wb_ref[...])
    o_ref[...] = (ga_ref[...].astype(F32) * ya + gb_ref[...].astype(F32) * yb).astype(o_ref.dtype)


def mix_up(oa, ob, wa, wb, ga, gb, *, tm=1024, tn=1024):
    t, kdim = oa.shape
    d = wa.shape[1]
    pipelined = (2 * _nbytes((tm, kdim), BF16) + 2 * _nbytes((kdim, tn), BF16) + 3 * _nbytes((tm, tn), BF16))
    return pl.pallas_call(
        _mix_up_kernel,
        grid=(t // tm, d // tn),
        in_specs=[pl.BlockSpec((tm, kdim), lambda i, j: (i, 0)),
                  pl.BlockSpec((tm, kdim), lambda i, j: (i, 0)),
                  pl.BlockSpec((kdim, tn), lambda i, j: (0, j)),
                  pl.BlockSpec((kdim, tn), lambda i, j: (0, j)),
                  pl.BlockSpec((tm, tn), lambda i, j: (i, j)),
                  pl.BlockSpec((tm, tn), lambda i, j: (i, j))],
        out_specs=pl.BlockSpec((tm, tn), lambda i, j: (i, j)),
        out_shape=jax.ShapeDtypeStruct((t, d), BF16),
        compiler_params=pltpu.CompilerParams(
            dimension_semantics=("parallel", "arbitrary"),
            vmem_limit_bytes=_vmem_limit(pipelined, 4 * _nbytes((tm, tn), F32))),
        name="mix_up",
    )(oa, ob, wa, wb, ga, gb)


def _ffn(x, h, w1, w3, w2, post_g, next_g, emit_next, side=None):
    g, w2_b = glu_up(h, w1, w3, w2)
    if side is None:
        y, side_b = matmul_kres(g, w2_b), None
    else:
        y, side_b = matmul_kres(g, w2_b, side)
    x_new, h_next = resnorm(y, x, post_g, next_g, scale=MACARON_WEIGHT, emit_next=emit_next)
    return x_new, h_next, side_b


def kernel(x, ffn1_pre_g, ffn1_post_g, ffn1_w1, ffn1_w3, ffn1_w2, mix_pre_g, mix_post_g, w_in, b_gate,
           hgrn_lb_logits, hgrn_norm_g, rel_bias, w_up_a, w_up_b, w_out,
           ffn2_pre_g, ffn2_post_g, ffn2_w1, ffn2_w3, ffn2_w2):
    batch, seq, d = x.shape
    depth = ffn1_w1.shape[0]
    d_half = d // 2
    lower_bounds = jnp.cumsum(jax.nn.softmax(hgrn_lb_logits.astype(F32), axis=0), axis=0)
    xt = x.reshape(batch * seq, d)
    h = norm_cast(xt, ffn1_pre_g[0])
    for layer in range(depth):
        xt, h, w_in_b = _ffn(xt, h, ffn1_w1[layer], ffn1_w3[layer], ffn1_w2[layer],
                             ffn1_post_g[layer], mix_pre_g[layer], True, side=w_in[layer])

        qs, lf, kk, vv, gs, w_up_a_b = proj(h, w_in_b, [0, d_half, 2 * d_half, 3 * d_half], d_half,
                                            [lower_bounds[layer]], _hgrn_epilogue,
                                            [BF16, F32, BF16, BF16, BF16], w_up_a[layer])
        qb, kb, vb, w_up_b_b = proj(h, w_in_b, [4 * d_half, 5 * d_half, 6 * d_half], d_half,
                                    [], _attn_epilogue, [BF16, BF16, BF16], w_up_b[layer], tn=512)
        ga, gb, w_out_b = proj(h, w_in_b, [7 * d_half, 7 * d_half + d], d,
                               [b_gate[layer, 0], b_gate[layer, 1]], _gate_epilogue, [BF16, BF16],
                               w_out[layer], tn=512)

        oa = hgrn(qs, lf, kk, vv, gs, hgrn_norm_g[layer], batch=batch, seq=seq)
        ob = band_attn(qb, kb, vb, _attn_bias_row(rel_bias[layer].astype(F32)), batch=batch, seq=seq)

        m = mix_up(oa, ob, w_up_a_b, w_up_b_b, ga, gb)
        last = layer == depth - 1
        y = matmul_kres(m, w_out_b, tm=1024)
        xt, h = resnorm(y, xt, mix_post_g[layer], ffn2_pre_g[layer], scale=1.0, emit_next=True)
        next_pre = ffn1_pre_g[layer + 1] if not last else ffn2_pre_g[layer]
        xt, h, _ = _ffn(xt, h, ffn2_w1[layer], ffn2_w3[layer], ffn2_w2[layer],
                        ffn2_post_g[layer], next_pre, not last)
    return xt.reshape(batch, seq, d)
```

```python
import functools

import numpy as np
import jax
import jax.numpy as jnp
from jax import lax
from jax.experimental import pallas as pl
from jax.experimental.pallas import tpu as pltpu

F32 = jnp.float32
BF16 = jnp.bfloat16

NORM_EPS = 1e-6
LOG2_E = 1.4426950408889634
MACARON_WEIGHT = 0.5
CHUNK = 64
HEAD_DIM = 128
LANES = 128
F32_SUBLANES = 8
BF16_SUBLANES = 16
LEFT_CHUNKS = 8
MAX_REL = 256
GLU_DOT_ROWS = 1024
PROJ_DOT_ROWS = 1024
MASK_VALUE = -1e30

MIB = 1024 * 1024
V7X_VMEM_BYTES = 64 * MIB
VMEM_LIMIT_CAP = V7X_VMEM_BYTES - 2 * MIB
VMEM_LIMIT_FLOOR = 32 * MIB
VMEM_SPILL_BYTES = 8 * MIB


def _vmem_limit(pipelined_bytes, resident_bytes=0):
    est = 2 * pipelined_bytes + resident_bytes + VMEM_SPILL_BYTES
    return int(min(max(est, VMEM_LIMIT_FLOOR), VMEM_LIMIT_CAP))


def _nbytes(shape, dtype):
    return int(np.prod(shape)) * jnp.dtype(dtype).itemsize


def _dot(a, b):
    return jnp.dot(a, b, preferred_element_type=F32)


def _dot_nt(a, b):
    return lax.dot_general(a, b, (((1,), (1,)), ((), ())), preferred_element_type=F32)


def _dot_tn(a, b):
    return lax.dot_general(a, b, (((0,), (0,)), ((), ())), preferred_element_type=F32)


def _silu(x):
    return x * jax.nn.sigmoid(x)


def _side_cast_specs(side, n_i, n_j):
    rows, cols = side.shape
    slab, rem = divmod(rows, n_i * n_j)
    assert rem == 0 and slab % BF16_SUBLANES == 0, (rows, n_i, n_j)
    spec = pl.BlockSpec((slab, cols), lambda i, j: (i * n_j + j, 0))
    return spec, jax.ShapeDtypeStruct((rows, cols), BF16), _nbytes((slab, cols), F32) + _nbytes((slab, cols), BF16)


def _norm_cast_kernel(x_ref, g_ref, o_ref):
    x = x_ref[...]
    inv = lax.rsqrt(jnp.mean(x * x, axis=-1, keepdims=True) + NORM_EPS)
    o_ref[...] = (x * inv * g_ref[...]).astype(o_ref.dtype)


def norm_cast(x, g, *, tm=256):
    t, d = x.shape
    return pl.pallas_call(
        _norm_cast_kernel,
        grid=(t // tm,),
        in_specs=[pl.BlockSpec((tm, d), lambda i: (i, 0)),
                  pl.BlockSpec((1, d), lambda i: (0, 0))],
        out_specs=pl.BlockSpec((tm, d), lambda i: (i, 0)),
        out_shape=jax.ShapeDtypeStruct((t, d), BF16),
        compiler_params=pltpu.CompilerParams(
            dimension_semantics=("parallel",),
            vmem_limit_bytes=_vmem_limit(_nbytes((tm, d), F32) + _nbytes((tm, d), BF16),
                                         3 * _nbytes((tm, d), F32))),
        name="norm_cast",
    )(x, g.reshape(1, d))


def _glu_up_kernel(h_ref, w1_ref, w3_ref, side_ref, o_ref, side_o_ref):
    w1 = w1_ref[...].astype(BF16)
    w3 = w3_ref[...].astype(BF16)
    for r in range(0, h_ref.shape[0], GLU_DOT_ROWS):
        h = h_ref[r:r + GLU_DOT_ROWS, :]
        o_ref[r:r + GLU_DOT_ROWS, :] = (_silu(_dot(h, w1)) * _dot(h, w3)).astype(o_ref.dtype)
    side_o_ref[...] = side_ref[...].astype(side_o_ref.dtype)


def glu_up(h, w1, w3, side, *, tm=4096, tn=256):
    t, d = h.shape
    f = w1.shape[1]
    grid = (t // tm, f // tn)
    side_spec, side_shape, side_bytes = _side_cast_specs(side, *grid)
    return pl.pallas_call(
        _glu_up_kernel,
        grid=grid,
        in_specs=[pl.BlockSpec((tm, d), lambda i, j: (i, 0), pipeline_mode=pl.Buffered(1)),
                  pl.BlockSpec((d, tn), lambda i, j: (0, j)),
                  pl.BlockSpec((d, tn), lambda i, j: (0, j)),
                  side_spec],
        out_specs=[pl.BlockSpec((tm, tn), lambda i, j: (i, j)), side_spec],
        out_shape=[jax.ShapeDtypeStruct((t, f), BF16), side_shape],
        compiler_params=pltpu.CompilerParams(
            dimension_semantics=("parallel", "arbitrary"),
            vmem_limit_bytes=_vmem_limit(
                2 * _nbytes((d, tn), w1.dtype) + _nbytes((tm, tn), BF16) + side_bytes,
                _nbytes((tm, d), BF16) + 4 * _nbytes((GLU_DOT_ROWS, tn), F32) + 2 * _nbytes((d, tn), BF16))),
        name="glu_up",
    )(h, w1, w3, side)


def _matmul_kernel(a_ref, w_ref, *rest):
    maybe_side_ref, o_ref, maybe_side_o_ref = rest if len(rest) == 3 else (None, rest[0], None)
    o_ref[...] = _dot(a_ref[...], w_ref[...]).astype(o_ref.dtype)
    if maybe_side_ref is not None:
        maybe_side_o_ref[...] = maybe_side_ref[...].astype(maybe_side_o_ref.dtype)


def matmul_kres(a, w, side=None, *, tm=512, tn=1024):
    t, kdim = a.shape
    n = w.shape[1]
    grid = (n // tn, t // tm)
    in_specs = [pl.BlockSpec((tm, kdim), lambda j, i: (i, 0)),
                pl.BlockSpec((kdim, tn), lambda j, i: (0, j), pipeline_mode=pl.Buffered(1))]
    out_specs = [pl.BlockSpec((tm, tn), lambda j, i: (i, j))]
    out_shape = [jax.ShapeDtypeStruct((t, n), BF16)]
    operands = [a, w]
    side_bytes = 0
    if side is not None:
        side_spec, side_shape, side_bytes = _side_cast_specs(side, *grid)
        in_specs.append(side_spec)
        out_specs.append(side_spec)
        out_shape.append(side_shape)
        operands.append(side)
    outs = pl.pallas_call(
        _matmul_kernel,
        grid=grid,
        in_specs=in_specs,
        out_specs=out_specs,
        out_shape=out_shape,
        compiler_params=pltpu.CompilerParams(
            dimension_semantics=("parallel", "arbitrary"),
            vmem_limit_bytes=_vmem_limit(
                _nbytes((tm, kdim), BF16) + _nbytes((tm, tn), BF16) + side_bytes,
                _nbytes((kdim, tn), BF16) + 2 * _nbytes((tm, tn), F32))),
        name="matmul_kres",
    )(*operands)
    return outs if side is not None else outs[0]


def _resnorm_kernel(y_ref, x_ref, gpost_ref, gnext_ref, xo_ref, *maybe_h_ref, scale):
    y = y_ref[...].astype(F32)
    inv = lax.rsqrt(jnp.mean(y * y, axis=-1, keepdims=True) + NORM_EPS)
    xn = x_ref[...] + scale * (y * inv * gpost_ref[...])
    xo_ref[...] = xn
    if maybe_h_ref:
        inv2 = lax.rsqrt(jnp.mean(xn * xn, axis=-1, keepdims=True) + NORM_EPS)
        maybe_h_ref[0][...] = (xn * inv2 * gnext_ref[...]).astype(BF16)


def resnorm(y, x, g_post, g_next, *, scale, emit_next, tm=256):
    t, d = x.shape
    row = pl.BlockSpec((tm, d), lambda i: (i, 0))
    vec = pl.BlockSpec((1, d), lambda i: (0, 0))
    out_shape = [jax.ShapeDtypeStruct((t, d), F32)]
    if emit_next:
        out_shape.append(jax.ShapeDtypeStruct((t, d), BF16))
    pipelined = (_nbytes((tm, d), y.dtype) + 2 * _nbytes((tm, d), F32)
                 + (_nbytes((tm, d), BF16) if emit_next else 0))
    outs = pl.pallas_call(
        functools.partial(_resnorm_kernel, scale=scale),
        grid=(t // tm,),
        in_specs=[row, row, vec, vec],
        out_specs=[row] * len(out_shape),
        out_shape=out_shape,
        compiler_params=pltpu.CompilerParams(
            dimension_semantics=("parallel",),
            vmem_limit_bytes=_vmem_limit(pipelined, 3 * _nbytes((tm, d), F32))),
        name="resnorm",
    )(y, x, g_post.reshape(1, d), g_next.reshape(1, d))
    return outs if emit_next else (outs[0], None)


def _proj_kernel(*refs, n_seg, n_aux, epilogue):
    h_ref = refs[0]
    w_refs = refs[1:1 + n_seg]
    aux_refs = refs[1 + n_seg:1 + n_seg + n_aux]
    side_ref = refs[1 + n_seg + n_aux]
    out_refs = refs[2 + n_seg + n_aux:-1]
    side_o_ref = refs[-1]
    aux = [r[...] for r in aux_refs]
    for r in range(0, h_ref.shape[0], PROJ_DOT_ROWS):
        h = h_ref[r:r + PROJ_DOT_ROWS, :]
        outs = epilogue([_dot(h, w_ref[...]) for w_ref in w_refs], aux)
        for o_ref, o in zip(out_refs, outs):
            o_ref[r:r + PROJ_DOT_ROWS, :] = o.astype(o_ref.dtype)
    side_o_ref[...] = side_ref[...].astype(side_o_ref.dtype)


def proj(h, w_in, col_starts, width, aux, epilogue, out_dtypes, side, *, tm=2048, tn=256):
    t, d = h.shape
    n_seg = len(col_starts)
    grid = (t // tm, width // tn)
    side_spec, side_shape, side_bytes = _side_cast_specs(side, *grid)

    def w_spec(start):
        off = start // tn
        return pl.BlockSpec((d, tn), lambda i, j: (0, off + j))

    pipelined = n_seg * _nbytes((d, tn), BF16) + sum(_nbytes((tm, tn), dt) for dt in out_dtypes) + side_bytes
    return pl.pallas_call(
        functools.partial(_proj_kernel, n_seg=n_seg, n_aux=len(aux), epilogue=epilogue),
        grid=grid,
        in_specs=([pl.BlockSpec((tm, d), lambda i, j: (i, 0), pipeline_mode=pl.Buffered(1))]
                  + [w_spec(s) for s in col_starts]
                  + [pl.BlockSpec((1, tn), lambda i, j: (0, j)) for _ in aux]
                  + [side_spec]),
        out_specs=[pl.BlockSpec((tm, tn), lambda i, j: (i, j)) for _ in out_dtypes] + [side_spec],
        out_shape=[jax.ShapeDtypeStruct((t, width), dt) for dt in out_dtypes] + [side_shape],
        compiler_params=pltpu.CompilerParams(
            dimension_semantics=("parallel", "arbitrary"),
            vmem_limit_bytes=_vmem_limit(
                pipelined, _nbytes((tm, d), BF16) + (n_seg + 4) * _nbytes((PROJ_DOT_ROWS, tn), F32))),
        name="proj",
    )(h, *([w_in] * n_seg), *[a.reshape(1, width) for a in aux], side)


def _hgrn_epilogue(ys, aux):
    q, fr, i, g = ys
    lb, = aux
    f = lb + (1.0 - lb) * jax.nn.sigmoid(fr)
    return _silu(q), jnp.log(f) * LOG2_E, 1.0 - f, i, _silu(g)


def _attn_epilogue(ys, aux):
    q, k, v = ys
    return q * (HEAD_DIM ** -0.5), k, v


def _gate_epilogue(ys, aux):
    ga, gb = ys
    ba, bb = aux
    return jax.nn.sigmoid(ga + ba), jax.nn.sigmoid(gb + bb)


def _lower_half_total(p_hb, level, row8):
    c, w = p_hb.shape
    hb = 1 << level
    if 2 * hb >= F32_SUBLANES:
        blk = min(2 * hb, c)
        return jnp.concatenate(
            [jnp.broadcast_to(p_hb[b * blk + hb - 1:b * blk + hb, :], (blk, w)) for b in range(c // blk)], axis=0)
    x = p_hb.reshape(c // F32_SUBLANES, F32_SUBLANES, w)
    odd = (row8 & 1) == 1
    if level == 0:
        g = jnp.where(odd, pltpu.roll(x, 1, 1), x)
    else:
        z = jnp.where(odd, x, pltpu.roll(x, 7, 1))
        g = jnp.where((row8 & 2) == 0, z, pltpu.roll(z, 2, 1))
    return g.reshape(c, w)


def _add_to_upper_halves(p_hb, g, level, upper_mask):
    c = p_hb.shape[0]
    hb = 1 << level
    if hb < F32_SUBLANES:
        return p_hb + jnp.where(upper_mask, g, 0.0)
    pieces = []
    for lo in range(0, c, 2 * hb):
        pieces += [p_hb[lo:lo + hb], p_hb[lo + hb:lo + 2 * hb] + g[lo + hb:lo + 2 * hb]]
    return jnp.concatenate(pieces, axis=0)


def _hgrn_kernel(qs_ref, lf_ref, kk_ref, v_ref, gs_ref, ng_ref, o_ref, st_ref, *, heads_per_block,
                 chunks_per_iter):
    c = CHUNK
    dk = HEAD_DIM
    w = qs_ref.shape[1]
    n_levels = CHUNK.bit_length() - 1

    @pl.when(pl.program_id(2) == 0)
    def _():
        st_ref[...] = jnp.zeros_like(st_ref)

    row = lax.broadcasted_iota(jnp.int32, (c, w), 0)
    row8 = lax.broadcasted_iota(jnp.int32, (c // F32_SUBLANES, F32_SUBLANES, w), 1)
    upper = [((row >> p) & 1) == 1 for p in range(n_levels)]
    ti = lax.broadcasted_iota(jnp.int32, (c, c), 0)
    si = lax.broadcasted_iota(jnp.int32, (c, c), 1)
    diag_mask = ti == si
    level_masks = [((ti >> (p + 1)) == (si >> (p + 1))) & (((ti >> p) & 1) == 1) & (((si >> p) & 1) == 0)
                   for p in range(n_levels)]

    def iter_body(it, carry):
        pre = []
        for cc in range(chunks_per_iter):
            rows = pl.ds(pl.multiple_of((it * chunks_per_iter + cc) * c, c), c)
            p_hb = lf_ref[rows, :]
            e_q, e_k = [], []
            for p in range(n_levels + 1):
                g = _lower_half_total(p_hb, p, row8)
                e_q.append(jnp.exp2(p_hb))
                e_k.append(None if p == 0 else jnp.exp2(g - p_hb))
                if p < n_levels:
                    p_hb = _add_to_upper_halves(p_hb, g, p, upper[p])
            pre.append((rows, e_q, e_k))
        indep = {}
        for cc, (rows, e_q, e_k) in enumerate(pre):
            for hh in range(heads_per_block):
                cols = slice(hh * dk, (hh + 1) * dk)
                qb = qs_ref[rows, cols]
                kb = kk_ref[rows, cols]
                v = v_ref[rows, cols]
                q = qb.astype(F32)
                k = kb.astype(F32)
                parts = [_dot_nt(qb, kb)]
                for p in range(n_levels):
                    qt = (q * e_q[p][:, cols]).astype(BF16)
                    kt = kb if p == 0 else (k * e_k[p][:, cols]).astype(BF16)
                    parts.append(_dot_nt(qt, kt))
                e_in = e_q[n_levels][:, cols]
                e_out = e_k[n_levels][:, cols]
                kv = _dot_tn(v, (k * e_out).astype(BF16))
                indep[cc, hh] = (parts, kv, (q * e_in).astype(BF16), v, e_in[c - 1:c, :])
        for hh in range(heads_per_block):
            cols = slice(hh * dk, (hh + 1) * dk)
            st = st_ref[hh]
            for cc, (rows, _, _) in enumerate(pre):
                parts, kv, q_in, v, decay_all = indep[cc, hh]
                scores = jnp.where(diag_mask, parts[0], 0.0)
                for p in range(n_levels):
                    scores = jnp.where(level_masks[p], parts[p + 1], scores)
                o = _dot(scores.astype(BF16), v) + _dot_nt(q_in, st.astype(BF16))
                st = st * decay_all + kv
                inv = lax.rsqrt(jnp.mean(o * o, axis=-1, keepdims=True) + NORM_EPS)
                o = o * inv * ng_ref[:, cols] * gs_ref[rows, cols].astype(F32)
                o_ref[rows, cols] = o.astype(o_ref.dtype)
            st_ref[hh] = st
        return carry

    lax.fori_loop(0, qs_ref.shape[0] // (c * chunks_per_iter), iter_body, 0)


def hgrn(qs, lf, kk, v, gs, norm_g, *, batch, seq, block_len=1024, heads_per_block=4, chunks_per_iter=8):
    t, width = qs.shape
    bw = heads_per_block * HEAD_DIM
    n_l = seq // block_len
    tok = pl.BlockSpec((block_len, bw), lambda b, h, l: (b * n_l + l, h))
    pipelined = 4 * _nbytes((block_len, bw), BF16) + _nbytes((block_len, bw), F32)
    return pl.pallas_call(
        functools.partial(_hgrn_kernel, heads_per_block=heads_per_block, chunks_per_iter=chunks_per_iter),
        grid=(batch, width // bw, n_l),
        in_specs=[tok, tok, tok, tok, tok,
                  pl.BlockSpec((1, bw), lambda b, h, l: (0, h))],
        out_specs=tok,
        out_shape=jax.ShapeDtypeStruct((t, width), BF16),
        scratch_shapes=[pltpu.VMEM((heads_per_block, HEAD_DIM, HEAD_DIM), F32)],
        compiler_params=pltpu.CompilerParams(
            dimension_semantics=("parallel", "parallel", "arbitrary"),
            vmem_limit_bytes=_vmem_limit(pipelined, 0)),
        name="hgrn",
    )(qs, lf, kk, v, gs, norm_g.reshape(1, width))


ATTN_Q_BLOCK = 2 * CHUNK
ATTN_WINDOW = 640
ATTN_LEAD = ATTN_WINDOW - ATTN_Q_BLOCK
ATTN_BIAS_LANES = pl.cdiv(ATTN_WINDOW + ATTN_Q_BLOCK - 1, LANES) * LANES


def _attn_bias_row(rel_bias):
    m = np.arange(ATTN_BIAS_LANES)
    m = np.where(m >= ATTN_WINDOW, m - ATTN_BIAS_LANES, m)
    idx = np.clip(ATTN_LEAD - m, -MAX_REL, MAX_REL) + MAX_REL
    runs, start = [], 0
    for pos in range(1, len(idx) + 1):
        if pos == len(idx) or idx[pos] != idx[pos - 1] - 1:
            runs.append((start, pos))
            start = pos
    pieces = [jnp.flip(rel_bias[:, int(idx[hi - 1]):int(idx[lo]) + 1], axis=1) for lo, hi in runs]
    return jnp.concatenate(pieces, axis=1)


def _attn_kernel(q_ref, k_ref, v_ref, brow_ref, o_ref, kpad_ref, vpad_ref, *, group):
    seq = q_ref.shape[0]
    kpad_ref[0:ATTN_LEAD, :] = jnp.zeros((ATTN_LEAD, HEAD_DIM), kpad_ref.dtype)
    vpad_ref[0:ATTN_LEAD, :] = jnp.zeros((ATTN_LEAD, HEAD_DIM), vpad_ref.dtype)
    kpad_ref[ATTN_LEAD:, :] = k_ref[...]
    vpad_ref[ATTN_LEAD:, :] = v_ref[...]
    brow = jnp.broadcast_to(brow_ref[0], (ATTN_Q_BLOCK, ATTN_BIAS_LANES))
    toep = pltpu.roll(brow, 0, 1, stride=1, stride_axis=0)[:, :ATTN_WINDOW]
    qi = lax.broadcasted_iota(jnp.int32, (ATTN_Q_BLOCK, ATTN_WINDOW), 0)
    col = lax.broadcasted_iota(jnp.int32, (ATTN_Q_BLOCK, ATTN_WINDOW), 1)
    qc = qi // CHUNK
    kc = col // CHUNK
    lead = ATTN_LEAD // CHUNK
    bias = jnp.where((kc >= qc + lead - LEFT_CHUNKS) & (kc <= qc + lead), toep, MASK_VALUE)

    def do_group(base, masked):
        starts = [base + u * ATTN_Q_BLOCK for u in range(group)]
        scores = []
        for start in starts:
            q = q_ref[pl.ds(start, ATTN_Q_BLOCK), :]
            kw = kpad_ref[pl.ds(start, ATTN_WINDOW), :]
            s = _dot_nt(q, kw) + bias
            if masked:
                s = jnp.where(col >= ATTN_LEAD - start, s, MASK_VALUE)
            scores.append(s)
        probs = []
        for s in scores:
            m = jnp.max(s, axis=-1, keepdims=True)
            p = jnp.exp(s - m)
            probs.append((p.astype(BF16), jnp.sum(p, axis=-1, keepdims=True)))
        for start, (p, denom) in zip(starts, probs):
            vw = vpad_ref[pl.ds(start, ATTN_WINDOW), :]
            o = _dot(p, vw) / denom
            o_ref[pl.ds(start, ATTN_Q_BLOCK), :] = o.astype(o_ref.dtype)

    span = group * ATTN_Q_BLOCK
    n_masked = pl.cdiv(ATTN_LEAD, span)
    for g in range(n_masked):
        do_group(g * span, True)

    def body(g, carry):
        do_group(pl.multiple_of(g * span, span), False)
        return carry

    lax.fori_loop(n_masked, seq // span, body, 0)


def band_attn(q, k, v, bias_row, *, batch, seq, group=16):
    t, width = q.shape
    tok = pl.BlockSpec((seq, HEAD_DIM), lambda b, h: (b, h))
    return pl.pallas_call(
        functools.partial(_attn_kernel, group=group),
        grid=(batch, width // HEAD_DIM),
        in_specs=[tok, tok, tok,
                  pl.BlockSpec((1, 1, ATTN_BIAS_LANES), lambda b, h: (h, 0, 0))],
        out_specs=tok,
        out_shape=jax.ShapeDtypeStruct((t, width), BF16),
        scratch_shapes=[pltpu.VMEM((seq + ATTN_LEAD, HEAD_DIM), BF16),
                        pltpu.VMEM((seq + ATTN_LEAD, HEAD_DIM), BF16)],
        compiler_params=pltpu.CompilerParams(
            dimension_semantics=("parallel", "parallel"),
            vmem_limit_bytes=_vmem_limit(
                4 * _nbytes((seq, HEAD_DIM), BF16),
                2 * _nbytes((seq + ATTN_LEAD, HEAD_DIM), BF16))),
        name="band_attn",
    )(q, k, v, bias_row.reshape(bias_row.shape[0], 1, ATTN_BIAS_LANES))


def _mix_up_kernel(oa_ref, ob_ref, wa_ref, wb_ref, ga_ref, gb_ref, o_ref):
    ya = _dot(oa_ref[...], wa_ref[...])
    yb = _dot(ob_ref[...], wb_ref[...])
    o_ref[...] = (ga_ref[...].astype(F32) * ya + gb_ref[...].astype(F32) * yb).astype(o_ref.dtype)


def mix_up(oa, ob, wa, wb, ga, gb, *, tm=1024, tn=1024):
    t, kdim = oa.shape
    d = wa.shape[1]
    pipelined = (2 * _nbytes((tm, kdim), BF16) + 2 * _nbytes((kdim, tn), BF16) + 3 * _nbytes((tm, tn), BF16))
    return pl.pallas_call(
        _mix_up_kernel,
        grid=(t // tm, d // tn),
        in_specs=[pl.BlockSpec((tm, kdim), lambda i, j: (i, 0)),
                  pl.BlockSpec((tm, kdim), lambda i, j: (i, 0)),
                  pl.BlockSpec((kdim, tn), lambda i, j: (0, j)),
                  pl.BlockSpec((kdim, tn), lambda i, j: (0, j)),
                  pl.BlockSpec((tm, tn), lambda i, j: (i, j)),
                  pl.BlockSpec((tm, tn), lambda i, j: (i, j))],
        out_specs=pl.BlockSpec((tm, tn), lambda i, j: (i, j)),
        out_shape=jax.ShapeDtypeStruct((t, d), BF16),
        compiler_params=pltpu.CompilerParams(
            dimension_semantics=("parallel", "arbitrary"),
            vmem_limit_bytes=_vmem_limit(pipelined, 4 * _nbytes((tm, tn), F32))),
        name="mix_up",
    )(oa, ob, wa, wb, ga, gb)


def _ffn(x, h, w1, w3, w2, post_g, next_g, emit_next, side=None):
    g, w2_b = glu_up(h, w1, w3, w2)
    if side is None:
        y, side_b = matmul_kres(g, w2_b), None
    else:
        y, side_b = matmul_kres(g, w2_b, side)
    x_new, h_next = resnorm(y, x, post_g, next_g, scale=MACARON_WEIGHT, emit_next=emit_next)
    return x_new, h_next, side_b


def kernel(x, ffn1_pre_g, ffn1_post_g, ffn1_w1, ffn1_w3, ffn1_w2, mix_pre_g, mix_post_g, w_in, b_gate,
           hgrn_lb_logits, hgrn_norm_g, rel_bias, w_up_a, w_up_b, w_out,
           ffn2_pre_g, ffn2_post_g, ffn2_w1, ffn2_w3, ffn2_w2):
    batch, seq, d = x.shape
    depth = ffn1_w1.shape[0]
    d_half = d // 2
    lower_bounds = jnp.cumsum(jax.nn.softmax(hgrn_lb_logits.astype(F32), axis=0), axis=0)
    xt = x.reshape(batch * seq, d)
    h = norm_cast(xt, ffn1_pre_g[0])
    for layer in range(depth):
        xt, h, w_in_b = _ffn(xt, h, ffn1_w1[layer], ffn1_w3[layer], ffn1_w2[layer],
                             ffn1_post_g[layer], mix_pre_g[layer], True, side=w_in[layer])

        qs, lf, kk, vv, gs, w_up_a_b = proj(h, w_in_b, [0, d_half, 2 * d_half, 3 * d_half], d_half,
                                            [lower_bounds[layer]], _hgrn_epilogue,
                                            [BF16, F32, BF16, BF16, BF16], w_up_a[layer])
        qb, kb, vb, w_up_b_b = proj(h, w_in_b, [4 * d_half, 5 * d_half, 6 * d_half], d_half,
                                    [], _attn_epilogue, [BF16, BF16, BF16], w_up_b[layer])
        ga, gb, w_out_b = proj(h, w_in_b, [7 * d_half, 7 * d_half + d], d,
                               [b_gate[layer, 0], b_gate[layer, 1]], _gate_epilogue, [BF16, BF16],
                               w_out[layer], tn=512)

        oa = hgrn(qs, lf, kk, vv, gs, hgrn_norm_g[layer], batch=batch, seq=seq)
        ob = band_attn(qb, kb, vb, _attn_bias_row(rel_bias[layer].astype(F32)), batch=batch, seq=seq)

        m = mix_up(oa, ob, w_up_a_b, w_up_b_b, ga, gb)
        last = layer == depth - 1
        y = matmul_kres(m, w_out_b, tm=1024)
        xt, h = resnorm(y, xt, mix_post_g[layer], ffn2_pre_g[layer], scale=1.0, emit_next=True)
        next_pre = ffn1_pre_g[layer + 1] if not last else ffn2_pre_g[layer]
        xt, h, _ = _ffn(xt, h, ffn2_w1[layer], ffn2_w3[layer], ffn2_w2[layer],
                        ffn2_post_g[layer], next_pre, not last)
    return xt.reshape(batch, seq, d)
```

```python
import functools

import numpy as np
import jax
import jax.numpy as jnp
from jax import lax
from jax.experimental import pallas as pl
from jax.experimental.pallas import tpu as pltpu

F32 = jnp.float32
BF16 = jnp.bfloat16

NORM_EPS = 1e-6
LOG2_E = 1.4426950408889634
MACARON_WEIGHT = 0.5
CHUNK = 64
HEAD_DIM = 128
LANES = 128
F32_SUBLANES = 8
BF16_SUBLANES = 16
LEFT_CHUNKS = 8
MAX_REL = 256
GLU_DOT_ROWS = 1024
MASK_VALUE = -1e30

MIB = 1024 * 1024
V7X_VMEM_BYTES = 64 * MIB
VMEM_LIMIT_CAP = V7X_VMEM_BYTES - 2 * MIB
VMEM_LIMIT_FLOOR = 32 * MIB
VMEM_SPILL_BYTES = 8 * MIB


def _vmem_limit(pipelined_bytes, resident_bytes=0):
    est = 2 * pipelined_bytes + resident_bytes + VMEM_SPILL_BYTES
    return int(min(max(est, VMEM_LIMIT_FLOOR), VMEM_LIMIT_CAP))


def _nbytes(shape, dtype):
    return int(np.prod(shape)) * jnp.dtype(dtype).itemsize


def _dot(a, b):
    return jnp.dot(a, b, preferred_element_type=F32)


def _dot_nt(a, b):
    return lax.dot_general(a, b, (((1,), (1,)), ((), ())), preferred_element_type=F32)


def _dot_tn(a, b):
    return lax.dot_general(a, b, (((0,), (0,)), ((), ())), preferred_element_type=F32)


def _silu(x):
    return x * jax.nn.sigmoid(x)


def _side_cast_specs(side, n_i, n_j):
    rows, cols = side.shape
    slab, rem = divmod(rows, n_i * n_j)
    assert rem == 0 and slab % BF16_SUBLANES == 0, (rows, n_i, n_j)
    spec = pl.BlockSpec((slab, cols), lambda i, j: (i * n_j + j, 0))
    return spec, jax.ShapeDtypeStruct((rows, cols), BF16), _nbytes((slab, cols), F32) + _nbytes((slab, cols), BF16)


def _norm_cast_kernel(x_ref, g_ref, o_ref):
    x = x_ref[...]
    inv = lax.rsqrt(jnp.mean(x * x, axis=-1, keepdims=True) + NORM_EPS)
    o_ref[...] = (x * inv * g_ref[...]).astype(o_ref.dtype)


def norm_cast(x, g, *, tm=256):
    t, d = x.shape
    return pl.pallas_call(
        _norm_cast_kernel,
        grid=(t // tm,),
        in_specs=[pl.BlockSpec((tm, d), lambda i: (i, 0)),
                  pl.BlockSpec((1, d), lambda i: (0, 0))],
        out_specs=pl.BlockSpec((tm, d), lambda i: (i, 0)),
        out_shape=jax.ShapeDtypeStruct((t, d), BF16),
        compiler_params=pltpu.CompilerParams(
            dimension_semantics=("parallel",),
            vmem_limit_bytes=_vmem_limit(_nbytes((tm, d), F32) + _nbytes((tm, d), BF16),
                                         3 * _nbytes((tm, d), F32))),
        name="norm_cast",
    )(x, g.reshape(1, d))


def _glu_up_kernel(h_ref, w1_ref, w3_ref, side_ref, o_ref, side_o_ref):
    w1 = w1_ref[...].astype(BF16)
    w3 = w3_ref[...].astype(BF16)
    for r in range(0, h_ref.shape[0], GLU_DOT_ROWS):
        h = h_ref[r:r + GLU_DOT_ROWS, :]
        o_ref[r:r + GLU_DOT_ROWS, :] = (_silu(_dot(h, w1)) * _dot(h, w3)).astype(o_ref.dtype)
    side_o_ref[...] = side_ref[...].astype(side_o_ref.dtype)


def glu_up(h, w1, w3, side, *, tm=4096, tn=256):
    t, d = h.shape
    f = w1.shape[1]
    grid = (t // tm, f // tn)
    side_spec, side_shape, side_bytes = _side_cast_specs(side, *grid)
    return pl.pallas_call(
        _glu_up_kernel,
        grid=grid,
        in_specs=[pl.BlockSpec((tm, d), lambda i, j: (i, 0), pipeline_mode=pl.Buffered(1)),
                  pl.BlockSpec((d, tn), lambda i, j: (0, j)),
                  pl.BlockSpec((d, tn), lambda i, j: (0, j)),
                  side_spec],
        out_specs=[pl.BlockSpec((tm, tn), lambda i, j: (i, j)), side_spec],
        out_shape=[jax.ShapeDtypeStruct((t, f), BF16), side_shape],
        compiler_params=pltpu.CompilerParams(
            dimension_semantics=("parallel", "arbitrary"),
            vmem_limit_bytes=_vmem_limit(
                2 * _nbytes((d, tn), w1.dtype) + _nbytes((tm, tn), BF16) + side_bytes,
                _nbytes((tm, d), BF16) + 4 * _nbytes((GLU_DOT_ROWS, tn), F32) + 2 * _nbytes((d, tn), BF16))),
        name="glu_up",
    )(h, w1, w3, side)


def _matmul_kernel(a_ref, w_ref, *rest):
    maybe_side_ref, o_ref, maybe_side_o_ref = rest if len(rest) == 3 else (None, rest[0], None)
    o_ref[...] = _dot(a_ref[...], w_ref[...]).astype(o_ref.dtype)
    if maybe_side_ref is not None:
        maybe_side_o_ref[...] = maybe_side_ref[...].astype(maybe_side_o_ref.dtype)


def matmul_kres(a, w, side=None, *, tm=512, tn=1024):
    t, kdim = a.shape
    n = w.shape[1]
    grid = (n // tn, t // tm)
    in_specs = [pl.BlockSpec((tm, kdim), lambda j, i: (i, 0)),
                pl.BlockSpec((kdim, tn), lambda j, i: (0, j), pipeline_mode=pl.Buffered(1))]
    out_specs = [pl.BlockSpec((tm, tn), lambda j, i: (i, j))]
    out_shape = [jax.ShapeDtypeStruct((t, n), BF16)]
    operands = [a, w]
    side_bytes = 0
    if side is not None:
        side_spec, side_shape, side_bytes = _side_cast_specs(side, *grid)
        in_specs.append(side_spec)
        out_specs.append(side_spec)
        out_shape.append(side_shape)
        operands.append(side)
    outs = pl.pallas_call(
        _matmul_kernel,
        grid=grid,
        in_specs=in_specs,
        out_specs=out_specs,
        out_shape=out_shape,
        compiler_params=pltpu.CompilerParams(
            dimension_semantics=("parallel", "arbitrary"),
            vmem_limit_bytes=_vmem_limit(
                _nbytes((tm, kdim), BF16) + _nbytes((tm, tn), BF16) + side_bytes,
                _nbytes((kdim, tn), BF16) + 2 * _nbytes((tm, tn), F32))),
        name="matmul_kres",
    )(*operands)
    return outs if side is not None else outs[0]


def _resnorm_kernel(y_ref, x_ref, gpost_ref, gnext_ref, xo_ref, *maybe_h_ref, scale, row_chunk):
    gpost = gpost_ref[...]
    gnext = gnext_ref[...]

    def body(r, carry):
        rows = pl.ds(pl.multiple_of(r * row_chunk, row_chunk), row_chunk)
        y = y_ref[rows, :].astype(F32)
        inv = lax.rsqrt(jnp.mean(y * y, axis=-1, keepdims=True) + NORM_EPS)
        xn = x_ref[rows, :] + scale * (y * inv * gpost)
        xo_ref[rows, :] = xn
        if maybe_h_ref:
            inv2 = lax.rsqrt(jnp.mean(xn * xn, axis=-1, keepdims=True) + NORM_EPS)
            maybe_h_ref[0][rows, :] = (xn * inv2 * gnext).astype(BF16)
        return carry

    lax.fori_loop(0, xo_ref.shape[0] // row_chunk, body, 0)


def resnorm(y, x, g_post, g_next, *, scale, emit_next, tm=512, row_chunk=32):
    t, d = x.shape
    row = pl.BlockSpec((tm, d), lambda i: (i, 0))
    vec = pl.BlockSpec((1, d), lambda i: (0, 0))
    out_shape = [jax.ShapeDtypeStruct((t, d), F32)]
    if emit_next:
        out_shape.append(jax.ShapeDtypeStruct((t, d), BF16))
    pipelined = (_nbytes((tm, d), y.dtype) + 2 * _nbytes((tm, d), F32)
                 + (_nbytes((tm, d), BF16) if emit_next else 0))
    outs = pl.pallas_call(
        functools.partial(_resnorm_kernel, scale=scale, row_chunk=row_chunk),
        grid=(t // tm,),
        in_specs=[row, row, vec, vec],
        out_specs=[row] * len(out_shape),
        out_shape=out_shape,
        compiler_params=pltpu.CompilerParams(
            dimension_semantics=("parallel",),
            vmem_limit_bytes=_vmem_limit(pipelined, 3 * _nbytes((row_chunk, d), F32))),
        name="resnorm",
    )(y, x, g_post.reshape(1, d), g_next.reshape(1, d))
    return outs if emit_next else (outs[0], None)


def _proj_kernel(*refs, n_seg, n_aux, epilogue):
    h_ref = refs[0]
    w_refs = refs[1:1 + n_seg]
    aux_refs = refs[1 + n_seg:1 + n_seg + n_aux]
    side_ref = refs[1 + n_seg + n_aux]
    out_refs = refs[2 + n_seg + n_aux:-1]
    side_o_ref = refs[-1]
    h = h_ref[...]
    ys = [_dot(h, w_ref[...]) for w_ref in w_refs]
    outs = epilogue(ys, [r[...] for r in aux_refs])
    for o_ref, o in zip(out_refs, outs):
        o_ref[...] = o.astype(o_ref.dtype)
    side_o_ref[...] = side_ref[...].astype(side_o_ref.dtype)


def proj(h, w_in, col_starts, width, aux, epilogue, out_dtypes, side, *, tm=1024, tn=256):
    t, d = h.shape
    n_seg = len(col_starts)
    grid = (t // tm, width // tn)
    side_spec, side_shape, side_bytes = _side_cast_specs(side, *grid)

    def w_spec(start):
        off = start // tn
        return pl.BlockSpec((d, tn), lambda i, j: (0, off + j))

    pipelined = (_nbytes((tm, d), BF16) + n_seg * _nbytes((d, tn), BF16)
                 + sum(_nbytes((tm, tn), dt) for dt in out_dtypes) + side_bytes)
    return pl.pallas_call(
        functools.partial(_proj_kernel, n_seg=n_seg, n_aux=len(aux), epilogue=epilogue),
        grid=grid,
        in_specs=([pl.BlockSpec((tm, d), lambda i, j: (i, 0))]
                  + [w_spec(s) for s in col_starts]
                  + [pl.BlockSpec((1, tn), lambda i, j: (0, j)) for _ in aux]
                  + [side_spec]),
        out_specs=[pl.BlockSpec((tm, tn), lambda i, j: (i, j)) for _ in out_dtypes] + [side_spec],
        out_shape=[jax.ShapeDtypeStruct((t, width), dt) for dt in out_dtypes] + [side_shape],
        compiler_params=pltpu.CompilerParams(
            dimension_semantics=("parallel", "arbitrary"),
            vmem_limit_bytes=_vmem_limit(pipelined, (n_seg + 4) * _nbytes((tm, tn), F32))),
        name="proj",
    )(h, *([w_in] * n_seg), *[a.reshape(1, width) for a in aux], side)


def _hgrn_epilogue(ys, aux):
    q, fr, i, g = ys
    lb, = aux
    f = lb + (1.0 - lb) * jax.nn.sigmoid(fr)
    return _silu(q), jnp.log(f) * LOG2_E, 1.0 - f, i, _silu(g)


def _attn_epilogue(ys, aux):
    q, k, v = ys
    return q * (HEAD_DIM ** -0.5), k, v


def _gate_epilogue(ys, aux):
    ga, gb = ys
    ba, bb = aux
    return jax.nn.sigmoid(ga + ba), jax.nn.sigmoid(gb + bb)


def _lower_half_total(p_hb, level, row8):
    c, w = p_hb.shape
    hb = 1 << level
    if 2 * hb >= F32_SUBLANES:
        blk = min(2 * hb, c)
        return jnp.concatenate(
            [jnp.broadcast_to(p_hb[b * blk + hb - 1:b * blk + hb, :], (blk, w)) for b in range(c // blk)], axis=0)
    x = p_hb.reshape(c // F32_SUBLANES, F32_SUBLANES, w)
    odd = (row8 & 1) == 1
    if level == 0:
        g = jnp.where(odd, pltpu.roll(x, 1, 1), x)
    else:
        z = jnp.where(odd, x, pltpu.roll(x, 7, 1))
        g = jnp.where((row8 & 2) == 0, z, pltpu.roll(z, 2, 1))
    return g.reshape(c, w)


def _add_to_upper_halves(p_hb, g, level, upper_mask):
    c = p_hb.shape[0]
    hb = 1 << level
    if hb < F32_SUBLANES:
        return p_hb + jnp.where(upper_mask, g, 0.0)
    pieces = []
    for lo in range(0, c, 2 * hb):
        pieces += [p_hb[lo:lo + hb], p_hb[lo + hb:lo + 2 * hb] + g[lo + hb:lo + 2 * hb]]
    return jnp.concatenate(pieces, axis=0)


def _hgrn_kernel(qs_ref, lf_ref, kk_ref, v_ref, gs_ref, ng_ref, o_ref, st_ref, *, heads_per_block,
                 chunks_per_iter):
    c = CHUNK
    dk = HEAD_DIM
    w = qs_ref.shape[1]
    n_levels = CHUNK.bit_length() - 1

    @pl.when(pl.program_id(2) == 0)
    def _():
        st_ref[...] = jnp.zeros_like(st_ref)

    row = lax.broadcasted_iota(jnp.int32, (c, w), 0)
    row8 = lax.broadcasted_iota(jnp.int32, (c // F32_SUBLANES, F32_SUBLANES, w), 1)
    upper = [((row >> p) & 1) == 1 for p in range(n_levels)]
    ti = lax.broadcasted_iota(jnp.int32, (c, c), 0)
    si = lax.broadcasted_iota(jnp.int32, (c, c), 1)
    diag_mask = ti == si
    level_masks = [((ti >> (p + 1)) == (si >> (p + 1))) & (((ti >> p) & 1) == 1) & (((si >> p) & 1) == 0)
                   for p in range(n_levels)]

    def iter_body(it, carry):
        pre = []
        for cc in range(chunks_per_iter):
            rows = pl.ds(pl.multiple_of((it * chunks_per_iter + cc) * c, c), c)
            p_hb = lf_ref[rows, :]
            e_q, e_k = [], []
            for p in range(n_levels + 1):
                g = _lower_half_total(p_hb, p, row8)
                e_q.append(jnp.exp2(p_hb))
                e_k.append(None if p == 0 else jnp.exp2(g - p_hb))
                if p < n_levels:
                    p_hb = _add_to_upper_halves(p_hb, g, p, upper[p])
            pre.append((rows, e_q, e_k))
        indep = {}
        for cc, (rows, e_q, e_k) in enumerate(pre):
            for hh in range(heads_per_block):
                cols = slice(hh * dk, (hh + 1) * dk)
                qb = qs_ref[rows, cols]
                kb = kk_ref[rows, cols]
                v = v_ref[rows, cols]
                q = qb.astype(F32)
                k = kb.astype(F32)
                parts = [_dot_nt(qb, kb)]
                for p in range(n_levels):
                    qt = (q * e_q[p][:, cols]).astype(BF16)
                    kt = kb if p == 0 else (k * e_k[p][:, cols]).astype(BF16)
                    parts.append(_dot_nt(qt, kt))
                e_in = e_q[n_levels][:, cols]
                e_out = e_k[n_levels][:, cols]
                kv = _dot_tn(v, (k * e_out).astype(BF16))
                indep[cc, hh] = (parts, kv, (q * e_in).astype(BF16), v, e_in[c - 1:c, :])
        for hh in range(heads_per_block):
            cols = slice(hh * dk, (hh + 1) * dk)
            st = st_ref[hh]
            for cc, (rows, _, _) in enumerate(pre):
                parts, kv, q_in, v, decay_all = indep[cc, hh]
                scores = jnp.where(diag_mask, parts[0], 0.0)
                for p in range(n_levels):
                    scores = jnp.where(level_masks[p], parts[p + 1], scores)
                o = _dot(scores.astype(BF16), v) + _dot_nt(q_in, st.astype(BF16))
                st = st * decay_all + kv
                inv = lax.rsqrt(jnp.mean(o * o, axis=-1, keepdims=True) + NORM_EPS)
                o = o * inv * ng_ref[:, cols] * gs_ref[rows, cols].astype(F32)
                o_ref[rows, cols] = o.astype(o_ref.dtype)
            st_ref[hh] = st
        return carry

    lax.fori_loop(0, qs_ref.shape[0] // (c * chunks_per_iter), iter_body, 0)


def hgrn(qs, lf, kk, v, gs, norm_g, *, batch, seq, block_len=1024, heads_per_block=4, chunks_per_iter=8):
    t, width = qs.shape
    bw = heads_per_block * HEAD_DIM
    n_l = seq // block_len
    tok = pl.BlockSpec((block_len, bw), lambda b, h, l: (b * n_l + l, h))
    pipelined = 4 * _nbytes((block_len, bw), BF16) + _nbytes((block_len, bw), F32)
    return pl.pallas_call(
        functools.partial(_hgrn_kernel, heads_per_block=heads_per_block, chunks_per_iter=chunks_per_iter),
        grid=(batch, width // bw, n_l),
        in_specs=[tok, tok, tok, tok, tok,
                  pl.BlockSpec((1, bw), lambda b, h, l: (0, h))],
        out_specs=tok,
        out_shape=jax.ShapeDtypeStruct((t, width), BF16),
        scratch_shapes=[pltpu.VMEM((heads_per_block, HEAD_DIM, HEAD_DIM), F32)],
        compiler_params=pltpu.CompilerParams(
            dimension_semantics=("parallel", "parallel", "arbitrary"),
            vmem_limit_bytes=_vmem_limit(pipelined, 0)),
        name="hgrn",
    )(qs, lf, kk, v, gs, norm_g.reshape(1, width))


ATTN_Q_BLOCK = 2 * CHUNK
ATTN_WINDOW = 640
ATTN_LEAD = ATTN_WINDOW - ATTN_Q_BLOCK
ATTN_BIAS_LANES = pl.cdiv(ATTN_WINDOW + ATTN_Q_BLOCK - 1, LANES) * LANES


def _attn_bias_row(rel_bias):
    m = np.arange(ATTN_BIAS_LANES)
    m = np.where(m >= ATTN_WINDOW, m - ATTN_BIAS_LANES, m)
    idx = np.clip(ATTN_LEAD - m, -MAX_REL, MAX_REL) + MAX_REL
    runs, start = [], 0
    for pos in range(1, len(idx) + 1):
        if pos == len(idx) or idx[pos] != idx[pos - 1] - 1:
            runs.append((start, pos))
            start = pos
    pieces = [jnp.flip(rel_bias[:, int(idx[hi - 1]):int(idx[lo]) + 1], axis=1) for lo, hi in runs]
    return jnp.concatenate(pieces, axis=1)


def _attn_kernel(q_ref, k_ref, v_ref, brow_ref, o_ref, kpad_ref, vpad_ref, *, group):
    seq = q_ref.shape[0]
    kpad_ref[0:ATTN_LEAD, :] = jnp.zeros((ATTN_LEAD, HEAD_DIM), kpad_ref.dtype)
    vpad_ref[0:ATTN_LEAD, :] = jnp.zeros((ATTN_LEAD, HEAD_DIM), vpad_ref.dtype)
    kpad_ref[ATTN_LEAD:, :] = k_ref[...]
    vpad_ref[ATTN_LEAD:, :] = v_ref[...]
    brow = jnp.broadcast_to(brow_ref[0], (ATTN_Q_BLOCK, ATTN_BIAS_LANES))
    toep = pltpu.roll(brow, 0, 1, stride=1, stride_axis=0)[:, :ATTN_WINDOW]
    qi = lax.broadcasted_iota(jnp.int32, (ATTN_Q_BLOCK, ATTN_WINDOW), 0)
    col = lax.broadcasted_iota(jnp.int32, (ATTN_Q_BLOCK, ATTN_WINDOW), 1)
    qc = qi // CHUNK
    kc = col // CHUNK
    lead = ATTN_LEAD // CHUNK
    bias = jnp.where((kc >= qc + lead - LEFT_CHUNKS) & (kc <= qc + lead), toep, MASK_VALUE)

    def do_group(base, masked):
        starts = [base + u * ATTN_Q_BLOCK for u in range(group)]
        scores = []
        for start in starts:
            q = q_ref[pl.ds(start, ATTN_Q_BLOCK), :]
            kw = kpad_ref[pl.ds(start, ATTN_WINDOW), :]
            s = _dot_nt(q, kw) + bias
            if masked:
                s = jnp.where(col >= ATTN_LEAD - start, s, MASK_VALUE)
            scores.append(s)
        probs = []
        for s in scores:
            m = jnp.max(s, axis=-1, keepdims=True)
            p = jnp.exp(s - m)
            probs.append((p.astype(BF16), jnp.sum(p, axis=-1, keepdims=True)))
        for start, (p, denom) in zip(starts, probs):
            vw = vpad_ref[pl.ds(start, ATTN_WINDOW), :]
            o = _dot(p, vw) / denom
            o_ref[pl.ds(start, ATTN_Q_BLOCK), :] = o.astype(o_ref.dtype)

    span = group * ATTN_Q_BLOCK
    n_masked = pl.cdiv(ATTN_LEAD, span)
    for g in range(n_masked):
        do_group(g * span, True)

    def body(g, carry):
        do_group(pl.multiple_of(g * span, span), False)
        return carry

    lax.fori_loop(n_masked, seq // span, body, 0)


def band_attn(q, k, v, bias_row, *, batch, seq, group=16):
    t, width = q.shape
    tok = pl.BlockSpec((seq, HEAD_DIM), lambda b, h: (b, h))
    return pl.pallas_call(
        functools.partial(_attn_kernel, group=group),
        grid=(batch, width // HEAD_DIM),
        in_specs=[tok, tok, tok,
                  pl.BlockSpec((1, 1, ATTN_BIAS_LANES), lambda b, h: (h, 0, 0))],
        out_specs=tok,
        out_shape=jax.ShapeDtypeStruct((t, width), BF16),
        scratch_shapes=[pltpu.VMEM((seq + ATTN_LEAD, HEAD_DIM), BF16),
                        pltpu.VMEM((seq + ATTN_LEAD, HEAD_DIM), BF16)],
        compiler_params=pltpu.CompilerParams(
            dimension_semantics=("parallel", "parallel"),
            vmem_limit_bytes=_vmem_limit(
                4 * _nbytes((seq, HEAD_DIM), BF16),
                2 * _nbytes((seq + ATTN_LEAD, HEAD_DIM), BF16))),
        name="band_attn",
    )(q, k, v, bias_row.reshape(bias_row.shape[0], 1, ATTN_BIAS_LANES))


def _mix_up_kernel(oa_ref, ob_ref, wa_ref, wb_ref, ga_ref, gb_ref, o_ref):
    ya = _dot(oa_ref[...], wa_ref[...])
    yb = _dot(ob_ref[...], wb_ref[...])
    o_ref[...] = (ga_ref[...].astype(F32) * ya + gb_ref[...].astype(F32) * yb).astype(o_ref.dtype)


def mix_up(oa, ob, wa, wb, ga, gb, *, tm=1024, tn=1024):
    t, kdim = oa.shape
    d = wa.shape[1]
    pipelined = (2 * _nbytes((tm, kdim), BF16) + 2 * _nbytes((kdim, tn), BF16) + 3 * _nbytes((tm, tn), BF16))
    return pl.pallas_call(
        _mix_up_kernel,
        grid=(t // tm, d // tn),
        in_specs=[pl.BlockSpec((tm, kdim), lambda i, j: (i, 0)),
                  pl.BlockSpec((tm, kdim), lambda i, j: (i, 0)),
                  pl.BlockSpec((kdim, tn), lambda i, j: (0, j)),
                  pl.BlockSpec((kdim, tn), lambda i, j: (0, j)),
                  pl.BlockSpec((tm, tn), lambda i, j: (i, j)),
                  pl.BlockSpec((tm, tn), lambda i, j: (i, j))],
        out_specs=pl.BlockSpec((tm, tn), lambda i, j: (i, j)),
        out_shape=jax.ShapeDtypeStruct((t, d), BF16),
        compiler_params=pltpu.CompilerParams(
            dimension_semantics=("parallel", "arbitrary"),
            vmem_limit_bytes=_vmem_limit(pipelined, 4 * _nbytes((tm, tn), F32))),
        name="mix_up",
    )(oa, ob, wa, wb, ga, gb)


def _ffn(x, h, w1, w3, w2, post_g, next_g, emit_next, side=None):
    g, w2_b = glu_up(h, w1, w3, w2)
    if side is None:
        y, side_b = matmul_kres(g, w2_b), None
    else:
        y, side_b = matmul_kres(g, w2_b, side)
    x_new, h_next = resnorm(y, x, post_g, next_g, scale=MACARON_WEIGHT, emit_next=emit_next)
    return x_new, h_next, side_b


def kernel(x, ffn1_pre_g, ffn1_post_g, ffn1_w1, ffn1_w3, ffn1_w2, mix_pre_g, mix_post_g, w_in, b_gate,
           hgrn_lb_logits, hgrn_norm_g, rel_bias, w_up_a, w_up_b, w_out,
           ffn2_pre_g, ffn2_post_g, ffn2_w1, ffn2_w3, ffn2_w2):
    batch, seq, d = x.shape
    depth = ffn1_w1.shape[0]
    d_half = d // 2
    lower_bounds = jnp.cumsum(jax.nn.softmax(hgrn_lb_logits.astype(F32), axis=0), axis=0)
    xt = x.reshape(batch * seq, d)
    h = norm_cast(xt, ffn1_pre_g[0])
    for layer in range(depth):
        xt, h, w_in_b = _ffn(xt, h, ffn1_w1[layer], ffn1_w3[layer], ffn1_w2[layer],
                             ffn1_post_g[layer], mix_pre_g[layer], True, side=w_in[layer])

        qs, lf, kk, vv, gs, w_up_a_b = proj(h, w_in_b, [0, d_half, 2 * d_half, 3 * d_half], d_half,
                                            [lower_bounds[layer]], _hgrn_epilogue,
                                            [BF16, F32, BF16, BF16, BF16], w_up_a[layer])
        qb, kb, vb, w_up_b_b = proj(h, w_in_b, [4 * d_half, 5 * d_half, 6 * d_half], d_half,
                                    [], _attn_epilogue, [BF16, BF16, BF16], w_up_b[layer], tn=512)
        ga, gb, w_out_b = proj(h, w_in_b, [7 * d_half, 7 * d_half + d], d,
                               [b_gate[layer, 0], b_gate[layer, 1]], _gate_epilogue, [BF16, BF16],
                               w_out[layer], tn=512)

        oa = hgrn(qs, lf, kk, vv, gs, hgrn_norm_g[layer], batch=batch, seq=seq)
        ob = band_attn(qb, kb, vb, _attn_bias_row(rel_bias[layer].astype(F32)), batch=batch, seq=seq)

        m = mix_up(oa, ob, w_up_a_b, w_up_b_b, ga, gb)
        last = layer == depth - 1
        y = matmul_kres(m, w_out_b, tm=1024)
        xt, h = resnorm(y, xt, mix_post_g[layer], ffn2_pre_g[layer], scale=1.0, emit_next=True)
        next_pre = ffn1_pre_g[layer + 1] if not last else ffn2_pre_g[layer]
        xt, h, _ = _ffn(xt, h, ffn2_w1[layer], ffn2_w3[layer], ffn2_w2[layer],
                        ffn2_post_g[layer], next_pre, not last)
    return xt.reshape(batch, seq, d)
```
